```python
import jax
import jax.numpy as jnp
from jax import lax
import numpy as np

D_MODEL = 1024
BATCH = 32
SEQ = 256
DEPTH = 4
DEC_BATCH = 4
DEC_SEQ = 1024
PAST_LEN = 256

GRID_W = 64
HEAD_DIM = 64
GROUP_W = D_MODEL // 4
MIX_W = 4 * GROUP_W
LRU_W = GROUP_W
LRU_BLOCKS = 4
CONV_W = 4
LRU_C = 8.0
GQA_HEADS = GROUP_W // HEAD_DIM
GQA_KV = GQA_HEADS // 2
NAT_HEADS = GROUP_W // HEAD_DIM
NAT_WR = 8
NAT_WC = 16
RET_HEADS = GROUP_W // HEAD_DIM
RET_CHUNK = 128
D_FF = 11 * D_MODEL // 4
Q_BLOCK = 128
ROPE_BASE = 10000.0
EPS = 1e-6
NEG_INF = -1e30
N_MOD = 9
COL_WIDTHS = (LRU_W, LRU_W,
              GQA_HEADS * HEAD_DIM, GQA_KV * HEAD_DIM, GQA_KV * HEAD_DIM,
              GROUP_W, GROUP_W, GROUP_W,
              GROUP_W, GROUP_W, GROUP_W, GROUP_W)
IN_COLS = 2 * LRU_W + (GQA_HEADS + 2 * GQA_KV) * HEAD_DIM + 3 * GROUP_W + 4 * GROUP_W

kernel_name = 'hybrid_diffusion_prefix_trunk_step'

f32 = jnp.float32


def rmsnorm(x, g):
    xf = x.astype(f32)
    y = xf * lax.rsqrt(jnp.mean(xf * xf, axis=-1, keepdims=True) + EPS)
    return (y * g).astype(x.dtype)


def heads(x, n):
    return x.reshape(*x.shape[:-1], n, HEAD_DIM)


def split_columns(h):
    bounds, acc = [], 0
    for w in COL_WIDTHS[:-1]:
        acc += w
        bounds.append(acc)
    return jnp.split(h, bounds, axis=-1)


def swiglu(x, w_in, w_out):
    a, b = jnp.split(x @ w_in, 2, axis=-1)
    return (jax.nn.silu(a) * b) @ w_out


def adaln(cvec, w_mod, b_mod):
    m = jax.nn.silu(cvec) @ w_mod + b_mod
    return jnp.split(m[:, None, :], N_MOD, axis=-1)


def axial_rope(T):
    t = jnp.arange(T)
    row = (t // GRID_W).astype(f32)
    col = (t % GRID_W).astype(f32)
    n_freq = HEAD_DIM // 4
    inv = ROPE_BASE ** (-jnp.arange(n_freq, dtype=f32) / n_freq)
    ang = jnp.concatenate([row[:, None] * inv, col[:, None] * inv], axis=-1)
    return jnp.cos(ang), jnp.sin(ang)


def apply_rope(x, cos, sin):
    xf = x.astype(f32)
    x1, x2 = xf[..., 0::2], xf[..., 1::2]
    c, s = cos[None, :, None, :], sin[None, :, None, :]
    return jnp.stack([x1 * c - x2 * s, x1 * s + x2 * c], axis=-1).reshape(x.shape).astype(x.dtype)


def gqa_attend(q, k, v):
    B, T, H, D = q.shape
    KV = k.shape[2]
    G = H // KV
    nb = T // Q_BLOCK
    qb = q.reshape(B, nb, Q_BLOCK, KV, G, D).transpose(1, 0, 2, 3, 4, 5)
    scale = D ** -0.5

    def one_block(qblk):
        s = jnp.einsum('bqkgd,bskd->bkgqs', qblk, k, preferred_element_type=f32) * scale
        p = jax.nn.softmax(s, axis=-1).astype(v.dtype)
        return jnp.einsum('bkgqs,bskd->bqkgd', p, v)

    o = lax.map(one_block, qb)
    return o.transpose(1, 0, 2, 3, 4, 5).reshape(B, T, H * D)


def neighbourhood_attend(q, k, v, k_ctx, v_ctx, bias_tab):
    B, T, H, D = q.shape
    R = T // GRID_W
    wr = min(NAT_WR, R)
    rows = jnp.arange(R)
    rs = jnp.clip(rows - wr // 2, 0, R - wr)
    row_idx = rs[:, None] + jnp.arange(wr)[None, :]
    cols = jnp.arange(GRID_W)
    cs = jnp.clip(cols - NAT_WC // 2, 0, GRID_W - NAT_WC)
    in_win = (cols[None, :] >= cs[:, None]) & (cols[None, :] < cs[:, None] + NAT_WC)
    dr = row_idx - rows[:, None] + (NAT_WR - 1)
    dc = jnp.clip(cols[None, :] - cols[:, None] + (NAT_WC - 1), 0, 2 * NAT_WC - 2)
    bias = bias_tab[:, dr[:, None, :, None], dc[None, :, None, :]]
    qg = q.reshape(B, R, GRID_W, H, D)
    kg = k.reshape(B, R, GRID_W, H, D)[:, row_idx]
    vg = v.reshape(B, R, GRID_W, H, D)[:, row_idx]
    scale = D ** -0.5
    s_loc = jnp.einsum('brqhd,brwkhd->bhrqwk', qg, kg, preferred_element_type=f32) * scale
    s_loc = jnp.where(in_win[None, None, None, :, None, :], s_loc + bias[None].astype(f32), NEG_INF)
    s_ctx = jnp.einsum('brqhd,bshd->bhrqs', qg, k_ctx, preferred_element_type=f32) * scale
    n_loc = wr * GRID_W
    p = jax.nn.softmax(jnp.concatenate([s_loc.reshape(B, H, R, GRID_W, n_loc), s_ctx], axis=-1), axis=-1)
    p = p.astype(v.dtype)
    p_loc = p[..., :n_loc].reshape(B, H, R, GRID_W, wr, GRID_W)
    p_ctx = p[..., n_loc:]
    o = (jnp.einsum('bhrqwk,brwkhd->brqhd', p_loc, vg)
         + jnp.einsum('bhrqs,bshd->brqhd', p_ctx, v_ctx))
    return o.reshape(B, T, H * D)


def conv_centred(x, w, b):
    T = x.shape[1]
    left = CONV_W // 2
    xp = jnp.pad(x, ((0, 0), (left, CONV_W - 1 - left), (0, 0)))
    y = xp[:, 0:T] * w[0]
    for i in range(1, CONV_W):
        y = y + xp[:, i:i + T] * w[i]
    return y + b


def _lin_combine(left, right):
    a_l, b_l = left
    a_r, b_r = right
    return a_l * a_r, a_r * b_l + b_r


def rglru_mixer(xa, ga, conv_w, conv_b, w_r, b_r, w_i, b_i, lam, h0):
    xc = conv_centred(xa, conv_w, conv_b).astype(f32)
    B, T, W = xc.shape
    xb = xc.reshape(B, T, LRU_BLOCKS, W // LRU_BLOCKS)
    hs = []
    for d in range(2):
        r = jax.nn.sigmoid(jnp.einsum('btnc,ncd->btnd', xb, w_r[d].astype(f32)).reshape(B, T, W) + b_r[d])
        i = jax.nn.sigmoid(jnp.einsum('btnc,ncd->btnd', xb, w_i[d].astype(f32)).reshape(B, T, W) + b_i[d])
        log_a = -LRU_C * r * jax.nn.softplus(-lam[d].astype(f32))
        a = jnp.exp(log_a)
        u = jnp.sqrt(-jnp.expm1(2.0 * log_a)) * (i * xc)
        A, Bc = lax.associative_scan(_lin_combine, (a, u), axis=1, reverse=(d == 1))
        hs.append(A * h0[:, d, None, :].astype(f32) + Bc)
    y = (hs[0] + hs[1]) * jax.nn.gelu(ga.astype(f32))
    return y.astype(xa.dtype), hs[0][:, -1], hs[1][:, 0]


def retention_scan(q, k, v, log_g, S0):
    B, T, H, D = q.shape
    n = T // RET_CHUNK
    pos = jnp.arange(RET_CHUNK, dtype=f32)
    diff = pos[:, None] - pos[None, :]
    dmat = jnp.where(diff >= 0, jnp.exp(log_g[:, None, None] * jnp.maximum(diff, 0.0)), 0.0)
    xi = jnp.exp(log_g[None, :] * (pos[:, None] + 1.0))
    zeta = jnp.exp(log_g[None, :] * (RET_CHUNK - 1.0 - pos[:, None]))
    g_chunk = jnp.exp(log_g * RET_CHUNK)

    def chunks(t):
        return t.reshape(B, n, RET_CHUNK, H, D).swapaxes(0, 1)

    def step(S, blk):
        qb, kb, vb = blk
        s = jnp.einsum('bqhd,bkhd->bhqk', qb, kb) * dmat
        o = (jnp.einsum('bhqk,bkhe->bqhe', s, vb)
             + jnp.einsum('bqhd,bhde->bqhe', qb, S) * xi[None, :, :, None])
        S = S * g_chunk[None, :, None, None] + jnp.einsum('bkhd,kh,bkhe->bhde', kb, zeta, vb)
        return S, o

    S, o = lax.scan(step, S0, (chunks(q), chunks(k), chunks(v)))
    return o.swapaxes(0, 1).reshape(B, T, H, D), S


def retention_mixer(qd, kd, vd, gd, decay_logit, gn_g, S0):
    B, T, _ = qd.shape
    q = heads(qd, RET_HEADS).astype(f32)
    k = heads(kd, RET_HEADS).astype(f32) * (HEAD_DIM ** -0.5)
    v = heads(vd, RET_HEADS).astype(f32)
    log_g = jax.nn.log_sigmoid(decay_logit.astype(f32))
    o_f, S_f = retention_scan(q, k, v, log_g[0], S0[:, 0].astype(f32))
    o_b, S_b = retention_scan(jnp.flip(q, 1), jnp.flip(k, 1), jnp.flip(v, 1), log_g[1], S0[:, 1].astype(f32))
    o = o_f + jnp.flip(o_b, 1)
    mu = jnp.mean(o, axis=-1, keepdims=True)
    var = jnp.mean(jnp.square(o - mu), axis=-1, keepdims=True)
    o = ((o - mu) * lax.rsqrt(var + EPS)).reshape(B, T, RET_HEADS * HEAD_DIM) * gn_g
    y = o * jax.nn.silu(gd.astype(f32))
    return y.astype(qd.dtype), S_f, S_b


def trunk_layer(x, cvec, lp, ctx):
    is_ctx = ctx is None
    Bn, T, _ = x.shape
    sh1, sc1, g1, sh2, sc2, g2, sh3, sc3, g3 = adaln(cvec, lp['w_mod'], lp['b_mod'])

    h = rmsnorm(x, lp['norm_g'][0]) * (1.0 + sc1) + sh1
    x = x + 0.5 * g1 * swiglu(h, lp['ffn_w_in'][0], lp['ffn_w_out'][0])

    h = rmsnorm(x, lp['norm_g'][1]) * (1.0 + sc2) + sh2
    xa, ga, qb, kb, vb, qc, kc, vc, qd, kd, vd, gd = split_columns(h @ lp['w_in'])

    h0 = jnp.zeros((Bn, 2, LRU_W), f32) if is_ctx else ctx['lru']
    y_a, h_fwd, h_bwd = rglru_mixer(xa, ga, lp['conv_w'], lp['conv_b'], lp['lru_w_r'], lp['lru_b_r'],
                                    lp['lru_w_i'], lp['lru_b_i'], lp['lru_lambda'], h0)
    qb = rmsnorm(heads(qb, GQA_HEADS), lp['gqa_qn'])
    kb = rmsnorm(heads(kb, GQA_KV), lp['gqa_kn'])
    vb = heads(vb, GQA_KV)
    qc = rmsnorm(heads(qc, NAT_HEADS), lp['nat_qn'])
    kc = rmsnorm(heads(kc, NAT_HEADS), lp['nat_kn'])
    vc = heads(vc, NAT_HEADS)
    if is_ctx:
        y_b = gqa_attend(qb, kb, vb)
        y_c = gqa_attend(qc, kc, vc)
        S0 = jnp.zeros((Bn, 2, RET_HEADS, HEAD_DIM, HEAD_DIM), f32)
    else:
        cos, sin = axial_rope(T)
        k_all = jnp.concatenate([apply_rope(kb, cos, sin), ctx['bk']], axis=1)
        v_all = jnp.concatenate([vb, ctx['bv']], axis=1)
        y_b = gqa_attend(apply_rope(qb, cos, sin), k_all, v_all)
        y_c = neighbourhood_attend(qc, kc, vc, ctx['ck'], ctx['cv'], lp['nat_bias'])
        S0 = ctx['ret']
    y_d, S_f, S_b = retention_mixer(qd, kd, vd, gd, lp['ret_decay'], lp['ret_gn'], S0)

    y = jnp.concatenate([y_a, y_b, y_c, y_d], axis=-1) @ lp['w_out']
    x = x + g2 * y

    h = rmsnorm(x, lp['norm_g'][2]) * (1.0 + sc3) + sh3
    x = x + 0.5 * g3 * swiglu(h, lp['ffn_w_in'][1], lp['ffn_w_out'][1])

    if not is_ctx:
        return x, None
    dt = x.dtype
    new = (kb, vb, kc, vc,
           jnp.stack([h_fwd, h_bwd], axis=1).astype(dt),
           jnp.stack([S_f, S_b], axis=1).astype(dt))
    return x, new


def setup_inputs(seed: int = 0) -> dict:
    key = jax.random.key(seed)
    ks = jax.random.split(key, 40)
    nrm = jax.random.normal
    D = D_MODEL
    bw = LRU_W // LRU_BLOCKS
    u = jax.random.uniform(ks[20], (DEPTH, 2, LRU_W), minval=0.9, maxval=0.999)
    s = u ** (1.0 / LRU_C)
    lru_lambda = jnp.log(s) - jnp.log1p(-s)
    gam_logit = jnp.log(2.0 ** (5.0 + jnp.arange(RET_HEADS, dtype=f32)) - 1.0)
    ret_decay = gam_logit[None, None, :] + 0.05 * nrm(ks[21], (DEPTH, 2, RET_HEADS))
    return {
        'x_prompt': nrm(ks[0], (BATCH, SEQ, D)),
        'x_sample': nrm(ks[1], (DEC_BATCH, DEC_SEQ, D)),
        'cache_b_k': nrm(ks[2], (DEC_BATCH, DEPTH, PAST_LEN, GQA_KV, HEAD_DIM)),
        'cache_b_v': nrm(ks[3], (DEC_BATCH, DEPTH, PAST_LEN, GQA_KV, HEAD_DIM)),
        'cache_c_k': nrm(ks[4], (DEC_BATCH, DEPTH, PAST_LEN, NAT_HEADS, HEAD_DIM)),
        'cache_c_v': nrm(ks[5], (DEC_BATCH, DEPTH, PAST_LEN, NAT_HEADS, HEAD_DIM)),
        'state_lru': 0.5 * nrm(ks[6], (DEC_BATCH, DEPTH, 2, LRU_W)),
        'state_ret': 0.1 * nrm(ks[7], (DEC_BATCH, DEPTH, 2, RET_HEADS, HEAD_DIM, HEAD_DIM)),
        'c': nrm(ks[8], (DEC_BATCH, D)),
        'c_ctx': nrm(ks[9], (D,)),
        'w_mod': 0.5 * nrm(ks[10], (DEPTH, D, N_MOD * D)) * D ** -0.5,
        'b_mod': 0.01 * nrm(ks[11], (DEPTH, N_MOD * D)),
        'norm_g': 1.0 + 0.01 * nrm(ks[12], (DEPTH, 3, D)),
        'ffn_w_in': nrm(ks[13], (DEPTH, 2, D, 2 * D_FF)) * D ** -0.5,
        'ffn_w_out': nrm(ks[14], (DEPTH, 2, D_FF, D)) * D_FF ** -0.5,
        'w_in': nrm(ks[15], (DEPTH, D, IN_COLS)) * D ** -0.5,
        'w_out': nrm(ks[16], (DEPTH, MIX_W, D)) * MIX_W ** -0.5,
        'conv_w': nrm(ks[17], (DEPTH, CONV_W, LRU_W)) * CONV_W ** -0.5,
        'conv_b': 0.01 * nrm(ks[18], (DEPTH, LRU_W)),
        'lru_w_r': nrm(ks[19], (DEPTH, 2, LRU_BLOCKS, bw, bw)) * bw ** -0.5,
        'lru_b_r': 0.01 * nrm(ks[22], (DEPTH, 2, LRU_W)),
        'lru_w_i': nrm(ks[23], (DEPTH, 2, LRU_BLOCKS, bw, bw)) * bw ** -0.5,
        'lru_b_i': 0.01 * nrm(ks[24], (DEPTH, 2, LRU_W)),
        'lru_lambda': lru_lambda,
        'gqa_qn': 1.0 + 0.01 * nrm(ks[25], (DEPTH, HEAD_DIM)),
        'gqa_kn': 1.0 + 0.01 * nrm(ks[26], (DEPTH, HEAD_DIM)),
        'nat_qn': 1.0 + 0.01 * nrm(ks[27], (DEPTH, HEAD_DIM)),
        'nat_kn': 1.0 + 0.01 * nrm(ks[28], (DEPTH, HEAD_DIM)),
        'nat_bias': 0.1 * nrm(ks[29], (DEPTH, NAT_HEADS, 2 * NAT_WR - 1, 2 * NAT_WC - 1)),
        'ret_decay': ret_decay,
        'ret_gn': 1.0 + 0.01 * nrm(ks[30], (DEPTH, RET_HEADS * HEAD_DIM)),
    }


def reference(x_prompt, x_sample, cache_b_k, cache_b_v, cache_c_k, cache_c_v, state_lru, state_ret,
              c, c_ctx, w_mod, b_mod, norm_g, ffn_w_in, ffn_w_out, w_in, w_out, conv_w, conv_b,
              lru_w_r, lru_b_r, lru_w_i, lru_b_i, lru_lambda, gqa_qn, gqa_kn, nat_qn, nat_kn,
              nat_bias, ret_decay, ret_gn):
    y_prompt = x_prompt
    y_sample = x_sample
    c_context = c_ctx[None, :]
    bk_l, bv_l, ck_l, cv_l, lru_l, ret_l = [], [], [], [], [], []
    for l in range(DEPTH):
        lp = dict(w_mod=w_mod[l], b_mod=b_mod[l], norm_g=norm_g[l], ffn_w_in=ffn_w_in[l],
                  ffn_w_out=ffn_w_out[l], w_in=w_in[l], w_out=w_out[l], conv_w=conv_w[l],
                  conv_b=conv_b[l], lru_w_r=lru_w_r[l], lru_b_r=lru_b_r[l], lru_w_i=lru_w_i[l],
                  lru_b_i=lru_b_i[l], lru_lambda=lru_lambda[l], gqa_qn=gqa_qn[l], gqa_kn=gqa_kn[l],
                  nat_qn=nat_qn[l], nat_kn=nat_kn[l], nat_bias=nat_bias[l], ret_decay=ret_decay[l],
                  ret_gn=ret_gn[l])
        y_prompt, (bk, bv, ck, cv, hl, sr) = trunk_layer(y_prompt, c_context, lp, None)
        bk_l.append(bk); bv_l.append(bv); ck_l.append(ck); cv_l.append(cv)
        lru_l.append(hl); ret_l.append(sr)
        ctx = dict(bk=cache_b_k[:, l], bv=cache_b_v[:, l], ck=cache_c_k[:, l], cv=cache_c_v[:, l],
                   lru=state_lru[:, l], ret=state_ret[:, l])
        y_sample, _ = trunk_layer(y_sample, c, lp, ctx)
    new_cache_b_k = jnp.stack(bk_l, axis=1)
    new_cache_b_v = jnp.stack(bv_l, axis=1)
    new_cache_c_k = jnp.stack(ck_l, axis=1)
    new_cache_c_v = jnp.stack(cv_l, axis=1)
    new_state_lru = jnp.stack(lru_l, axis=1)
    new_state_ret = jnp.stack(ret_l, axis=1)
    return (y_prompt, y_sample, new_cache_b_k, new_cache_b_v, new_cache_c_k, new_cache_c_v,
            new_state_lru, new_state_ret)
```

```python
import functools
import math

import jax
import jax.numpy as jnp
from jax import lax
from jax.experimental import pallas as pl
from jax.experimental.pallas import tpu as pltpu

f32 = jnp.float32
bf16 = jnp.bfloat16

D = 1024
DEPTH = 4
N_CTX, T_CTX = 32, 256
N_LAT, T_LAT = 4, 1024
M_CTX = N_CTX * T_CTX
M_LAT = N_LAT * T_LAT
M_ALL = M_CTX + M_LAT
PAST = 256
GRID_W = 64
GRID_R = T_LAT // GRID_W
HD = 64
GW = 256
LRU_C = 8.0
NAT_WR, NAT_WC = 8, 16
D_FF = 2816
IN_COLS = 2816
EPS = 1e-6
NEG_INF = -1e30
N_MOD = 9
ATT_SCALE = HD ** -0.5
ROPE_BASE = 10000.0

C_XA, C_GA = 0, 256
C_QB, C_KB, C_VB = 512, 768, 896
C_QC, C_KC, C_VC = 1024, 1280, 1536
C_QD, C_KD, C_VD, C_GD = 1792, 2048, 2304, 2560

FFN_TM = 512
FFN_CHUNKS = ((0, 768), (768, 1536), (1536, 2304), (2304, 2816))
Q_TILE = 256
VMEM_LIMIT = 56 * 1024 * 1024


def _dot(a, b):
    return jnp.dot(a, b, preferred_element_type=f32)


def _dot_nt(a, b):
    return lax.dot_general(a, b, (((1,), (1,)), ((), ())), preferred_element_type=f32)


def _dot_tn(a, b):
    return lax.dot_general(a, b, (((0,), (0,)), ((), ())), preferred_element_type=f32)


def _softplus(z):
    return jnp.maximum(z, 0.0) + jnp.log1p(jnp.exp(-jnp.abs(z)))


def _sigmoid(z):
    return 1.0 / (1.0 + jnp.exp(-z))


def _silu(z):
    return z * _sigmoid(z)


def _gelu_tanh(z):
    c = math.sqrt(2.0 / math.pi)
    return z * (0.5 * (1.0 + jnp.tanh(c * (z + 0.044715 * (z * z * z)))))


def _rms_mod(x, ng, sc, sh):
    ms = jnp.mean(x * x, axis=-1, keepdims=True)
    y = x * lax.rsqrt(ms + EPS) * ng
    return y * (1.0 + sc) + sh


def _headnorm(x, gain):
    T, W = x.shape
    lane = lax.broadcasted_iota(jnp.int32, (T, 128), 1)
    lo = lane < HD
    cols = []
    for c in range(W // 128):
        xc = x[:, c * 128:(c + 1) * 128]
        x2 = xc * xc
        s_lo = jnp.sum(jnp.where(lo, x2, 0.0), axis=-1, keepdims=True)
        s_hi = jnp.sum(jnp.where(lo, 0.0, x2), axis=-1, keepdims=True)
        r_lo = lax.rsqrt(s_lo * (1.0 / HD) + EPS)
        r_hi = lax.rsqrt(s_hi * (1.0 / HD) + EPS)
        cols.append(xc * jnp.where(lo, r_lo, r_hi))
    y = cols[0] if len(cols) == 1 else jnp.concatenate(cols, axis=-1)
    return y * gain


def _rope(x, cos, sin_signed):
    T, W = x.shape
    lane = lax.broadcasted_iota(jnp.int32, (T, 128), 1)
    even = (lane & 1) == 0
    cols = []
    for c in range(W // 128):
        xc = x[:, c * 128:(c + 1) * 128]
        sw = jnp.where(even, pltpu.roll(xc, 127, 1), pltpu.roll(xc, 1, 1))
        sl = slice(c * 128, (c + 1) * 128)
        cols.append(xc * cos[:, sl] + sw * sin_signed[:, sl])
    return cols[0] if len(cols) == 1 else jnp.concatenate(cols, axis=-1)


def _scan(a, b, tpos, reverse):
    T = a.shape[0]
    s = 1
    while s < T:
        if reverse:
            keep = tpos < T - s
            a_s = jnp.where(keep, pltpu.roll(a, T - s, 0), 1.0)
            b_s = jnp.where(keep, pltpu.roll(b, T - s, 0), 0.0)
        else:
            keep = tpos >= s
            a_s = jnp.where(keep, pltpu.roll(a, s, 0), 1.0)
            b_s = jnp.where(keep, pltpu.roll(b, s, 0), 0.0)
        b = a * b_s + b
        if 2 * s < T:
            a = a * a_s
        s *= 2
    return b


def _softmax_pv(s_list, v_list):
    m = s_list[0].max(axis=-1, keepdims=True)
    for s in s_list[1:]:
        m = jnp.maximum(m, s.max(axis=-1, keepdims=True))
    l = None
    o = None
    for s, v in zip(s_list, v_list):
        p = jnp.exp(s - m)
        ls = jnp.sum(p, axis=-1, keepdims=True)
        os_ = _dot(p.astype(bf16), v)
        l = ls if l is None else l + ls
        o = os_ if o is None else o + os_
    return o * (1.0 / l)


def _loop(n, body):
    if n == 1:
        body(0)
    else:
        def step(i, carry):
            body(i)
            return carry
        lax.fori_loop(0, n, step, 0)


def _mod_kernel(c_ref, w_ref, b_ref, o_ref):
    cs = _silu(c_ref[...]).astype(bf16)
    o_ref[...] = _dot(cs, w_ref[...].astype(bf16)) + b_ref[...]


def _modulation(cvec, w_mod, b_mod):
    return pl.pallas_call(
        _mod_kernel,
        grid=(DEPTH, N_MOD),
        in_specs=[
            pl.BlockSpec((8, D), lambda l, j: (0, 0)),
            pl.BlockSpec((None, D, D), lambda l, j: (l, 0, j)),
            pl.BlockSpec((None, None, 1, D), lambda l, j: (l, j, 0, 0)),
        ],
        out_specs=pl.BlockSpec((None, None, 8, D), lambda l, j: (l, j, 0, 0)),
        out_shape=jax.ShapeDtypeStruct((DEPTH, N_MOD, 8, D), f32),
        compiler_params=pltpu.CompilerParams(
            dimension_semantics=("parallel", "parallel"), vmem_limit_bytes=VMEM_LIMIT),
        name="adaln_mod",
    )(cvec, w_mod, b_mod.reshape(DEPTH, N_MOD, 1, D))


def _ffn_kernel(x_ref, sh_ref, sc_ref, gt_ref, ng_ref, wi_ref, wo_ref, o_ref):
    x = x_ref[...]
    h = _rms_mod(x, ng_ref[...], sc_ref[...], sh_ref[...]).astype(bf16)
    acc = None
    for c0, c1 in FFN_CHUNKS:
        a = _dot(h, wi_ref[:, c0:c1])
        b = _dot(h, wi_ref[:, D_FF + c0:D_FF + c1])
        g = (_silu(a) * b).astype(bf16)
        y = _dot(g, wo_ref[c0:c1, :])
        acc = y if acc is None else acc + y
    o_ref[...] = x + (0.5 * gt_ref[...]) * acc


def _mod_row_of_tile(i, rows_per_tile):
    n_ctx = M_CTX // rows_per_tile
    per_b = T_LAT // rows_per_tile
    return jnp.where(i < n_ctx, 0, 1 + jnp.maximum(i - n_ctx, 0) // per_b)


def _ffn(x, mods, norm_g, wi, wo, l, f):
    j0 = 6 * f
    row = functools.partial(_mod_row_of_tile, rows_per_tile=FFN_TM)

    def mod_spec(j):
        return pl.BlockSpec((None, None, None, 1, D), lambda i: (l, j, row(i), 0, 0))

    const = dict(pipeline_mode=pl.Buffered(1))
    return pl.pallas_call(
        _ffn_kernel,
        grid=(M_ALL // FFN_TM,),
        in_specs=[
            pl.BlockSpec((FFN_TM, D), lambda i: (i, 0)),
            mod_spec(j0), mod_spec(j0 + 1), mod_spec(j0 + 2),
            pl.BlockSpec((None, None, 1, D), lambda i: (l, 2 * f, 0, 0)),
            pl.BlockSpec((None, None, D, 2 * D_FF), lambda i: (l, f, 0, 0), **const),
            pl.BlockSpec((None, None, D_FF, D), lambda i: (l, f, 0, 0), **const),
        ],
        out_specs=pl.BlockSpec((FFN_TM, D), lambda i: (i, 0)),
        out_shape=jax.ShapeDtypeStruct((M_ALL, D), f32),
        compiler_params=pltpu.CompilerParams(
            dimension_semantics=("parallel",), vmem_limit_bytes=VMEM_LIMIT),
        name=f"ffn{f}",
    )(x, mods, mods, mods, norm_g, wi, wo)


def _lru_group(h_ref, win_ref, cw_ref, cb_ref, wg_ref, bg_ref, lam_ref, h0, ycat_ref, st_ref):
    T = h_ref.shape[0]
    p = _dot(h_ref[...], win_ref[:, C_XA:C_XA + 2 * GW])
    xa = p[:, :GW]
    ga = p[:, GW:]
    tpos = lax.broadcasted_iota(jnp.int32, (T, GW), 0)

    def shifted(k):
        if k > 0:
            return jnp.where(tpos >= k, pltpu.roll(xa, k, 0), 0.0)
        return jnp.where(tpos < T + k, pltpu.roll(xa, T + k, 0), 0.0)

    cw = cw_ref[...]
    xc = shifted(2) * cw[0:1] + shifted(1) * cw[1:2] + xa * cw[2:3] + shifted(-1) * cw[3:4]
    xc = xc + cb_ref[...]
    pre = _dot(xc.astype(bf16), wg_ref[...]) + bg_ref[...]
    lam = lam_ref[...]
    hsum = None
    for d in range(2):
        r = _sigmoid(pre[:, d * 2 * GW:d * 2 * GW + GW])
        gi = _sigmoid(pre[:, d * 2 * GW + GW:(d + 1) * 2 * GW])
        log_a = (-LRU_C * r) * _softplus(-lam[d:d + 1])
        a = jnp.exp(log_a)
        u = jnp.sqrt(1.0 - a * a) * (gi * xc)
        if h0 is not None:
            edge = (tpos == 0) if d == 0 else (tpos == T - 1)
            u = u + jnp.where(edge, a * h0[d:d + 1], 0.0)
        hd = _scan(a, u, tpos, reverse=(d == 1))
        if st_ref is not None:
            if d == 0:
                st_ref[0:1, :] = hd[T - 1:T, :]
            else:
                st_ref[1:2, :] = hd[0:1, :]
        hsum = hd if hsum is None else hsum + hd
    ycat_ref[:, 0:GW] = hsum * _gelu_tanh(ga)


def _retention_group(h_ref, win_ref, dec_ref, gn_ref, s0_ref, ycat_ref, sret_ref,
                     q_sc, k_sc, v_sc, g_sc):
    T = h_ref.shape[0]
    nt = T // Q_TILE
    p = _dot(h_ref[...], win_ref[:, C_QD:C_QD + 4 * GW])
    q_sc[...] = p[:, 0:GW].astype(bf16)
    kf = p[:, GW:2 * GW] * ATT_SCALE
    k_sc[...] = kf.astype(bf16)
    v_sc[...] = p[:, 2 * GW:3 * GW].astype(bf16)
    g_sc[...] = p[:, 3 * GW:4 * GW]
    gn = gn_ref[...]
    tcol = lax.broadcasted_iota(jnp.int32, (T, 1), 0).astype(f32)
    for h in range(4):
        hs = slice(h * HD, (h + 1) * HD)
        lgf = -_softplus(-dec_ref[h])
        lgb = -_softplus(-dec_ref[4 + h])
        kh = k_sc[:, hs]
        vh = v_sc[:, hs]
        if sret_ref is not None:
            zf = jnp.exp(lgf * ((T - 1.0) - tcol))
            zb = jnp.exp(lgb * tcol)
            kff = kf[:, hs]
            sret_ref[h * HD:(h + 1) * HD, :] = _dot_tn((kff * zf).astype(bf16), vh)
            sret_ref[(4 + h) * HD:(5 + h) * HD, :] = _dot_tn((kff * zb).astype(bf16), vh)
        if s0_ref is not None:
            s0f = s0_ref[h * HD:(h + 1) * HD, :].astype(bf16)
            s0b = s0_ref[(4 + h) * HD:(5 + h) * HD, :].astype(bf16)
        gnh = gn[:, hs]

        def tile(i, hs=hs, lgf=lgf, lgb=lgb, kh=kh, vh=vh, gnh=gnh):
            t0 = pl.multiple_of(i * Q_TILE, Q_TILE)
            rows = pl.ds(t0, Q_TILE)
            qh = q_sc[rows, hs]
            s = _dot_nt(qh, kh)
            ri = lax.broadcasted_iota(jnp.int32, (Q_TILE, T), 0) + t0
            ci = lax.broadcasted_iota(jnp.int32, (Q_TILE, T), 1)
            diff = (ri - ci).astype(f32)
            dm = (jnp.where(diff >= 0, jnp.exp(lgf * jnp.maximum(diff, 0.0)), 0.0)
                  + jnp.where(diff <= 0, jnp.exp(lgb * jnp.maximum(-diff, 0.0)), 0.0))
            o = _dot((s * dm).astype(bf16), vh)
            if s0_ref is not None:
                tq = (lax.broadcasted_iota(jnp.int32, (Q_TILE, 1), 0) + t0).astype(f32)
                o = o + _dot(qh, s0f) * jnp.exp(lgf * (tq + 1.0))
                o = o + _dot(qh, s0b) * jnp.exp(lgb * (T - tq))
            mu = jnp.mean(o, axis=-1, keepdims=True)
            oc = o - mu
            var = jnp.mean(oc * oc, axis=-1, keepdims=True)
            on = oc * lax.rsqrt(var + EPS) * gnh
            ycat_ref[rows, 3 * GW + h * HD:3 * GW + (h + 1) * HD] = on * _silu(g_sc[rows, hs])

        _loop(nt, tile)


def _ctx_mixer_kernel(x_ref, sh_ref, sc_ref, gt_ref, ng_ref, win_ref, wout_ref,
                      cw_ref, cb_ref, wg_ref, bg_ref, lam_ref,
                      qnb_ref, knb_ref, qnc_ref, knc_ref, dec_ref, gn_ref,
                      o_ref, kb_ref, vb_ref, kc_ref, vc_ref, st_ref, sret_ref,
                      h_ref, ycat_ref, q_sc, k_sc, v_sc, g_sc):
    x = x_ref[...]
    h_ref[...] = _rms_mod(x, ng_ref[...], sc_ref[...], sh_ref[...]).astype(bf16)

    _lru_group(h_ref, win_ref, cw_ref, cb_ref, wg_ref, bg_ref, lam_ref, None, ycat_ref, st_ref)

    p = _dot(h_ref[...], win_ref[:, C_QB:C_QB + 2 * GW])
    q = _headnorm(p[:, 0:GW], qnb_ref[...]).astype(bf16)
    kn = _headnorm(p[:, GW:GW + 128], knb_ref[...])
    kb_ref[...] = kn
    vb_ref[...] = p[:, GW + 128:2 * GW]
    k = kn.astype(bf16)
    v = p[:, GW + 128:2 * GW].astype(bf16)
    for hh in range(4):
        kv = hh // 2
        s = _dot_nt(q[:, hh * HD:(hh + 1) * HD], k[:, kv * HD:(kv + 1) * HD]) * ATT_SCALE
        ycat_ref[:, GW + hh * HD:GW + (hh + 1) * HD] = _softmax_pv([s], [v[:, kv * HD:(kv + 1) * HD]])

    p = _dot(h_ref[...], win_ref[:, C_QC:C_QC + 3 * GW])
    q = _headnorm(p[:, 0:GW], qnc_ref[...]).astype(bf16)
    kn = _headnorm(p[:, GW:2 * GW], knc_ref[...])
    kc_ref[...] = kn
    vc_ref[...] = p[:, 2 * GW:3 * GW]
    k = kn.astype(bf16)
    v = p[:, 2 * GW:3 * GW].astype(bf16)
    for hh in range(4):
        hs = slice(hh * HD, (hh + 1) * HD)
        s = _dot_nt(q[:, hs], k[:, hs]) * ATT_SCALE
        ycat_ref[:, 2 * GW + hh * HD:2 * GW + (hh + 1) * HD] = _softmax_pv([s], [v[:, hs]])

    _retention_group(h_ref, win_ref, dec_ref, gn_ref, None, ycat_ref, sret_ref,
                     q_sc, k_sc, v_sc, g_sc)

    y = _dot(ycat_ref[...].astype(bf16), wout_ref[...])
    o_ref[...] = x + gt_ref[...] * y


def _lat_mixer_kernel(x_ref, sh_ref, sc_ref, gt_ref, ng_ref, win_ref, wout_ref,
                      cw_ref, cb_ref, wg_ref, bg_ref, lam_ref,
                      qnb_ref, knb_ref, qnc_ref, knc_ref, dec_ref, gn_ref,
                      cos_ref, sin_ref, bias_ref,
                      cbk_ref, cbv_ref, cck_ref, ccv_ref, lru0_ref, ret0_ref,
                      o_ref,
                      h_ref, ycat_ref, q_sc, k_sc, v_sc, g_sc, kall_sc, vall_sc):
    T = T_LAT
    x = x_ref[...]
    h_ref[...] = _rms_mod(x, ng_ref[...], sc_ref[...], sh_ref[...]).astype(bf16)

    _lru_group(h_ref, win_ref, cw_ref, cb_ref, wg_ref, bg_ref, lam_ref, lru0_ref[...],
               ycat_ref, None)

    p = _dot(h_ref[...], win_ref[:, C_QB:C_QB + 2 * GW])
    cos = cos_ref[...]
    sin = sin_ref[...]
    q_sc[...] = _rope(_headnorm(p[:, 0:GW], qnb_ref[...]), cos, sin).astype(bf16)
    kall_sc[0:T, 0:128] = _rope(_headnorm(p[:, GW:GW + 128], knb_ref[...]),
                                cos[:, 0:128], sin[:, 0:128]).astype(bf16)
    kall_sc[T:T + PAST, 0:128] = cbk_ref[...].astype(bf16)
    vall_sc[0:T, 0:128] = p[:, GW + 128:2 * GW].astype(bf16)
    vall_sc[T:T + PAST, 0:128] = cbv_ref[...].astype(bf16)
    for hh in range(4):
        kv = hh // 2
        k = kall_sc[:, kv * HD:(kv + 1) * HD]
        v = vall_sc[:, kv * HD:(kv + 1) * HD]

        def tile_b(i, hh=hh, k=k, v=v):
            rows = pl.ds(pl.multiple_of(i * Q_TILE, Q_TILE), Q_TILE)
            s = _dot_nt(q_sc[rows, hh * HD:(hh + 1) * HD], k) * ATT_SCALE
            ycat_ref[rows, GW + hh * HD:GW + (hh + 1) * HD] = _softmax_pv([s], [v])

        _loop(T // Q_TILE, tile_b)

    p = _dot(h_ref[...], win_ref[:, C_QC:C_QC + 3 * GW])
    q_sc[...] = _headnorm(p[:, 0:GW], qnc_ref[...]).astype(bf16)
    kall_sc[0:T, :] = _headnorm(p[:, GW:2 * GW], knc_ref[...]).astype(bf16)
    kall_sc[T:T + PAST, :] = cck_ref[...].astype(bf16)
    vall_sc[0:T, :] = p[:, 2 * GW:3 * GW].astype(bf16)
    vall_sc[T:T + PAST, :] = ccv_ref[...].astype(bf16)
    n_loc = NAT_WR * GRID_W
    qcol = lax.broadcasted_iota(jnp.int32, (GRID_W, n_loc), 0)
    kcol = lax.broadcasted_iota(jnp.int32, (GRID_W, n_loc), 1) & (GRID_W - 1)
    cstart = jnp.clip(qcol - NAT_WC // 2, 0, GRID_W - NAT_WC)
    in_win = (kcol >= cstart) & (kcol < cstart + NAT_WC)

    def row_block(r):
        rs = jnp.clip(r - NAT_WR // 2, 0, GRID_R - NAT_WR)
        didx = rs - r + (NAT_WR - 1)
        qrows = pl.ds(pl.multiple_of(r * GRID_W, GRID_W), GRID_W)
        krows = pl.ds(pl.multiple_of(rs * GRID_W, GRID_W), n_loc)
        for hh in range(4):
            hs = slice(hh * HD, (hh + 1) * HD)
            qh = q_sc[qrows, hs]
            s_loc = _dot_nt(qh, kall_sc[krows, hs]) * ATT_SCALE
            s_loc = jnp.where(in_win, s_loc + bias_ref[hh, didx], NEG_INF)
            s_ctx = _dot_nt(qh, kall_sc[T:T + PAST, hs]) * ATT_SCALE
            ycat_ref[qrows, 2 * GW + hh * HD:2 * GW + (hh + 1) * HD] = _softmax_pv(
                [s_loc, s_ctx], [vall_sc[krows, hs], vall_sc[T:T + PAST, hs]])

    _loop(GRID_R, row_block)

    _retention_group(h_ref, win_ref, dec_ref, gn_ref, ret0_ref, ycat_ref, None,
                     q_sc, k_sc, v_sc, g_sc)

    y = _dot(ycat_ref[...].astype(bf16), wout_ref[...])
    o_ref[...] = x + gt_ref[...] * y


def _layer_specs(l, mod_row, grid_rank=1):
    const = dict(pipeline_mode=pl.Buffered(1))

    def lspec(shape):
        nd = len(shape)
        return pl.BlockSpec((None,) + shape, lambda i: (l,) + (0,) * nd, **const)

    def mod_spec(j):
        return pl.BlockSpec((None, None, None, 1, D), lambda i: (l, j, mod_row(i), 0, 0))

    return [
        mod_spec(3), mod_spec(4), mod_spec(5),
        pl.BlockSpec((None, None, 1, D), lambda i: (l, 1, 0, 0)),
        lspec((D, IN_COLS)), lspec((D, D)),
        lspec((4, GW)), lspec((1, GW)), lspec((GW, 4 * GW)), lspec((1, 4 * GW)), lspec((2, GW)),
        lspec((1, GW)), lspec((1, 128)), lspec((1, GW)), lspec((1, GW)),
        lspec((8, 1, 1)), lspec((1, GW)),
    ]


def _ctx_mixer(x, mods, norm_g, lw, l):
    T = T_CTX
    outs = pl.pallas_call(
        _ctx_mixer_kernel,
        grid=(N_CTX,),
        in_specs=[pl.BlockSpec((T, D), lambda i: (i, 0))] + _layer_specs(l, lambda i: 0),
        out_specs=[
            pl.BlockSpec((T, D), lambda i: (i, 0)),
            pl.BlockSpec((None, T, 128), lambda i: (i, 0, 0)),
            pl.BlockSpec((None, T, 128), lambda i: (i, 0, 0)),
            pl.BlockSpec((None, T, GW), lambda i: (i, 0, 0)),
            pl.BlockSpec((None, T, GW), lambda i: (i, 0, 0)),
            pl.BlockSpec((None, 2, GW), lambda i: (i, 0, 0)),
            pl.BlockSpec((None, 8 * HD, HD), lambda i: (i, 0, 0)),
        ],
        out_shape=[
            jax.ShapeDtypeStruct((M_ALL, D), f32),
            jax.ShapeDtypeStruct((N_CTX, T, 128), f32),
            jax.ShapeDtypeStruct((N_CTX, T, 128), f32),
            jax.ShapeDtypeStruct((N_CTX, T, GW), f32),
            jax.ShapeDtypeStruct((N_CTX, T, GW), f32),
            jax.ShapeDtypeStruct((N_CTX, 2, GW), f32),
            jax.ShapeDtypeStruct((N_CTX, 8 * HD, HD), f32),
        ],
        scratch_shapes=[
            pltpu.VMEM((T, D), bf16), pltpu.VMEM((T, D), f32),
            pltpu.VMEM((T, GW), bf16), pltpu.VMEM((T, GW), bf16), pltpu.VMEM((T, GW), bf16),
            pltpu.VMEM((T, GW), f32),
        ],
        input_output_aliases={0: 0},
        compiler_params=pltpu.CompilerParams(
            dimension_semantics=("parallel",), vmem_limit_bytes=VMEM_LIMIT),
        name="ctx_mixer",
    )(x, mods, mods, mods, norm_g, lw["w_in"], lw["w_out"],
      lw["conv_w"], lw["conv_b"], lw["wg"], lw["bg"], lw["lam"],
      lw["qnb"], lw["knb"], lw["qnc"], lw["knc"], lw["dec"], lw["gn"])
    return outs


def _lat_mixer(x, mods, norm_g, lw, l, cos, sin, caches):
    T = T_LAT
    blk0 = M_CTX // T
    const = dict(pipeline_mode=pl.Buffered(1))
    cbk, cbv, cck, ccv, lru0, ret0 = caches

    def cache_spec(w):
        return pl.BlockSpec((None, None, PAST, w), lambda i: (i, l, 0, 0))

    return pl.pallas_call(
        _lat_mixer_kernel,
        grid=(N_LAT,),
        in_specs=[pl.BlockSpec((T, D), lambda i: (blk0 + i, 0))]
        + _layer_specs(l, lambda i: 1 + i)
        + [
            pl.BlockSpec((T, GW), lambda i: (0, 0), **const),
            pl.BlockSpec((T, GW), lambda i: (0, 0), **const),
            pl.BlockSpec((None, 4, NAT_WR, GRID_W, NAT_WR * GRID_W),
                         lambda i: (l, 0, 0, 0, 0), **const),
            cache_spec(128), cache_spec(128), cache_spec(GW), cache_spec(GW),
            pl.BlockSpec((None, None, 2, GW), lambda i: (i, l, 0, 0)),
            pl.BlockSpec((None, None, 8 * HD, HD), lambda i: (i, l, 0, 0)),
        ],
        out_specs=pl.BlockSpec((T, D), lambda i: (blk0 + i, 0)),
        out_shape=jax.ShapeDtypeStruct((M_ALL, D), f32),
        scratch_shapes=[
            pltpu.VMEM((T, D), bf16), pltpu.VMEM((T, D), f32),
            pltpu.VMEM((T, GW), bf16), pltpu.VMEM((T, GW), bf16), pltpu.VMEM((T, GW), bf16),
            pltpu.VMEM((T, GW), f32),
            pltpu.VMEM((T + PAST, GW), bf16), pltpu.VMEM((T + PAST, GW), bf16),
        ],
        input_output_aliases={0: 0},
        compiler_params=pltpu.CompilerParams(
            dimension_semantics=("parallel",), vmem_limit_bytes=VMEM_LIMIT),
        name="lat_mixer",
    )(x, mods, mods, mods, norm_g, lw["w_in"], lw["w_out"],
      lw["conv_w"], lw["conv_b"], lw["wg"], lw["bg"], lw["lam"],
      lw["qnb"], lw["knb"], lw["qnc"], lw["knc"], lw["dec"], lw["gn"],
      cos, sin, lw["nat_bias"], cbk, cbv, cck, ccv, lru0, ret0)


def _block_diag(w):
    eye = jnp.eye(4, dtype=w.dtype)
    return jnp.einsum("lncd,nm->lncmd", w, eye).reshape(DEPTH, GW, GW)


def _rope_tables():
    t = jnp.arange(T_LAT)
    row = (t // GRID_W).astype(f32)
    col = (t % GRID_W).astype(f32)
    n_freq = HD // 4
    inv = ROPE_BASE ** (-jnp.arange(n_freq, dtype=f32) / n_freq)
    ang = jnp.concatenate([row[:, None] * inv, col[:, None] * inv], axis=-1)
    cos = jnp.repeat(jnp.cos(ang), 2, axis=-1)
    sin = jnp.repeat(jnp.sin(ang), 2, axis=-1)
    sign = jnp.where(jnp.arange(HD) % 2 == 0, -1.0, 1.0).astype(f32)
    return jnp.tile(cos, (1, 4)), jnp.tile(sin * sign, (1, 4))


def _nat_bias_windows(nat_bias):
    didx = jnp.arange(NAT_WR)
    w = jnp.arange(NAT_WR)
    dr = didx[:, None] + w[None, :]
    cols = jnp.arange(GRID_W)
    dc = jnp.clip(cols[None, :] - cols[:, None] + (NAT_WC - 1), 0, 2 * NAT_WC - 2)
    tab = nat_bias[:, :, dr[:, None, :, None], dc[None, :, None, :]]
    return tab.reshape(DEPTH, 4, NAT_WR, GRID_W, NAT_WR * GRID_W)


def kernel(x_prompt, x_sample, cache_b_k, cache_b_v, cache_c_k, cache_c_v, state_lru, state_ret, c, c_ctx, w_mod, b_mod, norm_g, ffn_w_in, ffn_w_out, w_in, w_out, conv_w, conv_b, lru_w_r, lru_b_r, lru_w_i, lru_b_i, lru_lambda, gqa_qn, gqa_kn, nat_qn, nat_kn, nat_bias, ret_decay, ret_gn):
    x = jnp.concatenate([x_prompt.reshape(M_CTX, D), x_sample.reshape(M_LAT, D)], axis=0)

    cvec = jnp.concatenate([c_ctx[None, :], c, jnp.zeros((8 - 1 - N_LAT, D), f32)], axis=0)
    mods = _modulation(cvec, w_mod, b_mod).reshape(DEPTH, N_MOD, 8, 1, D)
    ng = norm_g.reshape(DEPTH, 3, 1, D)

    wi = ffn_w_in.astype(bf16)
    wo = ffn_w_out.astype(bf16)
    lw = dict(
        w_in=w_in.astype(bf16),
        w_out=w_out.astype(bf16),
        conv_w=conv_w,
        conv_b=conv_b.reshape(DEPTH, 1, GW),
        wg=jnp.concatenate([_block_diag(lru_w_r[:, 0]), _block_diag(lru_w_i[:, 0]),
                            _block_diag(lru_w_r[:, 1]), _block_diag(lru_w_i[:, 1])],
                           axis=-1).astype(bf16),
        bg=jnp.concatenate([lru_b_r[:, 0], lru_b_i[:, 0], lru_b_r[:, 1], lru_b_i[:, 1]],
                           axis=-1).reshape(DEPTH, 1, 4 * GW),
        lam=lru_lambda,
        qnb=jnp.tile(gqa_qn, (1, 4)).reshape(DEPTH, 1, GW),
        knb=jnp.tile(gqa_kn, (1, 2)).reshape(DEPTH, 1, 128),
        qnc=jnp.tile(nat_qn, (1, 4)).reshape(DEPTH, 1, GW),
        knc=jnp.tile(nat_kn, (1, 4)).reshape(DEPTH, 1, GW),
        dec=ret_decay.reshape(DEPTH, 8, 1, 1),
        gn=ret_gn.reshape(DEPTH, 1, GW),
        nat_bias=_nat_bias_windows(nat_bias),
    )
    cos, sin = _rope_tables()
    caches = (cache_b_k.reshape(N_LAT, DEPTH, PAST, 128), cache_b_v.reshape(N_LAT, DEPTH, PAST, 128),
              cache_c_k.reshape(N_LAT, DEPTH, PAST, GW), cache_c_v.reshape(N_LAT, DEPTH, PAST, GW),
              state_lru, state_ret.reshape(N_LAT, DEPTH, 8 * HD, HD))

    kb_l, vb_l, kc_l, vc_l, st_l, sr_l = [], [], [], [], [], []
    for l in range(DEPTH):
        x = _ffn(x, mods, ng, wi, wo, l, 0)
        x, kb, vb, kc, vc, st, sr = _ctx_mixer(x, mods, ng, lw, l)
        x = _lat_mixer(x, mods, ng, lw, l, cos, sin, caches)
        x = _ffn(x, mods, ng, wi, wo, l, 1)
        kb_l.append(kb); vb_l.append(vb); kc_l.append(kc); vc_l.append(vc)
        st_l.append(st); sr_l.append(sr)

    y_prompt = x[:M_CTX].reshape(N_CTX, T_CTX, D)
    y_sample = x[M_CTX:].reshape(N_LAT, T_LAT, D)
    new_bk = jnp.stack(kb_l, axis=1).reshape(N_CTX, DEPTH, T_CTX, 2, HD)
    new_bv = jnp.stack(vb_l, axis=1).reshape(N_CTX, DEPTH, T_CTX, 2, HD)
    new_ck = jnp.stack(kc_l, axis=1).reshape(N_CTX, DEPTH, T_CTX, 4, HD)
    new_cv = jnp.stack(vc_l, axis=1).reshape(N_CTX, DEPTH, T_CTX, 4, HD)
    new_lru = jnp.stack(st_l, axis=1)
    new_ret = jnp.stack(sr_l, axis=1).reshape(N_CTX, DEPTH, 2, 4, HD, HD)
    return (y_prompt, y_sample, new_bk, new_bv, new_ck, new_cv, new_lru, new_ret)
```

```python
import functools
import math

import jax
import jax.numpy as jnp
from jax import lax
from jax.experimental import pallas as pl
from jax.experimental.pallas import tpu as pltpu

f32 = jnp.float32
bf16 = jnp.bfloat16

D = 1024
DEPTH = 4
N_CTX, T_CTX = 32, 256
N_LAT, T_LAT = 4, 1024
M_CTX = N_CTX * T_CTX
M_LAT = N_LAT * T_LAT
M_ALL = M_CTX + M_LAT
PAST = 256
GRID_W = 64
GRID_R = T_LAT // GRID_W
HD = 64
GW = 256
LRU_C = 8.0
NAT_WR, NAT_WC = 8, 16
D_FF = 2816
IN_COLS = 2816
EPS = 1e-6
NEG_INF = -1e30
N_MOD = 9
ATT_SCALE = HD ** -0.5
ROPE_BASE = 10000.0

C_XA, C_GA = 0, 256
C_QB, C_KB, C_VB = 512, 768, 896
C_QC, C_KC, C_VC = 1024, 1280, 1536
C_QD, C_KD, C_VD, C_GD = 1792, 2048, 2304, 2560

FFN_TM = 512
FFN_CHUNKS = ((0, 768), (768, 1536), (1536, 2304), (2304, 2816))
Q_TILE = 256
CTX_SEQS = 2
VMEM_LIMIT = 56 * 1024 * 1024


def _dot(a, b):
    return jnp.dot(a, b, preferred_element_type=f32)


def _dot_nt(a, b):
    return lax.dot_general(a, b, (((1,), (1,)), ((), ())), preferred_element_type=f32)


def _dot_tn(a, b):
    return lax.dot_general(a, b, (((0,), (0,)), ((), ())), preferred_element_type=f32)


def _softplus(z):
    return jnp.maximum(z, 0.0) + jnp.log1p(jnp.exp(-jnp.abs(z)))


def _sigmoid(z):
    return 1.0 / (1.0 + jnp.exp(-z))


def _silu(z):
    return z * _sigmoid(z)


def _gelu_tanh(z):
    c = math.sqrt(2.0 / math.pi)
    return z * (0.5 * (1.0 + jnp.tanh(c * (z + 0.044715 * (z * z * z)))))


def _rms_mod(x, ng, sc, sh):
    ms = jnp.mean(x * x, axis=-1, keepdims=True)
    y = x * lax.rsqrt(ms + EPS) * ng
    return y * (1.0 + sc) + sh


def _headnorm(x, gain):
    T, W = x.shape
    lane = lax.broadcasted_iota(jnp.int32, (T, 128), 1)
    lo = lane < HD
    cols = []
    for c in range(W // 128):
        xc = x[:, c * 128:(c + 1) * 128]
        x2 = xc * xc
        s_lo = jnp.sum(jnp.where(lo, x2, 0.0), axis=-1, keepdims=True)
        s_hi = jnp.sum(jnp.where(lo, 0.0, x2), axis=-1, keepdims=True)
        r_lo = lax.rsqrt(s_lo * (1.0 / HD) + EPS)
        r_hi = lax.rsqrt(s_hi * (1.0 / HD) + EPS)
        cols.append(xc * jnp.where(lo, r_lo, r_hi))
    y = cols[0] if len(cols) == 1 else jnp.concatenate(cols, axis=-1)
    return y * gain


def _rope(x, cos, sin_signed):
    T, W = x.shape
    lane = lax.broadcasted_iota(jnp.int32, (T, 128), 1)
    even = (lane & 1) == 0
    cols = []
    for c in range(W // 128):
        xc = x[:, c * 128:(c + 1) * 128]
        sw = jnp.where(even, pltpu.roll(xc, 127, 1), pltpu.roll(xc, 1, 1))
        cols.append(xc * cos + sw * sin_signed)
    return cols[0] if len(cols) == 1 else jnp.concatenate(cols, axis=-1)


def _softmax_pv(s_list, v_list):
    m = s_list[0].max(axis=-1, keepdims=True)
    for s in s_list[1:]:
        m = jnp.maximum(m, s.max(axis=-1, keepdims=True))
    l = None
    o = None
    for s, v in zip(s_list, v_list):
        p = jnp.exp(s - m)
        ls = jnp.sum(p, axis=-1, keepdims=True)
        os_ = _dot(p.astype(bf16), v)
        l = ls if l is None else l + ls
        o = os_ if o is None else o + os_
    return o * (1.0 / l)


def _loop(n, body, unroll=1):
    if n == 1:
        body(0)
    else:
        def step(i, carry):
            body(i)
            return carry
        lax.fori_loop(0, n, step, 0, unroll=unroll)


def _mod_kernel(c_ref, w_ref, b_ref, o_ref):
    cs = _silu(c_ref[...]).astype(bf16)
    o_ref[...] = _dot(cs, w_ref[...].astype(bf16)) + b_ref[...]


def _modulation(cvec, w_mod, b_mod):
    return pl.pallas_call(
        _mod_kernel,
        grid=(DEPTH, N_MOD),
        in_specs=[
            pl.BlockSpec((8, D), lambda l, j: (0, 0)),
            pl.BlockSpec((None, D, D), lambda l, j: (l, 0, j)),
            pl.BlockSpec((None, None, 1, D), lambda l, j: (l, j, 0, 0)),
        ],
        out_specs=pl.BlockSpec((None, None, 8, D), lambda l, j: (l, j, 0, 0)),
        out_shape=jax.ShapeDtypeStruct((DEPTH, N_MOD, 8, D), f32),
        compiler_params=pltpu.CompilerParams(
            dimension_semantics=("parallel", "parallel"), vmem_limit_bytes=VMEM_LIMIT),
        name="adaln_mod",
    )(cvec, w_mod, b_mod.reshape(DEPTH, N_MOD, 1, D))


def _ffn_kernel(x_ref, sh_ref, sc_ref, gt_ref, ng_ref, wi_ref, wo_ref, o_ref):
    x = x_ref[...]
    h = _rms_mod(x, ng_ref[...], sc_ref[...], sh_ref[...]).astype(bf16)
    acc = None
    for c0, c1 in FFN_CHUNKS:
        a = _dot(h, wi_ref[:, c0:c1])
        b = _dot(h, wi_ref[:, D_FF + c0:D_FF + c1])
        g = (_silu(a) * b).astype(bf16)
        y = _dot(g, wo_ref[c0:c1, :])
        acc = y if acc is None else acc + y
    o_ref[...] = x + (0.5 * gt_ref[...]) * acc


def _mod_row_of_tile(i, rows_per_tile):
    n_ctx = M_CTX // rows_per_tile
    per_b = T_LAT // rows_per_tile
    return jnp.where(i < n_ctx, 0, 1 + jnp.maximum(i - n_ctx, 0) // per_b)


def _ffn(x, mods, norm_g, wi, wo, l, f):
    j0 = 6 * f
    row = functools.partial(_mod_row_of_tile, rows_per_tile=FFN_TM)

    def mod_spec(j):
        return pl.BlockSpec((None, None, None, 1, D), lambda i: (l, j, row(i), 0, 0))

    const = dict(pipeline_mode=pl.Buffered(1))
    return pl.pallas_call(
        _ffn_kernel,
        grid=(M_ALL // FFN_TM,),
        in_specs=[
            pl.BlockSpec((FFN_TM, D), lambda i: (i, 0)),
            mod_spec(j0), mod_spec(j0 + 1), mod_spec(j0 + 2),
            pl.BlockSpec((None, None, 1, D), lambda i: (l, 2 * f, 0, 0)),
            pl.BlockSpec((None, None, D, 2 * D_FF), lambda i: (l, f, 0, 0), **const),
            pl.BlockSpec((None, None, D_FF, D), lambda i: (l, f, 0, 0), **const),
        ],
        out_specs=pl.BlockSpec((FFN_TM, D), lambda i: (i, 0)),
        out_shape=jax.ShapeDtypeStruct((M_ALL, D), f32),
        compiler_params=pltpu.CompilerParams(
            dimension_semantics=("parallel",), vmem_limit_bytes=VMEM_LIMIT),
        name=f"ffn{f}",
    )(x, mods, mods, mods, norm_g, wi, wo)


def _scan8(a, b, h_init, reverse, emit):
    T, C = a.shape
    nb = T // 8
    row8 = lax.broadcasted_iota(jnp.int32, (8, C), 0)
    for s in (1, 2, 4):
        if reverse:
            a_r = pltpu.roll(a, T - s, 0)
            b_r = pltpu.roll(b, T - s, 0)
            edge = row8 < 8 - s
            a_s = jnp.concatenate([a_r[:T - 8], jnp.where(edge, a_r[T - 8:], 1.0)], axis=0)
            b_s = jnp.concatenate([b_r[:T - 8], jnp.where(edge, b_r[T - 8:], 0.0)], axis=0)
        else:
            a_r = pltpu.roll(a, s, 0)
            b_r = pltpu.roll(b, s, 0)
            edge = row8 >= s
            a_s = jnp.concatenate([jnp.where(edge, a_r[:8], 1.0), a_r[8:]], axis=0)
            b_s = jnp.concatenate([jnp.where(edge, b_r[:8], 0.0), b_r[8:]], axis=0)
        b = a * b_s + b
        a = a * a_s
    h = jnp.broadcast_to(h_init, (8, C))
    for v in (range(nb - 1, -1, -1) if reverse else range(nb)):
        h = a[v * 8:(v + 1) * 8] * h + b[v * 8:(v + 1) * 8]
        emit(v, h)
    return h[0:1] if reverse else h[7:8]


def _lru_group(h_ref, win_ref, cw_ref, cb_ref, wg_ref, bg_ref, lam_ref, h0, ycat_ref, st_ref,
               gate_sc):
    T = h_ref.shape[0]
    p = _dot(h_ref[...], win_ref[:, C_XA:C_XA + 2 * GW])
    xa = p[:, :GW]
    gate_sc[...] = _gelu_tanh(p[:, GW:])
    tpos = lax.broadcasted_iota(jnp.int32, (T, GW), 0)

    def shifted(k):
        if k > 0:
            return jnp.where(tpos >= k, pltpu.roll(xa, k, 0), 0.0)
        return jnp.where(tpos < T + k, pltpu.roll(xa, T + k, 0), 0.0)

    cw = cw_ref[...]
    xc = shifted(2) * cw[0:1] + shifted(1) * cw[1:2] + xa * cw[2:3] + shifted(-1) * cw[3:4]
    xc = xc + cb_ref[...]
    xcb = xc.astype(bf16)
    lam = lam_ref[...]
    for d in range(2):
        ds_ = slice(d * 2 * GW, (d + 1) * 2 * GW)
        pre = _dot(xcb, wg_ref[:, ds_]) + bg_ref[:, ds_]
        r = _sigmoid(pre[:, 0:GW])
        gi = _sigmoid(pre[:, GW:2 * GW])
        log_a = (-LRU_C * r) * _softplus(-lam[d:d + 1])
        a = jnp.exp(log_a)
        u = jnp.sqrt(1.0 - a * a) * (gi * xc)
        h_init = jnp.zeros((1, GW), f32) if h0 is None else h0[d:d + 1]

        def emit_fwd(v, hv):
            ycat_ref[v * 8:(v + 1) * 8, 0:GW] = hv

        def emit_bwd(v, hv):
            rows = slice(v * 8, (v + 1) * 8)
            ycat_ref[rows, 0:GW] = (ycat_ref[rows, 0:GW] + hv) * gate_sc[rows, :]

        last = _scan8(a, u, h_init, d == 1, emit_bwd if d == 1 else emit_fwd)
        if st_ref is not None:
            st_ref[d:d + 1, :] = last


def _build_decay_tiles(dec_ref, dm_sc, nt):
    r = lax.broadcasted_iota(jnp.int32, (Q_TILE, Q_TILE), 0)
    c = lax.broadcasted_iota(jnp.int32, (Q_TILE, Q_TILE), 1)
    base = (r - c).astype(f32)
    for h in range(4):
        lgf = -_softplus(-dec_ref[h])
        lgb = -_softplus(-dec_ref[4 + h])
        for m in range(-(nt - 1), nt):
            diff = base + float(m * Q_TILE)
            if m > 0:
                mat = jnp.exp(lgf * diff)
            elif m < 0:
                mat = jnp.exp(lgb * (-diff))
            else:
                mat = (jnp.where(diff >= 0, jnp.exp(lgf * jnp.maximum(diff, 0.0)), 0.0)
                       + jnp.where(diff <= 0, jnp.exp(lgb * jnp.maximum(-diff, 0.0)), 0.0))
            dm_sc[h * (2 * nt - 1) + m + nt - 1] = mat


def _retention_group(h_ref, win_ref, dec_ref, gn_ref, s0_ref, ycat_ref, sret_ref,
                     q_sc, k_sc, v_sc, g_sc, dm_sc):
    T = h_ref.shape[0]
    nt = T // Q_TILE
    p = _dot(h_ref[...], win_ref[:, C_QD:C_QD + 4 * GW])
    q_sc[...] = p[:, 0:GW].astype(bf16)
    kf = p[:, GW:2 * GW] * ATT_SCALE
    k_sc[...] = kf.astype(bf16)
    v_sc[...] = p[:, 2 * GW:3 * GW].astype(bf16)
    g_sc[...] = p[:, 3 * GW:4 * GW]
    gn = gn_ref[...]
    tcol = lax.broadcasted_iota(jnp.int32, (T, 1), 0).astype(f32)
    for h in range(4):
        hs = slice(h * HD, (h + 1) * HD)
        lgf = -_softplus(-dec_ref[h])
        lgb = -_softplus(-dec_ref[4 + h])
        kh = k_sc[:, hs]
        vh = v_sc[:, hs]
        if sret_ref is not None:
            zf = jnp.exp(lgf * ((T - 1.0) - tcol))
            zb = jnp.exp(lgb * tcol)
            kff = kf[:, hs]
            sret_ref[h * HD:(h + 1) * HD, :] = _dot_tn((kff * zf).astype(bf16), vh)
            sret_ref[(4 + h) * HD:(5 + h) * HD, :] = _dot_tn((kff * zb).astype(bf16), vh)
        if s0_ref is not None:
            s0f = s0_ref[h * HD:(h + 1) * HD, :].astype(bf16)
            s0b = s0_ref[(4 + h) * HD:(5 + h) * HD, :].astype(bf16)
        gnh = gn[:, hs]

        def tile(i, h=h, hs=hs, lgf=lgf, lgb=lgb, kh=kh, vh=vh, gnh=gnh):
            t0 = pl.multiple_of(i * Q_TILE, Q_TILE)
            rows = pl.ds(t0, Q_TILE)
            qh = q_sc[rows, hs]
            o = None
            for j in range(nt):
                ks = slice(j * Q_TILE, (j + 1) * Q_TILE)
                dm = dm_sc[h * (2 * nt - 1) + (nt - 1 - j) + i]
                oj = _dot((_dot_nt(qh, kh[ks]) * dm).astype(bf16), vh[ks])
                o = oj if o is None else o + oj
            if s0_ref is not None:
                tq = (lax.broadcasted_iota(jnp.int32, (Q_TILE, 1), 0) + t0).astype(f32)
                o = o + _dot(qh, s0f) * jnp.exp(lgf * (tq + 1.0))
                o = o + _dot(qh, s0b) * jnp.exp(lgb * (T - tq))
            mu = jnp.mean(o, axis=-1, keepdims=True)
            oc = o - mu
            var = jnp.mean(oc * oc, axis=-1, keepdims=True)
            on = oc * lax.rsqrt(var + EPS) * gnh
            ycat_ref[rows, 3 * GW + h * HD:3 * GW + (h + 1) * HD] = on * _silu(g_sc[rows, hs])

        _loop(nt, tile)


def _ctx_mixer_kernel(x_ref, sh_ref, sc_ref, gt_ref, ng_ref, win_ref, wout_ref,
                      cw_ref, cb_ref, wg_ref, bg_ref, lam_ref,
                      qnb_ref, knb_ref, qnc_ref, knc_ref, dec_ref, gn_ref,
                      o_ref, kb_ref, vb_ref, kc_ref, vc_ref, st_ref, sret_ref,
                      h_ref, ycat_ref, q_sc, k_sc, v_sc, g_sc, gate_sc, dm_sc):
    @pl.when(pl.program_id(0) == 0)
    def _():
        _build_decay_tiles(dec_ref, dm_sc, T_CTX // Q_TILE)

    for g in range(CTX_SEQS):
        _ctx_sequence(x_ref.at[g], sh_ref, sc_ref, gt_ref, ng_ref, win_ref, wout_ref,
                      cw_ref, cb_ref, wg_ref, bg_ref, lam_ref,
                      qnb_ref, knb_ref, qnc_ref, knc_ref, dec_ref, gn_ref,
                      o_ref.at[g], kb_ref.at[g], vb_ref.at[g], kc_ref.at[g], vc_ref.at[g],
                      st_ref.at[g], sret_ref.at[g],
                      h_ref.at[g], ycat_ref.at[g], q_sc.at[g], k_sc.at[g], v_sc.at[g],
                      g_sc.at[g], gate_sc.at[g], dm_sc)


def _ctx_sequence(x_ref, sh_ref, sc_ref, gt_ref, ng_ref, win_ref, wout_ref,
                  cw_ref, cb_ref, wg_ref, bg_ref, lam_ref,
                  qnb_ref, knb_ref, qnc_ref, knc_ref, dec_ref, gn_ref,
                  o_ref, kb_ref, vb_ref, kc_ref, vc_ref, st_ref, sret_ref,
                  h_ref, ycat_ref, q_sc, k_sc, v_sc, g_sc, gate_sc, dm_sc):
    x = x_ref[...]
    h_ref[...] = _rms_mod(x, ng_ref[...], sc_ref[...], sh_ref[...]).astype(bf16)

    _lru_group(h_ref, win_ref, cw_ref, cb_ref, wg_ref, bg_ref, lam_ref, None, ycat_ref, st_ref,
               gate_sc)

    p = _dot(h_ref[...], win_ref[:, C_QB:C_QB + 2 * GW])
    q = _headnorm(p[:, 0:GW], qnb_ref[...] * ATT_SCALE).astype(bf16)
    kn = _headnorm(p[:, GW:GW + 128], knb_ref[...])
    kb_ref[...] = kn
    vb_ref[...] = p[:, GW + 128:2 * GW]
    k = kn.astype(bf16)
    v = p[:, GW + 128:2 * GW].astype(bf16)
    for hh in range(4):
        kv = hh // 2
        s = _dot_nt(q[:, hh * HD:(hh + 1) * HD], k[:, kv * HD:(kv + 1) * HD])
        ycat_ref[:, GW + hh * HD:GW + (hh + 1) * HD] = _softmax_pv([s], [v[:, kv * HD:(kv + 1) * HD]])

    p = _dot(h_ref[...], win_ref[:, C_QC:C_QC + 3 * GW])
    q = _headnorm(p[:, 0:GW], qnc_ref[...] * ATT_SCALE).astype(bf16)
    kn = _headnorm(p[:, GW:2 * GW], knc_ref[...])
    kc_ref[...] = kn
    vc_ref[...] = p[:, 2 * GW:3 * GW]
    k = kn.astype(bf16)
    v = p[:, 2 * GW:3 * GW].astype(bf16)
    for hh in range(4):
        hs = slice(hh * HD, (hh + 1) * HD)
        s = _dot_nt(q[:, hs], k[:, hs])
        ycat_ref[:, 2 * GW + hh * HD:2 * GW + (hh + 1) * HD] = _softmax_pv([s], [v[:, hs]])

    _retention_group(h_ref, win_ref, dec_ref, gn_ref, None, ycat_ref, sret_ref,
                     q_sc, k_sc, v_sc, g_sc, dm_sc)

    y = _dot(ycat_ref[...].astype(bf16), wout_ref[...])
    o_ref[...] = x + gt_ref[...] * y


def _lat_mixer_kernel(x_ref, sh_ref, sc_ref, gt_ref, ng_ref, win_ref, wout_ref,
                      cw_ref, cb_ref, wg_ref, bg_ref, lam_ref,
                      qnb_ref, knb_ref, qnc_ref, knc_ref, dec_ref, gn_ref,
                      cos_ref, sin_ref, bias_ref,
                      cbk_ref, cbv_ref, cck_ref, ccv_ref, lru0_ref, ret0_ref,
                      o_ref,
                      h_ref, ycat_ref, q_sc, k_sc, v_sc, g_sc, gate_sc, dm_sc, kall_sc, vall_sc):
    T = T_LAT

    @pl.when(pl.program_id(0) == 0)
    def _():
        _build_decay_tiles(dec_ref, dm_sc, T // Q_TILE)

    x = x_ref[...]
    h_ref[...] = _rms_mod(x, ng_ref[...], sc_ref[...], sh_ref[...]).astype(bf16)

    _lru_group(h_ref, win_ref, cw_ref, cb_ref, wg_ref, bg_ref, lam_ref, lru0_ref[...],
               ycat_ref, None, gate_sc)

    p = _dot(h_ref[...], win_ref[:, C_QB:C_QB + 2 * GW])
    cos = cos_ref[...]
    sin = sin_ref[...]
    q_sc[...] = _rope(_headnorm(p[:, 0:GW], qnb_ref[...] * ATT_SCALE), cos, sin).astype(bf16)
    kall_sc[0:T, 0:128] = _rope(_headnorm(p[:, GW:GW + 128], knb_ref[...]),
                                cos, sin).astype(bf16)
    kall_sc[T:T + PAST, 0:128] = cbk_ref[...].astype(bf16)
    vall_sc[0:T, 0:128] = p[:, GW + 128:2 * GW].astype(bf16)
    vall_sc[T:T + PAST, 0:128] = cbv_ref[...].astype(bf16)
    for hh in range(4):
        kv = hh // 2
        k = kall_sc[:, kv * HD:(kv + 1) * HD]
        v = vall_sc[:, kv * HD:(kv + 1) * HD]

        def tile_b(i, hh=hh, k=k, v=v):
            rows = pl.ds(pl.multiple_of(i * Q_TILE, Q_TILE), Q_TILE)
            s = _dot_nt(q_sc[rows, hh * HD:(hh + 1) * HD], k)
            ycat_ref[rows, GW + hh * HD:GW + (hh + 1) * HD] = _softmax_pv([s], [v])

        _loop(T // Q_TILE, tile_b)

    p = _dot(h_ref[...], win_ref[:, C_QC:C_QC + 3 * GW])
    q_sc[...] = _headnorm(p[:, 0:GW], qnc_ref[...] * ATT_SCALE).astype(bf16)
    kall_sc[0:T, :] = _headnorm(p[:, GW:2 * GW], knc_ref[...]).astype(bf16)
    kall_sc[T:T + PAST, :] = cck_ref[...].astype(bf16)
    vall_sc[0:T, :] = p[:, 2 * GW:3 * GW].astype(bf16)
    vall_sc[T:T + PAST, :] = ccv_ref[...].astype(bf16)
    n_loc = NAT_WR * GRID_W
    qcol = lax.broadcasted_iota(jnp.int32, (GRID_W, n_loc), 0)
    kcol = lax.broadcasted_iota(jnp.int32, (GRID_W, n_loc), 1) & (GRID_W - 1)
    cstart = jnp.clip(qcol - NAT_WC // 2, 0, GRID_W - NAT_WC)
    in_win = (kcol >= cstart) & (kcol < cstart + NAT_WC)

    def row_block(r):
        rs = jnp.clip(r - NAT_WR // 2, 0, GRID_R - NAT_WR)
        didx = rs - r + (NAT_WR - 1)
        qrows = pl.ds(pl.multiple_of(r * GRID_W, GRID_W), GRID_W)
        krows = pl.ds(pl.multiple_of(rs * GRID_W, GRID_W), n_loc)
        for hh in range(4):
            hs = slice(hh * HD, (hh + 1) * HD)
            qh = q_sc[qrows, hs]
            s_loc = _dot_nt(qh, kall_sc[krows, hs])
            s_loc = jnp.where(in_win, s_loc + bias_ref[hh, didx], NEG_INF)
            s_ctx = _dot_nt(qh, kall_sc[T:T + PAST, hs])
            ycat_ref[qrows, 2 * GW + hh * HD:2 * GW + (hh + 1) * HD] = _softmax_pv(
                [s_loc, s_ctx], [vall_sc[krows, hs], vall_sc[T:T + PAST, hs]])

    _loop(GRID_R, row_block, unroll=2)

    _retention_group(h_ref, win_ref, dec_ref, gn_ref, ret0_ref, ycat_ref, None,
                     q_sc, k_sc, v_sc, g_sc, dm_sc)

    y = _dot(ycat_ref[...].astype(bf16), wout_ref[...])
    o_ref[...] = x + gt_ref[...] * y


def _layer_specs(l, mod_row, grid_rank=1):
    const = dict(pipeline_mode=pl.Buffered(1))

    def lspec(shape):
        nd = len(shape)
        return pl.BlockSpec((None,) + shape, lambda i: (l,) + (0,) * nd, **const)

    def mod_spec(j):
        return pl.BlockSpec((None, None, None, 1, D), lambda i: (l, j, mod_row(i), 0, 0))

    return [
        mod_spec(3), mod_spec(4), mod_spec(5),
        pl.BlockSpec((None, None, 1, D), lambda i: (l, 1, 0, 0)),
        lspec((D, IN_COLS)), lspec((D, D)),
        lspec((4, GW)), lspec((1, GW)), lspec((GW, 4 * GW)), lspec((1, 4 * GW)), lspec((2, GW)),
        lspec((1, GW)), lspec((1, 128)), lspec((1, GW)), lspec((1, GW)),
        lspec((8, 1, 1)), lspec((1, GW)),
    ]


def _ctx_mixer(x, mods, norm_g, lw, l):
    T = T_CTX
    G = CTX_SEQS

    def seq_spec(*tail):
        return pl.BlockSpec((G,) + tail, lambda i: (i,) + (0,) * len(tail))

    outs = pl.pallas_call(
        _ctx_mixer_kernel,
        grid=(N_CTX // G,),
        in_specs=[seq_spec(T, D)] + _layer_specs(l, lambda i: 0),
        out_specs=[
            seq_spec(T, D),
            seq_spec(T, 128), seq_spec(T, 128), seq_spec(T, GW), seq_spec(T, GW),
            seq_spec(2, GW), seq_spec(8 * HD, HD),
        ],
        out_shape=[
            jax.ShapeDtypeStruct((M_ALL // T, T, D), f32),
            jax.ShapeDtypeStruct((N_CTX, T, 128), f32),
            jax.ShapeDtypeStruct((N_CTX, T, 128), f32),
            jax.ShapeDtypeStruct((N_CTX, T, GW), f32),
            jax.ShapeDtypeStruct((N_CTX, T, GW), f32),
            jax.ShapeDtypeStruct((N_CTX, 2, GW), f32),
            jax.ShapeDtypeStruct((N_CTX, 8 * HD, HD), f32),
        ],
        scratch_shapes=[
            pltpu.VMEM((G, T, D), bf16), pltpu.VMEM((G, T, D), f32),
            pltpu.VMEM((G, T, GW), bf16), pltpu.VMEM((G, T, GW), bf16),
            pltpu.VMEM((G, T, GW), bf16),
            pltpu.VMEM((G, T, GW), f32), pltpu.VMEM((G, T, GW), f32),
            pltpu.VMEM((4, Q_TILE, Q_TILE), f32),
        ],
        input_output_aliases={0: 0},
        compiler_params=pltpu.CompilerParams(
            dimension_semantics=("arbitrary",), vmem_limit_bytes=VMEM_LIMIT),
        name="ctx_mixer",
    )(x.reshape(M_ALL // T, T, D), mods, mods, mods, norm_g, lw["w_in"], lw["w_out"],
      lw["conv_w"], lw["conv_b"], lw["wg"], lw["bg"], lw["lam"],
      lw["qnb"], lw["knb"], lw["qnc"], lw["knc"], lw["dec"], lw["gn"])
    return (outs[0].reshape(M_ALL, D),) + tuple(outs[1:])


def _lat_mixer(x, mods, norm_g, lw, l, cos, sin, caches):
    T = T_LAT
    blk0 = M_CTX // T
    const = dict(pipeline_mode=pl.Buffered(1))
    cbk, cbv, cck, ccv, lru0, ret0 = caches

    def cache_spec(w):
        return pl.BlockSpec((None, None, PAST, w), lambda i: (i, l, 0, 0))

    return pl.pallas_call(
        _lat_mixer_kernel,
        grid=(N_LAT,),
        in_specs=[pl.BlockSpec((T, D), lambda i: (blk0 + i, 0), **const)]
        + _layer_specs(l, lambda i: 1 + i)
        + [
            pl.BlockSpec((T, 128), lambda i: (0, 0), **const),
            pl.BlockSpec((T, 128), lambda i: (0, 0), **const),
            pl.BlockSpec((None, 4, NAT_WR, GRID_W, NAT_WR * GRID_W),
                         lambda i: (l, 0, 0, 0, 0), **const),
            cache_spec(128), cache_spec(128), cache_spec(GW), cache_spec(GW),
            pl.BlockSpec((None, None, 2, GW), lambda i: (i, l, 0, 0)),
            pl.BlockSpec((None, None, 8 * HD, HD), lambda i: (i, l, 0, 0)),
        ],
        out_specs=pl.BlockSpec((T, D), lambda i: (blk0 + i, 0), **const),
        out_shape=jax.ShapeDtypeStruct((M_ALL, D), f32),
        scratch_shapes=[
            pltpu.VMEM((T, D), bf16), pltpu.VMEM((T, D), f32),
            pltpu.VMEM((T, GW), bf16), pltpu.VMEM((T, GW), bf16), pltpu.VMEM((T, GW), bf16),
            pltpu.VMEM((T, GW), f32), pltpu.VMEM((T, GW), f32),
            pltpu.VMEM((4 * (2 * (T // Q_TILE) - 1), Q_TILE, Q_TILE), f32),
            pltpu.VMEM((T + PAST, GW), bf16), pltpu.VMEM((T + PAST, GW), bf16),
        ],
        input_output_aliases={0: 0},
        compiler_params=pltpu.CompilerParams(
            dimension_semantics=("arbitrary",), vmem_limit_bytes=VMEM_LIMIT),
        name="lat_mixer",
    )(x, mods, mods, mods, norm_g, lw["w_in"], lw["w_out"],
      lw["conv_w"], lw["conv_b"], lw["wg"], lw["bg"], lw["lam"],
      lw["qnb"], lw["knb"], lw["qnc"], lw["knc"], lw["dec"], lw["gn"],
      cos, sin, lw["nat_bias"], cbk, cbv, cck, ccv, lru0, ret0)


def _block_diag(w):
    eye = jnp.eye(4, dtype=w.dtype)
    return jnp.einsum("lncd,nm->lncmd", w, eye).reshape(DEPTH, GW, GW)


def _rope_tables():
    t = jnp.arange(T_LAT)
    row = (t // GRID_W).astype(f32)
    col = (t % GRID_W).astype(f32)
    n_freq = HD // 4
    inv = ROPE_BASE ** (-jnp.arange(n_freq, dtype=f32) / n_freq)
    ang = jnp.concatenate([row[:, None] * inv, col[:, None] * inv], axis=-1)
    cos = jnp.repeat(jnp.cos(ang), 2, axis=-1)
    sin = jnp.repeat(jnp.sin(ang), 2, axis=-1)
    sign = jnp.where(jnp.arange(HD) % 2 == 0, -1.0, 1.0).astype(f32)
    return jnp.tile(cos, (1, 2)), jnp.tile(sin * sign, (1, 2))


def _nat_bias_windows(nat_bias):
    pad = GRID_W - NAT_WC
    ext = jnp.concatenate([jnp.repeat(nat_bias[..., :1], pad, -1), nat_bias,
                           jnp.repeat(nat_bias[..., -1:], pad + 1, -1)], axis=-1)
    t = jnp.tile(ext, (1, 1, 1, GRID_W))[..., :GRID_W * 127]
    t = t.reshape(DEPTH, 4, 2 * NAT_WR - 1, GRID_W, 127)[..., GRID_W - 1:]
    tab = jnp.stack([t[:, :, d:d + NAT_WR] for d in range(NAT_WR)], axis=2)
    tab = tab.transpose(0, 1, 2, 4, 3, 5)
    return tab.reshape(DEPTH, 4, NAT_WR, GRID_W, NAT_WR * GRID_W)


def kernel(x_prompt, x_sample, cache_b_k, cache_b_v, cache_c_k, cache_c_v, state_lru, state_ret, c, c_ctx, w_mod, b_mod, norm_g, ffn_w_in, ffn_w_out, w_in, w_out, conv_w, conv_b, lru_w_r, lru_b_r, lru_w_i, lru_b_i, lru_lambda, gqa_qn, gqa_kn, nat_qn, nat_kn, nat_bias, ret_decay, ret_gn):
    x = jnp.concatenate([x_prompt.reshape(M_CTX, D), x_sample.reshape(M_LAT, D)], axis=0)

    cvec = jnp.concatenate([c_ctx[None, :], c, jnp.zeros((8 - 1 - N_LAT, D), f32)], axis=0)
    mods = _modulation(cvec, w_mod, b_mod).reshape(DEPTH, N_MOD, 8, 1, D)
    ng = norm_g.reshape(DEPTH, 3, 1, D)

    wi = ffn_w_in.astype(bf16)
    wo = ffn_w_out.astype(bf16)
    lw = dict(
        w_in=w_in.astype(bf16),
        w_out=w_out.astype(bf16),
        conv_w=conv_w,
        conv_b=conv_b.reshape(DEPTH, 1, GW),
        wg=jnp.concatenate([_block_diag(lru_w_r[:, 0]), _block_diag(lru_w_i[:, 0]),
                            _block_diag(lru_w_r[:, 1]), _block_diag(lru_w_i[:, 1])],
                           axis=-1).astype(bf16),
        bg=jnp.concatenate([lru_b_r[:, 0], lru_b_i[:, 0], lru_b_r[:, 1], lru_b_i[:, 1]],
                           axis=-1).reshape(DEPTH, 1, 4 * GW),
        lam=lru_lambda,
        qnb=jnp.tile(gqa_qn, (1, 4)).reshape(DEPTH, 1, GW),
        knb=jnp.tile(gqa_kn, (1, 2)).reshape(DEPTH, 1, 128),
        qnc=jnp.tile(nat_qn, (1, 4)).reshape(DEPTH, 1, GW),
        knc=jnp.tile(nat_kn, (1, 4)).reshape(DEPTH, 1, GW),
        dec=ret_decay.reshape(DEPTH, 8, 1, 1),
        gn=ret_gn.reshape(DEPTH, 1, GW),
        nat_bias=_nat_bias_windows(nat_bias),
    )
    cos, sin = _rope_tables()
    caches = (cache_b_k.reshape(N_LAT, DEPTH, PAST, 128), cache_b_v.reshape(N_LAT, DEPTH, PAST, 128),
              cache_c_k.reshape(N_LAT, DEPTH, PAST, GW), cache_c_v.reshape(N_LAT, DEPTH, PAST, GW),
              state_lru, state_ret.reshape(N_LAT, DEPTH, 8 * HD, HD))

    kb_l, vb_l, kc_l, vc_l, st_l, sr_l = [], [], [], [], [], []
    for l in range(DEPTH):
        x = _ffn(x, mods, ng, wi, wo, l, 0)
        x, kb, vb, kc, vc, st, sr = _ctx_mixer(x, mods, ng, lw, l)
        x = _lat_mixer(x, mods, ng, lw, l, cos, sin, caches)
        x = _ffn(x, mods, ng, wi, wo, l, 1)
        kb_l.append(kb); vb_l.append(vb); kc_l.append(kc); vc_l.append(vc)
        st_l.append(st); sr_l.append(sr)

    y_prompt = x[:M_CTX].reshape(N_CTX, T_CTX, D)
    y_sample = x[M_CTX:].reshape(N_LAT, T_LAT, D)
    new_bk = jnp.stack(kb_l, axis=1).reshape(N_CTX, DEPTH, T_CTX, 2, HD)
    new_bv = jnp.stack(vb_l, axis=1).reshape(N_CTX, DEPTH, T_CTX, 2, HD)
    new_ck = jnp.stack(kc_l, axis=1).reshape(N_CTX, DEPTH, T_CTX, 4, HD)
    new_cv = jnp.stack(vc_l, axis=1).reshape(N_CTX, DEPTH, T_CTX, 4, HD)
    new_lru = jnp.stack(st_l, axis=1)
    new_ret = jnp.stack(sr_l, axis=1).reshape(N_CTX, DEPTH, 2, 4, HD, HD)
    return (y_prompt, y_sample, new_bk, new_bv, new_ck, new_cv, new_lru, new_ret)
```

```python
import functools
import math

import jax
import jax.numpy as jnp
from jax import lax
from jax.experimental import pallas as pl
from jax.experimental.pallas import tpu as pltpu

f32 = jnp.float32
bf16 = jnp.bfloat16

D = 1024
DEPTH = 4
N_CTX, T_CTX = 32, 256
N_LAT, T_LAT = 4, 1024
M_CTX = N_CTX * T_CTX
M_LAT = N_LAT * T_LAT
M_ALL = M_CTX + M_LAT
PAST = 256
GRID_W = 64
GRID_R = T_LAT // GRID_W
HD = 64
GW = 256
LRU_C = 8.0
NAT_WR, NAT_WC = 8, 16
D_FF = 2816
IN_COLS = 2816
EPS = 1e-6
NEG_INF = -1e30
N_MOD = 9
ATT_SCALE = HD ** -0.5
ROPE_BASE = 10000.0

C_XA, C_GA = 0, 256
C_QB, C_KB, C_VB = 512, 768, 896
C_QC, C_KC, C_VC = 1024, 1280, 1536
C_QD, C_KD, C_VD, C_GD = 1792, 2048, 2304, 2560

FFN_TM = 512
FFN_CHUNKS = ((0, 768), (768, 1536), (1536, 2304), (2304, 2816))
Q_TILE = 256
CTX_SEQS = 2
CTX_STAGGER = 16
LOOP_UNROLL = 2
VMEM_LIMIT = 56 * 1024 * 1024


def _dot(a, b):
    return jnp.dot(a, b, preferred_element_type=f32)


def _dot_nt(a, b):
    return lax.dot_general(a, b, (((1,), (1,)), ((), ())), preferred_element_type=f32)


def _dot_tn(a, b):
    return lax.dot_general(a, b, (((0,), (0,)), ((), ())), preferred_element_type=f32)


def _softplus(z):
    return jnp.maximum(z, 0.0) + jnp.log1p(jnp.exp(-jnp.abs(z)))


def _sigmoid(z):
    return 1.0 / (1.0 + jnp.exp(-z))


def _silu(z):
    return z * _sigmoid(z)


def _gelu_tanh(z):
    c = math.sqrt(2.0 / math.pi)
    return z * (0.5 * (1.0 + jnp.tanh(c * (z + 0.044715 * (z * z * z)))))


def _rms_mod(x, ng, sc, sh):
    ms = jnp.mean(x * x, axis=-1, keepdims=True)
    y = x * lax.rsqrt(ms + EPS) * ng
    return y * (1.0 + sc) + sh


def _headnorm(x, gain):
    T, W = x.shape
    lane = lax.broadcasted_iota(jnp.int32, (T, 128), 1)
    lo = lane < HD
    cols = []
    for c in range(W // 128):
        xc = x[:, c * 128:(c + 1) * 128]
        x2 = xc * xc
        s_lo = jnp.sum(jnp.where(lo, x2, 0.0), axis=-1, keepdims=True)
        s_hi = jnp.sum(jnp.where(lo, 0.0, x2), axis=-1, keepdims=True)
        r_lo = lax.rsqrt(s_lo * (1.0 / HD) + EPS)
        r_hi = lax.rsqrt(s_hi * (1.0 / HD) + EPS)
        cols.append(xc * jnp.where(lo, r_lo, r_hi))
    y = cols[0] if len(cols) == 1 else jnp.concatenate(cols, axis=-1)
    return y * gain


def _rope(x, cos, sin_signed):
    T, W = x.shape
    lane = lax.broadcasted_iota(jnp.int32, (T, 128), 1)
    even = (lane & 1) == 0
    cols = []
    for c in range(W // 128):
        xc = x[:, c * 128:(c + 1) * 128]
        sw = jnp.where(even, pltpu.roll(xc, 127, 1), pltpu.roll(xc, 1, 1))
        cols.append(xc * cos + sw * sin_signed)
    return cols[0] if len(cols) == 1 else jnp.concatenate(cols, axis=-1)


def _softmax_pv(s_list, v_list):
    m = s_list[0].max(axis=-1, keepdims=True)
    for s in s_list[1:]:
        m = jnp.maximum(m, s.max(axis=-1, keepdims=True))
    l = None
    o = None
    for s, v in zip(s_list, v_list):
        p = jnp.exp(s - m)
        ls = jnp.sum(p, axis=-1, keepdims=True)
        os_ = _dot(p.astype(bf16), v)
        l = ls if l is None else l + ls
        o = os_ if o is None else o + os_
    return o * (1.0 / l)


def _interleave(*stages):
    live = list(stages)
    while live:
        for g in list(live):
            try:
                next(g)
            except StopIteration:
                live.remove(g)
        yield


def _delayed(stages, n):
    for _ in range(n):
        yield
    yield from stages


def _run(stages):
    for _ in stages:
        pass


def _loop(n, body, unroll=1):
    if n == 1:
        body(0)
    else:
        def step(i, carry):
            body(i)
            return carry
        lax.fori_loop(0, n, step, 0, unroll=unroll)


def _mod_kernel(c_ref, w_ref, b_ref, o_ref):
    cs = _silu(c_ref[...]).astype(bf16)
    o_ref[...] = _dot(cs, w_ref[...].astype(bf16)) + b_ref[...]


def _modulation(cvec, w_mod, b_mod):
    return pl.pallas_call(
        _mod_kernel,
        grid=(DEPTH, N_MOD),
        in_specs=[
            pl.BlockSpec((8, D), lambda l, j: (0, 0)),
            pl.BlockSpec((None, D, D), lambda l, j: (l, 0, j)),
            pl.BlockSpec((None, None, 1, D), lambda l, j: (l, j, 0, 0)),
        ],
        out_specs=pl.BlockSpec((None, None, 8, D), lambda l, j: (l, j, 0, 0)),
        out_shape=jax.ShapeDtypeStruct((DEPTH, N_MOD, 8, D), f32),
        compiler_params=pltpu.CompilerParams(
            dimension_semantics=("parallel", "parallel"), vmem_limit_bytes=VMEM_LIMIT),
        name="adaln_mod",
    )(cvec, w_mod, b_mod.reshape(DEPTH, N_MOD, 1, D))


def _ffn_rows(x, sh_ref, sc_ref, gt_ref, ng_ref, wi_ref, wo_ref):
    h = _rms_mod(x, ng_ref[...], sc_ref[...], sh_ref[...]).astype(bf16)
    acc = None
    for c0, c1 in FFN_CHUNKS:
        a = _dot(h, wi_ref[:, c0:c1])
        b = _dot(h, wi_ref[:, D_FF + c0:D_FF + c1])
        g = (_silu(a) * b).astype(bf16)
        y = _dot(g, wo_ref[c0:c1, :])
        acc = y if acc is None else acc + y
    return x + (0.5 * gt_ref[...]) * acc


def _ffn_kernel(x_ref, sh_ref, sc_ref, gt_ref, ng_ref, wi_ref, wo_ref, o_ref):
    o_ref[...] = _ffn_rows(x_ref[...], sh_ref, sc_ref, gt_ref, ng_ref, wi_ref, wo_ref)


def _ffn_split_in_kernel(xc_ref, xl_ref, sh_ref, sc_ref, gt_ref, ng_ref, wi_ref, wo_ref, o_ref):
    is_ctx = pl.program_id(0) < M_CTX // FFN_TM
    x = jnp.where(is_ctx, xc_ref[...], xl_ref[...])
    o_ref[...] = _ffn_rows(x, sh_ref, sc_ref, gt_ref, ng_ref, wi_ref, wo_ref)


def _ffn_split_out_kernel(x_ref, sh_ref, sc_ref, gt_ref, ng_ref, wi_ref, wo_ref,
                          oc_ref, ol_ref):
    out = _ffn_rows(x_ref[...], sh_ref, sc_ref, gt_ref, ng_ref, wi_ref, wo_ref)
    is_ctx = pl.program_id(0) < M_CTX // FFN_TM

    @pl.when(is_ctx)
    def _():
        oc_ref[...] = out

    @pl.when(jnp.logical_not(is_ctx))
    def _():
        ol_ref[...] = out


def _mod_row_of_tile(i, rows_per_tile):
    n_ctx = M_CTX // rows_per_tile
    per_b = T_LAT // rows_per_tile
    return jnp.where(i < n_ctx, 0, 1 + jnp.maximum(i - n_ctx, 0) // per_b)


def _ffn(xs, mods, norm_g, wi, wo, l, f, split_out=False):
    j0 = 6 * f
    n_ctx = M_CTX // FFN_TM
    row = functools.partial(_mod_row_of_tile, rows_per_tile=FFN_TM)

    def mod_spec(j):
        return pl.BlockSpec((None, None, None, 1, D), lambda i: (l, j, row(i), 0, 0))

    ctx_rows = pl.BlockSpec((FFN_TM, D), lambda i: (jnp.minimum(i, n_ctx - 1), 0))
    lat_rows = pl.BlockSpec((FFN_TM, D), lambda i: (jnp.maximum(i - n_ctx, 0), 0))
    all_rows = pl.BlockSpec((FFN_TM, D), lambda i: (i, 0))
    split_in = isinstance(xs, tuple)
    assert not (split_in and split_out)
    if split_in:
        body, x_specs, xs = _ffn_split_in_kernel, [ctx_rows, lat_rows], list(xs)
    else:
        body, x_specs, xs = (_ffn_split_out_kernel if split_out else _ffn_kernel), [all_rows], [xs]
    if split_out:
        out_specs = [ctx_rows, lat_rows]
        out_shape = [jax.ShapeDtypeStruct((M_CTX, D), f32), jax.ShapeDtypeStruct((M_LAT, D), f32)]
    else:
        out_specs = all_rows
        out_shape = jax.ShapeDtypeStruct((M_ALL, D), f32)

    const = dict(pipeline_mode=pl.Buffered(1))
    return pl.pallas_call(
        body,
        grid=(M_ALL // FFN_TM,),
        in_specs=x_specs + [
            mod_spec(j0), mod_spec(j0 + 1), mod_spec(j0 + 2),
            pl.BlockSpec((None, None, 1, D), lambda i: (l, 2 * f, 0, 0)),
            pl.BlockSpec((None, None, D, 2 * D_FF), lambda i: (l, f, 0, 0), **const),
            pl.BlockSpec((None, None, D_FF, D), lambda i: (l, f, 0, 0), **const),
        ],
        out_specs=out_specs,
        out_shape=out_shape,
        compiler_params=pltpu.CompilerParams(
            dimension_semantics=("arbitrary",), vmem_limit_bytes=VMEM_LIMIT),
        name=f"ffn{f}",
    )(*xs, mods, mods, mods, norm_g, wi, wo)


def _scan8(a, b, h_init, reverse, emit):
    T, C = a.shape
    nb = T // 8
    row8 = lax.broadcasted_iota(jnp.int32, (8, C), 0)
    for s in (1, 2, 4):
        if reverse:
            a_r = pltpu.roll(a, T - s, 0)
            b_r = pltpu.roll(b, T - s, 0)
            edge = row8 < 8 - s
            a_s = jnp.concatenate([a_r[:T - 8], jnp.where(edge, a_r[T - 8:], 1.0)], axis=0)
            b_s = jnp.concatenate([b_r[:T - 8], jnp.where(edge, b_r[T - 8:], 0.0)], axis=0)
        else:
            a_r = pltpu.roll(a, s, 0)
            b_r = pltpu.roll(b, s, 0)
            edge = row8 >= s
            a_s = jnp.concatenate([jnp.where(edge, a_r[:8], 1.0), a_r[8:]], axis=0)
            b_s = jnp.concatenate([jnp.where(edge, b_r[:8], 0.0), b_r[8:]], axis=0)
        b = a * b_s + b
        a = a * a_s
        yield
    h = jnp.broadcast_to(h_init, (8, C))
    for n, v in enumerate(range(nb - 1, -1, -1) if reverse else range(nb)):
        h = a[v * 8:(v + 1) * 8] * h + b[v * 8:(v + 1) * 8]
        emit(v, h)
        if n % 4 == 3:
            yield
    return h[0:1] if reverse else h[7:8]


def _lru_group(h_ref, win_ref, cw_ref, cb_ref, wg_ref, bg_ref, lam_ref, h0, ycat_ref, st_ref,
               gate_sc):
    T = h_ref.shape[0]
    p = _dot(h_ref[...], win_ref[:, C_XA:C_XA + 2 * GW])
    xa = p[:, :GW]
    yield
    gate_sc[...] = _gelu_tanh(p[:, GW:])
    yield
    tpos = lax.broadcasted_iota(jnp.int32, (T, GW), 0)

    def shifted(k):
        if k > 0:
            return jnp.where(tpos >= k, pltpu.roll(xa, k, 0), 0.0)
        return jnp.where(tpos < T + k, pltpu.roll(xa, T + k, 0), 0.0)

    cw = cw_ref[...]
    xc = shifted(2) * cw[0:1] + shifted(1) * cw[1:2] + xa * cw[2:3] + shifted(-1) * cw[3:4]
    xc = xc + cb_ref[...]
    xcb = xc.astype(bf16)
    lam = lam_ref[...]
    yield
    for d in range(2):
        ds_ = slice(d * 2 * GW, (d + 1) * 2 * GW)
        pre = _dot(xcb, wg_ref[:, ds_]) + bg_ref[:, ds_]
        yield
        r = _sigmoid(pre[:, 0:GW])
        gi = _sigmoid(pre[:, GW:2 * GW])
        yield
        log_a = (-LRU_C * r) * _softplus(-lam[d:d + 1])
        a = jnp.exp(log_a)
        u = jnp.sqrt(1.0 - a * a) * (gi * xc)
        yield
        h_init = jnp.zeros((1, GW), f32) if h0 is None else h0[d:d + 1]

        def emit_fwd(v, hv):
            ycat_ref[v * 8:(v + 1) * 8, 0:GW] = hv

        def emit_bwd(v, hv):
            rows = slice(v * 8, (v + 1) * 8)
            ycat_ref[rows, 0:GW] = (ycat_ref[rows, 0:GW] + hv) * gate_sc[rows, :]

        last = yield from _scan8(a, u, h_init, d == 1, emit_bwd if d == 1 else emit_fwd)
        if st_ref is not None:
            st_ref[d:d + 1, :] = last
        yield


def _build_decay_tiles(dec_ref, dm_sc, nt):
    r = lax.broadcasted_iota(jnp.int32, (Q_TILE, Q_TILE), 0)
    c = lax.broadcasted_iota(jnp.int32, (Q_TILE, Q_TILE), 1)
    base = (r - c).astype(f32)
    for h in range(4):
        lgf = -_softplus(-dec_ref[h])
        lgb = -_softplus(-dec_ref[4 + h])
        for m in range(-(nt - 1), nt):
            diff = base + float(m * Q_TILE)
            if m > 0:
                mat = jnp.exp(lgf * diff)
            elif m < 0:
                mat = jnp.exp(lgb * (-diff))
            else:
                mat = (jnp.where(diff >= 0, jnp.exp(lgf * jnp.maximum(diff, 0.0)), 0.0)
                       + jnp.where(diff <= 0, jnp.exp(lgb * jnp.maximum(-diff, 0.0)), 0.0))
            dm_sc[h * (2 * nt - 1) + m + nt - 1] = mat


def _retention_group(h_ref, win_ref, dec_ref, gn_ref, s0_ref, ycat_ref, sret_ref,
                     q_sc, k_sc, v_sc, g_sc, dm_sc):
    T = h_ref.shape[0]
    nt = T // Q_TILE
    p = _dot(h_ref[...], win_ref[:, C_QD:C_QD + 4 * GW])
    yield
    q_sc[...] = p[:, 0:GW].astype(bf16)
    kf = p[:, GW:2 * GW] * ATT_SCALE
    k_sc[...] = kf.astype(bf16)
    v_sc[...] = p[:, 2 * GW:3 * GW].astype(bf16)
    g_sc[...] = p[:, 3 * GW:4 * GW]
    gn = gn_ref[...]
    tcol = lax.broadcasted_iota(jnp.int32, (T, 1), 0).astype(f32)
    yield
    for h in range(4):
        hs = slice(h * HD, (h + 1) * HD)
        lgf = -_softplus(-dec_ref[h])
        lgb = -_softplus(-dec_ref[4 + h])
        kh = k_sc[:, hs]
        vh = v_sc[:, hs]
        if sret_ref is not None:
            zf = jnp.exp(lgf * ((T - 1.0) - tcol))
            zb = jnp.exp(lgb * tcol)
            kff = kf[:, hs]
            sret_ref[h * HD:(h + 1) * HD, :] = _dot_tn((kff * zf).astype(bf16), vh)
            sret_ref[(4 + h) * HD:(5 + h) * HD, :] = _dot_tn((kff * zb).astype(bf16), vh)
            yield
        if s0_ref is not None:
            s0f = s0_ref[h * HD:(h + 1) * HD, :].astype(bf16)
            s0b = s0_ref[(4 + h) * HD:(5 + h) * HD, :].astype(bf16)
        gnh = gn[:, hs]

        def tile(i, h=h, hs=hs, lgf=lgf, lgb=lgb, kh=kh, vh=vh, gnh=gnh):
            t0 = pl.multiple_of(i * Q_TILE, Q_TILE)
            rows = pl.ds(t0, Q_TILE)
            qh = q_sc[rows, hs]
            o = None
            for j in range(nt):
                ks = slice(j * Q_TILE, (j + 1) * Q_TILE)
                dm = dm_sc[h * (2 * nt - 1) + (nt - 1 - j) + i]
                sd = (_dot_nt(qh, kh[ks]) * dm).astype(bf16)
                yield
                oj = _dot(sd, vh[ks])
                o = oj if o is None else o + oj
                yield
            if s0_ref is not None:
                tq = (lax.broadcasted_iota(jnp.int32, (Q_TILE, 1), 0) + t0).astype(f32)
                o = o + _dot(qh, s0f) * jnp.exp(lgf * (tq + 1.0))
                o = o + _dot(qh, s0b) * jnp.exp(lgb * (T - tq))
            mu = jnp.mean(o, axis=-1, keepdims=True)
            oc = o - mu
            var = jnp.mean(oc * oc, axis=-1, keepdims=True)
            yield
            on = oc * lax.rsqrt(var + EPS) * gnh
            ycat_ref[rows, 3 * GW + h * HD:3 * GW + (h + 1) * HD] = on * _silu(g_sc[rows, hs])
            yield

        if nt == 1:
            yield from tile(0)
        else:
            _loop(nt, lambda i, tile=tile: _run(tile(i)), unroll=LOOP_UNROLL)


N_CTX_IN = 18


def _ctx_mixer_kernel(*refs):
    n_out = 7
    n_scratch = 8
    ins = refs[:N_CTX_IN]
    outs = refs[len(refs) - n_out - n_scratch:]
    _ctx_mixer_body(*ins, *outs)


def _ctx_mixer_body(x_ref, sh_ref, sc_ref, gt_ref, ng_ref, win_ref, wout_ref,
                    cw_ref, cb_ref, wg_ref, bg_ref, lam_ref,
                    qnb_ref, knb_ref, qnc_ref, knc_ref, dec_ref, gn_ref,
                    o_ref, kb_ref, vb_ref, kc_ref, vc_ref, st_ref, sret_ref,
                    h_ref, ycat_ref, q_sc, k_sc, v_sc, g_sc, gate_sc, dm_sc):
    @pl.when(pl.program_id(0) == 0)
    def _():
        _build_decay_tiles(dec_ref, dm_sc, T_CTX // Q_TILE)

    _run(_interleave(*[
        _delayed(_ctx_sequence(
            x_ref.at[g], sh_ref, sc_ref, gt_ref, ng_ref, win_ref, wout_ref,
            cw_ref, cb_ref, wg_ref, bg_ref, lam_ref,
            qnb_ref, knb_ref, qnc_ref, knc_ref, dec_ref, gn_ref,
            o_ref.at[g], kb_ref.at[g], vb_ref.at[g], kc_ref.at[g], vc_ref.at[g],
            st_ref.at[g], sret_ref.at[g],
            h_ref.at[g], ycat_ref.at[g], q_sc.at[g], k_sc.at[g], v_sc.at[g],
            g_sc.at[g], gate_sc.at[g], dm_sc), g * CTX_STAGGER)
        for g in range(CTX_SEQS)]))


def _ctx_sequence(x_ref, sh_ref, sc_ref, gt_ref, ng_ref, win_ref, wout_ref,
                  cw_ref, cb_ref, wg_ref, bg_ref, lam_ref,
                  qnb_ref, knb_ref, qnc_ref, knc_ref, dec_ref, gn_ref,
                  o_ref, kb_ref, vb_ref, kc_ref, vc_ref, st_ref, sret_ref,
                  h_ref, ycat_ref, q_sc, k_sc, v_sc, g_sc, gate_sc, dm_sc):
    x = x_ref[...]
    h_ref[...] = _rms_mod(x, ng_ref[...], sc_ref[...], sh_ref[...]).astype(bf16)

    yield
    yield from _interleave(
        _lru_group(h_ref, win_ref, cw_ref, cb_ref, wg_ref, bg_ref, lam_ref, None, ycat_ref,
                   st_ref, gate_sc),
        _ctx_attention_group(h_ref, win_ref, C_QB, 2, qnb_ref, knb_ref, kb_ref, vb_ref,
                             ycat_ref, GW),
        _ctx_attention_group(h_ref, win_ref, C_QC, 4, qnc_ref, knc_ref, kc_ref, vc_ref,
                             ycat_ref, 2 * GW),
        _retention_group(h_ref, win_ref, dec_ref, gn_ref, None, ycat_ref, sret_ref,
                         q_sc, k_sc, v_sc, g_sc, dm_sc),
    )
    gt = gt_ref[...]
    for c in range(D // GW):
        cs = slice(c * GW, (c + 1) * GW)
        y = _dot(ycat_ref[...].astype(bf16), wout_ref[:, cs])
        o_ref[:, cs] = x_ref[:, cs] + gt[:, cs] * y
        yield


def _ctx_attention_group(h_ref, win_ref, col, n_kv, qn_ref, kn_ref, k_out_ref, v_out_ref,
                         ycat_ref, ycol):
    kw = n_kv * HD
    p = _dot(h_ref[...], win_ref[:, col:col + GW + 2 * kw])
    yield
    q = _headnorm(p[:, 0:GW], qn_ref[...] * ATT_SCALE).astype(bf16)
    yield
    kn = _headnorm(p[:, GW:GW + kw], kn_ref[...])
    k_out_ref[...] = kn
    v_out_ref[...] = p[:, GW + kw:GW + 2 * kw]
    k = kn.astype(bf16)
    v = p[:, GW + kw:GW + 2 * kw].astype(bf16)
    yield
    for hh in range(4):
        kv = hh * n_kv // 4
        s = _dot_nt(q[:, hh * HD:(hh + 1) * HD], k[:, kv * HD:(kv + 1) * HD])
        yield
        pr = jnp.exp(s - s.max(axis=-1, keepdims=True))
        l = jnp.sum(pr, axis=-1, keepdims=True)
        yield
        o = _dot(pr.astype(bf16), v[:, kv * HD:(kv + 1) * HD]) * (1.0 / l)
        ycat_ref[:, ycol + hh * HD:ycol + (hh + 1) * HD] = o
        yield


def _lat_mixer_kernel(x_ref, sh_ref, sc_ref, gt_ref, ng_ref, win_ref, wout_ref,
                      cw_ref, cb_ref, wg_ref, bg_ref, lam_ref,
                      qnb_ref, knb_ref, qnc_ref, knc_ref, dec_ref, gn_ref,
                      cos_ref, sin_ref, bias_ref,
                      cbk_ref, cbv_ref, cck_ref, ccv_ref, lru0_ref, ret0_ref,
                      o_ref,
                      h_ref, ycat_ref, q_sc, k_sc, v_sc, g_sc, gate_sc, dm_sc, kall_sc, vall_sc):
    T = T_LAT

    @pl.when(pl.program_id(0) == 0)
    def _():
        _build_decay_tiles(dec_ref, dm_sc, T // Q_TILE)

    x = x_ref[...]
    h_ref[...] = _rms_mod(x, ng_ref[...], sc_ref[...], sh_ref[...]).astype(bf16)

    _run(_lru_group(h_ref, win_ref, cw_ref, cb_ref, wg_ref, bg_ref, lam_ref, lru0_ref[...],
                    ycat_ref, None, gate_sc))

    p = _dot(h_ref[...], win_ref[:, C_QB:C_QB + 2 * GW])
    cos = cos_ref[...]
    sin = sin_ref[...]
    q_sc[...] = _rope(_headnorm(p[:, 0:GW], qnb_ref[...] * ATT_SCALE), cos, sin).astype(bf16)
    kall_sc[0:T, 0:128] = _rope(_headnorm(p[:, GW:GW + 128], knb_ref[...]),
                                cos, sin).astype(bf16)
    kall_sc[T:T + PAST, 0:128] = cbk_ref[...].astype(bf16)
    vall_sc[0:T, 0:128] = p[:, GW + 128:2 * GW].astype(bf16)
    vall_sc[T:T + PAST, 0:128] = cbv_ref[...].astype(bf16)
    for hh in range(4):
        kv = hh // 2
        k = kall_sc[:, kv * HD:(kv + 1) * HD]
        v = vall_sc[:, kv * HD:(kv + 1) * HD]

        def tile_b(i, hh=hh, k=k, v=v):
            rows = pl.ds(pl.multiple_of(i * Q_TILE, Q_TILE), Q_TILE)
            s = _dot_nt(q_sc[rows, hh * HD:(hh + 1) * HD], k)
            ycat_ref[rows, GW + hh * HD:GW + (hh + 1) * HD] = _softmax_pv([s], [v])

        _loop(T // Q_TILE, tile_b, unroll=LOOP_UNROLL)

    p = _dot(h_ref[...], win_ref[:, C_QC:C_QC + 3 * GW])
    q_sc[...] = _headnorm(p[:, 0:GW], qnc_ref[...] * ATT_SCALE).astype(bf16)
    kall_sc[0:T, :] = _headnorm(p[:, GW:2 * GW], knc_ref[...]).astype(bf16)
    kall_sc[T:T + PAST, :] = cck_ref[...].astype(bf16)
    vall_sc[0:T, :] = p[:, 2 * GW:3 * GW].astype(bf16)
    vall_sc[T:T + PAST, :] = ccv_ref[...].astype(bf16)
    n_loc = NAT_WR * GRID_W
    qcol = lax.broadcasted_iota(jnp.int32, (GRID_W, n_loc), 0)
    kcol = lax.broadcasted_iota(jnp.int32, (GRID_W, n_loc), 1) & (GRID_W - 1)
    cstart = jnp.clip(qcol - NAT_WC // 2, 0, GRID_W - NAT_WC)
    in_win = (kcol >= cstart) & (kcol < cstart + NAT_WC)

    def row_block(r):
        rs = jnp.clip(r - NAT_WR // 2, 0, GRID_R - NAT_WR)
        didx = rs - r + (NAT_WR - 1)
        qrows = pl.ds(pl.multiple_of(r * GRID_W, GRID_W), GRID_W)
        krows = pl.ds(pl.multiple_of(rs * GRID_W, GRID_W), n_loc)
        for hh in range(4):
            hs = slice(hh * HD, (hh + 1) * HD)
            qh = q_sc[qrows, hs]
            s_loc = _dot_nt(qh, kall_sc[krows, hs])
            s_loc = jnp.where(in_win, s_loc + bias_ref[hh, didx], NEG_INF)
            s_ctx = _dot_nt(qh, kall_sc[T:T + PAST, hs])
            ycat_ref[qrows, 2 * GW + hh * HD:2 * GW + (hh + 1) * HD] = _softmax_pv(
                [s_loc, s_ctx], [vall_sc[krows, hs], vall_sc[T:T + PAST, hs]])

    _loop(GRID_R, row_block, unroll=LOOP_UNROLL)

    _run(_retention_group(h_ref, win_ref, dec_ref, gn_ref, ret0_ref, ycat_ref, None,
                          q_sc, k_sc, v_sc, g_sc, dm_sc))

    y = _dot(ycat_ref[...].astype(bf16), wout_ref[...])
    o_ref[...] = x + gt_ref[...] * y


def _layer_specs(l, mod_row, grid_rank=1):
    const = dict(pipeline_mode=pl.Buffered(1))

    def lspec(shape):
        nd = len(shape)
        return pl.BlockSpec((None,) + shape, lambda i: (l,) + (0,) * nd, **const)

    def mod_spec(j):
        return pl.BlockSpec((None, None, None, 1, D), lambda i: (l, j, mod_row(i), 0, 0))

    return [
        mod_spec(3), mod_spec(4), mod_spec(5),
        pl.BlockSpec((None, None, 1, D), lambda i: (l, 1, 0, 0)),
        lspec((D, IN_COLS)), lspec((D, D)),
        lspec((4, GW)), lspec((1, GW)), lspec((GW, 4 * GW)), lspec((1, 4 * GW)), lspec((2, GW)),
        lspec((1, GW)), lspec((1, 128)), lspec((1, GW)), lspec((1, GW)),
        lspec((8, 1, 1)), lspec((1, GW)),
    ]


def _ctx_mixer(x, mods, norm_g, lw, l, prev):
    T = T_CTX
    G = CTX_SEQS

    def seq_spec(*tail):
        return pl.BlockSpec((G,) + tail, lambda i: (i,) + (0,) * len(tail))

    def layer_spec(*tail):
        return pl.BlockSpec((G, None) + tail, lambda i: (i, l) + (0,) * len(tail))

    tails = [(T, 128), (T, 128), (T, GW), (T, GW), (2, GW), (8 * HD, HD)]
    operands = [x.reshape(M_ALL // T, T, D), mods, mods, mods, norm_g, lw["w_in"], lw["w_out"],
                lw["conv_w"], lw["conv_b"], lw["wg"], lw["bg"], lw["lam"],
                lw["qnb"], lw["knb"], lw["qnc"], lw["knc"], lw["dec"], lw["gn"]]
    assert len(operands) == N_CTX_IN
    aliases = {0: 0}
    prev_specs = []
    if prev is not None:
        prev_specs = [pl.BlockSpec(memory_space=pl.ANY)] * len(prev)
        aliases.update({N_CTX_IN + k: 1 + k for k in range(len(prev))})
        operands = operands + list(prev)

    outs = pl.pallas_call(
        _ctx_mixer_kernel,
        grid=(N_CTX // G,),
        in_specs=[seq_spec(T, D)] + _layer_specs(l, lambda i: 0) + prev_specs,
        out_specs=[seq_spec(T, D)] + [layer_spec(*t) for t in tails],
        out_shape=[jax.ShapeDtypeStruct((M_ALL // T, T, D), f32)]
        + [jax.ShapeDtypeStruct((N_CTX, DEPTH) + t, f32) for t in tails],
        scratch_shapes=[
            pltpu.VMEM((G, T, D), bf16), pltpu.VMEM((G, T, D), f32),
            pltpu.VMEM((G, T, GW), bf16), pltpu.VMEM((G, T, GW), bf16),
            pltpu.VMEM((G, T, GW), bf16),
            pltpu.VMEM((G, T, GW), f32), pltpu.VMEM((G, T, GW), f32),
            pltpu.VMEM((4, Q_TILE, Q_TILE), f32),
        ],
        input_output_aliases=aliases,
        compiler_params=pltpu.CompilerParams(
            dimension_semantics=("arbitrary",), vmem_limit_bytes=VMEM_LIMIT),
        name="ctx_mixer",
    )(*operands)
    return outs[0].reshape(M_ALL, D), tuple(outs[1:])


def _lat_mixer(x, mods, norm_g, lw, l, cos, sin, caches):
    T = T_LAT
    blk0 = M_CTX // T
    const = dict(pipeline_mode=pl.Buffered(1))
    cbk, cbv, cck, ccv, lru0, ret0 = caches

    def cache_spec(w):
        return pl.BlockSpec((None, None, PAST, w), lambda i: (i, l, 0, 0))

    return pl.pallas_call(
        _lat_mixer_kernel,
        grid=(N_LAT,),
        in_specs=[pl.BlockSpec((T, D), lambda i: (blk0 + i, 0), **const)]
        + _layer_specs(l, lambda i: 1 + i)
        + [
            pl.BlockSpec((T, 128), lambda i: (0, 0), **const),
            pl.BlockSpec((T, 128), lambda i: (0, 0), **const),
            pl.BlockSpec((None, 4, NAT_WR, GRID_W, NAT_WR * GRID_W),
                         lambda i: (l, 0, 0, 0, 0), **const),
            cache_spec(128), cache_spec(128), cache_spec(GW), cache_spec(GW),
            pl.BlockSpec((None, None, 2, GW), lambda i: (i, l, 0, 0)),
            pl.BlockSpec((None, None, 8 * HD, HD), lambda i: (i, l, 0, 0)),
        ],
        out_specs=pl.BlockSpec((T, D), lambda i: (blk0 + i, 0), **const),
        out_shape=jax.ShapeDtypeStruct((M_ALL, D), f32),
        scratch_shapes=[
            pltpu.VMEM((T, D), bf16), pltpu.VMEM((T, D), f32),
            pltpu.VMEM((T, GW), bf16), pltpu.VMEM((T, GW), bf16), pltpu.VMEM((T, GW), bf16),
            pltpu.VMEM((T, GW), f32), pltpu.VMEM((T, GW), f32),
            pltpu.VMEM((4 * (2 * (T // Q_TILE) - 1), Q_TILE, Q_TILE), f32),
            pltpu.VMEM((T + PAST, GW), bf16), pltpu.VMEM((T + PAST, GW), bf16),
        ],
        input_output_aliases={0: 0},
        compiler_params=pltpu.CompilerParams(
            dimension_semantics=("arbitrary",), vmem_limit_bytes=VMEM_LIMIT),
        name="lat_mixer",
    )(x, mods, mods, mods, norm_g, lw["w_in"], lw["w_out"],
      lw["conv_w"], lw["conv_b"], lw["wg"], lw["bg"], lw["lam"],
      lw["qnb"], lw["knb"], lw["qnc"], lw["knc"], lw["dec"], lw["gn"],
      cos, sin, lw["nat_bias"], cbk, cbv, cck, ccv, lru0, ret0)


def _block_diag(w):
    eye = jnp.eye(4, dtype=w.dtype)
    return jnp.einsum("lncd,nm->lncmd", w, eye).reshape(DEPTH, GW, GW)


def _rope_tables():
    t = jnp.arange(T_LAT)
    row = (t // GRID_W).astype(f32)
    col = (t % GRID_W).astype(f32)
    n_freq = HD // 4
    inv = ROPE_BASE ** (-jnp.arange(n_freq, dtype=f32) / n_freq)
    ang = jnp.concatenate([row[:, None] * inv, col[:, None] * inv], axis=-1)
    cos = jnp.repeat(jnp.cos(ang), 2, axis=-1)
    sin = jnp.repeat(jnp.sin(ang), 2, axis=-1)
    sign = jnp.where(jnp.arange(HD) % 2 == 0, -1.0, 1.0).astype(f32)
    return jnp.tile(cos, (1, 2)), jnp.tile(sin * sign, (1, 2))


def _nat_bias_windows(nat_bias):
    pad = GRID_W - NAT_WC
    ext = jnp.concatenate([jnp.repeat(nat_bias[..., :1], pad, -1), nat_bias,
                           jnp.repeat(nat_bias[..., -1:], pad + 1, -1)], axis=-1)
    t = jnp.tile(ext, (1, 1, 1, GRID_W))[..., :GRID_W * 127]
    t = t.reshape(DEPTH, 4, 2 * NAT_WR - 1, GRID_W, 127)[..., GRID_W - 1:]
    tab = jnp.stack([t[:, :, d:d + NAT_WR] for d in range(NAT_WR)], axis=2)
    tab = tab.transpose(0, 1, 2, 4, 3, 5)
    return tab.reshape(DEPTH, 4, NAT_WR, GRID_W, NAT_WR * GRID_W)


def kernel(x_prompt, x_sample, cache_b_k, cache_b_v, cache_c_k, cache_c_v, state_lru, state_ret, c, c_ctx, w_mod, b_mod, norm_g, ffn_w_in, ffn_w_out, w_in, w_out, conv_w, conv_b, lru_w_r, lru_b_r, lru_w_i, lru_b_i, lru_lambda, gqa_qn, gqa_kn, nat_qn, nat_kn, nat_bias, ret_decay, ret_gn):
    cvec = jnp.concatenate([c_ctx[None, :], c, jnp.zeros((8 - 1 - N_LAT, D), f32)], axis=0)
    mods = _modulation(cvec, w_mod, b_mod).reshape(DEPTH, N_MOD, 8, 1, D)
    ng = norm_g.reshape(DEPTH, 3, 1, D)

    wi = ffn_w_in.astype(bf16)
    wo = ffn_w_out.astype(bf16)
    lw = dict(
        w_in=w_in.astype(bf16),
        w_out=w_out.astype(bf16),
        conv_w=conv_w,
        conv_b=conv_b.reshape(DEPTH, 1, GW),
        wg=jnp.concatenate([_block_diag(lru_w_r[:, 0]), _block_diag(lru_w_i[:, 0]),
                            _block_diag(lru_w_r[:, 1]), _block_diag(lru_w_i[:, 1])],
                           axis=-1).astype(bf16),
        bg=jnp.concatenate([lru_b_r[:, 0], lru_b_i[:, 0], lru_b_r[:, 1], lru_b_i[:, 1]],
                           axis=-1).reshape(DEPTH, 1, 4 * GW),
        lam=lru_lambda,
        qnb=jnp.tile(gqa_qn, (1, 4)).reshape(DEPTH, 1, GW),
        knb=jnp.tile(gqa_kn, (1, 2)).reshape(DEPTH, 1, 128),
        qnc=jnp.tile(nat_qn, (1, 4)).reshape(DEPTH, 1, GW),
        knc=jnp.tile(nat_kn, (1, 4)).reshape(DEPTH, 1, GW),
        dec=ret_decay.reshape(DEPTH, 8, 1, 1),
        gn=ret_gn.reshape(DEPTH, 1, GW),
        nat_bias=_nat_bias_windows(nat_bias),
    )
    cos, sin = _rope_tables()
    caches = (cache_b_k.reshape(N_LAT, DEPTH, PAST, 128), cache_b_v.reshape(N_LAT, DEPTH, PAST, 128),
              cache_c_k.reshape(N_LAT, DEPTH, PAST, GW), cache_c_v.reshape(N_LAT, DEPTH, PAST, GW),
              state_lru, state_ret.reshape(N_LAT, DEPTH, 8 * HD, HD))

    x = (x_prompt.reshape(M_CTX, D), x_sample.reshape(M_LAT, D))
    new = None
    for l in range(DEPTH):
        x = _ffn(x, mods, ng, wi, wo, l, 0)
        x, new = _ctx_mixer(x, mods, ng, lw, l, new)
        x = _lat_mixer(x, mods, ng, lw, l, cos, sin, caches)
        x = _ffn(x, mods, ng, wi, wo, l, 1, split_out=(l == DEPTH - 1))

    new_bk, new_bv, new_ck, new_cv, new_lru, new_ret = new
    return (x[0].reshape(N_CTX, T_CTX, D), x[1].reshape(N_LAT, T_LAT, D),
            new_bk.reshape(N_CTX, DEPTH, T_CTX, 2, HD), new_bv.reshape(N_CTX, DEPTH, T_CTX, 2, HD),
            new_ck.reshape(N_CTX, DEPTH, T_CTX, 4, HD), new_cv.reshape(N_CTX, DEPTH, T_CTX, 4, HD),
            new_lru, new_ret.reshape(N_CTX, DEPTH, 2, 4, HD, HD))
```

```python
import functools
import math

import jax
import jax.numpy as jnp
from jax import lax
from jax.experimental import pallas as pl
from jax.experimental.pallas import tpu as pltpu

f32 = jnp.float32
bf16 = jnp.bfloat16

D = 1024
DEPTH = 4
N_CTX, T_CTX = 32, 256
N_LAT, T_LAT = 4, 1024
M_CTX = N_CTX * T_CTX
M_LAT = N_LAT * T_LAT
M_ALL = M_CTX + M_LAT
PAST = 256
GRID_W = 64
GRID_R = T_LAT // GRID_W
HD = 64
GW = 256
LRU_C = 8.0
NAT_WR, NAT_WC = 8, 16
D_FF = 2816
IN_COLS = 2816
EPS = 1e-6
NEG_INF = -1e30
N_MOD = 9
ATT_SCALE = HD ** -0.5
ROPE_BASE = 10000.0

C_XA, C_GA = 0, 256
C_QB, C_KB, C_VB = 512, 768, 896
C_QC, C_KC, C_VC = 1024, 1280, 1536
C_QD, C_KD, C_VD, C_GD = 1792, 2048, 2304, 2560

FFN_TM = 1024
FFN_SUB = 512
FFN_CHUNKS = ((0, 768), (768, 1536), (1536, 2304), (2304, 2816))
Q_TILE = 256
LOOP_UNROLL = 2
VMEM_LIMIT = 56 * 1024 * 1024


def _dot(a, b):
    return jnp.dot(a, b, preferred_element_type=f32)


def _dot_nt(a, b):
    return lax.dot_general(a, b, (((1,), (1,)), ((), ())), preferred_element_type=f32)


def _dot_tn(a, b):
    return lax.dot_general(a, b, (((0,), (0,)), ((), ())), preferred_element_type=f32)


def _softplus(z):
    return jnp.maximum(z, 0.0) + jnp.log1p(jnp.exp(-jnp.abs(z)))


def _sigmoid(z):
    return 1.0 / (1.0 + jnp.exp(-z))


def _silu(z):
    return z * _sigmoid(z)


def _gelu_tanh(z):
    c = math.sqrt(2.0 / math.pi)
    return z * (0.5 * (1.0 + jnp.tanh(c * (z + 0.044715 * (z * z * z)))))


def _rms_mod(x, ng, sc, sh):
    ms = jnp.mean(x * x, axis=-1, keepdims=True)
    y = x * lax.rsqrt(ms + EPS) * ng
    return y * (1.0 + sc) + sh


def _headnorm(x, gain):
    T, W = x.shape
    lane = lax.broadcasted_iota(jnp.int32, (T, 128), 1)
    lo = lane < HD
    cols = []
    for c in range(W // 128):
        xc = x[:, c * 128:(c + 1) * 128]
        x2 = xc * xc
        s_lo = jnp.sum(jnp.where(lo, x2, 0.0), axis=-1, keepdims=True)
        s_hi = jnp.sum(jnp.where(lo, 0.0, x2), axis=-1, keepdims=True)
        r_lo = lax.rsqrt(s_lo * (1.0 / HD) + EPS)
        r_hi = lax.rsqrt(s_hi * (1.0 / HD) + EPS)
        cols.append(xc * jnp.where(lo, r_lo, r_hi))
    y = cols[0] if len(cols) == 1 else jnp.concatenate(cols, axis=-1)
    return y * gain


def _rope(x, cos, sin_signed):
    T, W = x.shape
    lane = lax.broadcasted_iota(jnp.int32, (T, 128), 1)
    even = (lane & 1) == 0
    cols = []
    for c in range(W // 128):
        xc = x[:, c * 128:(c + 1) * 128]
        sw = jnp.where(even, pltpu.roll(xc, 127, 1), pltpu.roll(xc, 1, 1))
        cols.append(xc * cos + sw * sin_signed)
    return cols[0] if len(cols) == 1 else jnp.concatenate(cols, axis=-1)


def _attend_stages(scores, v_list, store):
    s_list = scores()
    yield
    m = s_list[0].max(axis=-1, keepdims=True)
    for s in s_list[1:]:
        m = jnp.maximum(m, s.max(axis=-1, keepdims=True))
    yield
    p_list = [jnp.exp(s - m) for s in s_list]
    l = None
    for p in p_list:
        ls = jnp.sum(p, axis=-1, keepdims=True)
        l = ls if l is None else l + ls
    yield
    o = None
    for p, v in zip(p_list, v_list):
        os_ = _dot(p.astype(bf16), v)
        o = os_ if o is None else o + os_
    store(o * (1.0 / l))
    yield


def _interleave(*stages):
    live = list(stages)
    while live:
        for g in list(live):
            try:
                next(g)
            except StopIteration:
                live.remove(g)
        yield


def _delayed(stages, n):
    for _ in range(n):
        yield
    yield from stages


def _run(stages):
    for _ in stages:
        pass


def _loop(n, body, unroll=1):
    if n == 1:
        body(0)
    else:
        def step(i, carry):
            body(i)
            return carry
        lax.fori_loop(0, n, step, 0, unroll=unroll)


def _mod_kernel(c_ref, w_ref, b_ref, o_ref):
    cs = _silu(c_ref[...]).astype(bf16)
    o_ref[...] = _dot(cs, w_ref[...].astype(bf16)) + b_ref[...]


def _modulation(cvec, w_mod, b_mod):
    return pl.pallas_call(
        _mod_kernel,
        grid=(DEPTH, N_MOD),
        in_specs=[
            pl.BlockSpec((8, D), lambda l, j: (0, 0)),
            pl.BlockSpec((None, D, D), lambda l, j: (l, 0, j)),
            pl.BlockSpec((None, None, 1, D), lambda l, j: (l, j, 0, 0)),
        ],
        out_specs=pl.BlockSpec((None, None, 8, D), lambda l, j: (l, j, 0, 0)),
        out_shape=jax.ShapeDtypeStruct((DEPTH, N_MOD, 8, D), f32),
        compiler_params=pltpu.CompilerParams(
            dimension_semantics=("parallel", "parallel"), vmem_limit_bytes=VMEM_LIMIT),
        name="adaln_mod",
    )(cvec, w_mod, b_mod.reshape(DEPTH, N_MOD, 1, D))


def _ffn_rows(load, store, sh_ref, sc_ref, gt_ref, ng_ref, wi_ref, wo_ref):
    x = load()
    h = _rms_mod(x, ng_ref[...], sc_ref[...], sh_ref[...]).astype(bf16)
    yield
    acc = None
    for c0, c1 in FFN_CHUNKS:
        a = _dot(h, wi_ref[:, c0:c1])
        b = _dot(h, wi_ref[:, D_FF + c0:D_FF + c1])
        g = (_silu(a) * b).astype(bf16)
        y = _dot(g, wo_ref[c0:c1, :])
        acc = y if acc is None else acc + y
        yield
    store(x + (0.5 * gt_ref[...]) * acc)


_FFN_PASSES = tuple(slice(r, r + FFN_SUB) for r in range(0, FFN_TM, FFN_SUB))


def _ffn_passes(load, store, *refs):
    _run(_interleave(*[
        _delayed(_ffn_rows(functools.partial(load, rows), functools.partial(store, rows), *refs), k)
        for k, rows in enumerate(_FFN_PASSES)]))


def _ffn_kernel(x_ref, sh_ref, sc_ref, gt_ref, ng_ref, wi_ref, wo_ref, o_ref):
    def store(rows, v):
        o_ref[rows] = v

    _ffn_passes(lambda rows: x_ref[rows], store, sh_ref, sc_ref, gt_ref, ng_ref, wi_ref, wo_ref)


def _ffn_split_in_kernel(xc_ref, xl_ref, sh_ref, sc_ref, gt_ref, ng_ref, wi_ref, wo_ref, o_ref):
    is_ctx = pl.program_id(0) < M_CTX // FFN_TM

    def store(rows, v):
        o_ref[rows] = v

    _ffn_passes(lambda rows: jnp.where(is_ctx, xc_ref[rows], xl_ref[rows]), store,
                sh_ref, sc_ref, gt_ref, ng_ref, wi_ref, wo_ref)


def _ffn_split_out_kernel(x_ref, sh_ref, sc_ref, gt_ref, ng_ref, wi_ref, wo_ref,
                          oc_ref, ol_ref):
    outs = {}

    def keep(rows, v):
        outs[rows.start] = v

    _ffn_passes(lambda rows: x_ref[rows], keep, sh_ref, sc_ref, gt_ref, ng_ref, wi_ref, wo_ref)
    is_ctx = pl.program_id(0) < M_CTX // FFN_TM

    @pl.when(is_ctx)
    def _():
        for rows in _FFN_PASSES:
            oc_ref[rows] = outs[rows.start]

    @pl.when(jnp.logical_not(is_ctx))
    def _():
        for rows in _FFN_PASSES:
            ol_ref[rows] = outs[rows.start]


def _mod_row_of_tile(i, rows_per_tile):
    n_ctx = M_CTX // rows_per_tile
    per_b = T_LAT // rows_per_tile
    return jnp.where(i < n_ctx, 0, 1 + jnp.maximum(i - n_ctx, 0) // per_b)


def _ffn(xs, mods, norm_g, wi, wo, l, f, split_out=False):
    j0 = 6 * f
    n_ctx = M_CTX // FFN_TM
    row = functools.partial(_mod_row_of_tile, rows_per_tile=FFN_TM)

    def mod_spec(j):
        return pl.BlockSpec((None, None, None, 1, D), lambda i: (l, j, row(i), 0, 0))

    ctx_rows = pl.BlockSpec((FFN_TM, D), lambda i: (jnp.minimum(i, n_ctx - 1), 0))
    lat_rows = pl.BlockSpec((FFN_TM, D), lambda i: (jnp.maximum(i - n_ctx, 0), 0))
    all_rows = pl.BlockSpec((FFN_TM, D), lambda i: (i, 0))
    split_in = isinstance(xs, tuple)
    assert not (split_in and split_out)
    if split_in:
        body, x_specs, xs = _ffn_split_in_kernel, [ctx_rows, lat_rows], list(xs)
    else:
        body, x_specs, xs = (_ffn_split_out_kernel if split_out else _ffn_kernel), [all_rows], [xs]
    if split_out:
        out_specs = [ctx_rows, lat_rows]
        out_shape = [jax.ShapeDtypeStruct((M_CTX, D), f32), jax.ShapeDtypeStruct((M_LAT, D), f32)]
    else:
        out_specs = all_rows
        out_shape = jax.ShapeDtypeStruct((M_ALL, D), f32)

    const = dict(pipeline_mode=pl.Buffered(1))
    return pl.pallas_call(
        body,
        grid=(M_ALL // FFN_TM,),
        in_specs=x_specs + [
            mod_spec(j0), mod_spec(j0 + 1), mod_spec(j0 + 2),
            pl.BlockSpec((None, None, 1, D), lambda i: (l, 2 * f, 0, 0)),
            pl.BlockSpec((None, None, D, 2 * D_FF), lambda i: (l, f, 0, 0), **const),
            pl.BlockSpec((None, None, D_FF, D), lambda i: (l, f, 0, 0), **const),
        ],
        out_specs=out_specs,
        out_shape=out_shape,
        compiler_params=pltpu.CompilerParams(
            dimension_semantics=("arbitrary",), vmem_limit_bytes=VMEM_LIMIT),
        name=f"ffn{f}",
    )(*xs, mods, mods, mods, norm_g, wi, wo)


def _scan8(a, b, h_init, reverse, emit):
    T, C = a.shape
    nb = T // 8
    row8 = lax.broadcasted_iota(jnp.int32, (8, C), 0)
    for s in (1, 2, 4):
        if reverse:
            a_r = pltpu.roll(a, T - s, 0)
            b_r = pltpu.roll(b, T - s, 0)
            edge = row8 < 8 - s
            a_s = jnp.concatenate([a_r[:T - 8], jnp.where(edge, a_r[T - 8:], 1.0)], axis=0)
            b_s = jnp.concatenate([b_r[:T - 8], jnp.where(edge, b_r[T - 8:], 0.0)], axis=0)
        else:
            a_r = pltpu.roll(a, s, 0)
            b_r = pltpu.roll(b, s, 0)
            edge = row8 >= s
            a_s = jnp.concatenate([jnp.where(edge, a_r[:8], 1.0), a_r[8:]], axis=0)
            b_s = jnp.concatenate([jnp.where(edge, b_r[:8], 0.0), b_r[8:]], axis=0)
        b = a * b_s + b
        a = a * a_s
        yield
    h = jnp.broadcast_to(h_init, (8, C))
    for n, v in enumerate(range(nb - 1, -1, -1) if reverse else range(nb)):
        h = a[v * 8:(v + 1) * 8] * h + b[v * 8:(v + 1) * 8]
        emit(v, h)
        if n % 4 == 3:
            yield
    return h[0:1] if reverse else h[7:8]


def _lru_group(h_ref, win_ref, cw_ref, cb_ref, wg_ref, bg_ref, lam_ref, h0, ycat_ref, st_ref,
               gate_sc):
    T = h_ref.shape[0]
    p = _dot(h_ref[...], win_ref[:, C_XA:C_XA + 2 * GW])
    xa = p[:, :GW]
    yield
    gate_sc[...] = _gelu_tanh(p[:, GW:])
    yield
    tpos = lax.broadcasted_iota(jnp.int32, (T, GW), 0)

    def shifted(k):
        if k > 0:
            return jnp.where(tpos >= k, pltpu.roll(xa, k, 0), 0.0)
        return jnp.where(tpos < T + k, pltpu.roll(xa, T + k, 0), 0.0)

    cw = cw_ref[...]
    xc = shifted(2) * cw[0:1] + shifted(1) * cw[1:2] + xa * cw[2:3] + shifted(-1) * cw[3:4]
    xc = xc + cb_ref[...]
    xcb = xc.astype(bf16)
    lam = lam_ref[...]
    yield
    for d in range(2):
        ds_ = slice(d * 2 * GW, (d + 1) * 2 * GW)
        pre = _dot(xcb, wg_ref[:, ds_]) + bg_ref[:, ds_]
        yield
        r = _sigmoid(pre[:, 0:GW])
        gi = _sigmoid(pre[:, GW:2 * GW])
        yield
        log_a = (-LRU_C * r) * _softplus(-lam[d:d + 1])
        a = jnp.exp(log_a)
        u = jnp.sqrt(1.0 - a * a) * (gi * xc)
        yield
        h_init = jnp.zeros((1, GW), f32) if h0 is None else h0[d:d + 1]

        def emit_fwd(v, hv):
            ycat_ref[v * 8:(v + 1) * 8, 0:GW] = hv

        def emit_bwd(v, hv):
            rows = slice(v * 8, (v + 1) * 8)
            ycat_ref[rows, 0:GW] = (ycat_ref[rows, 0:GW] + hv) * gate_sc[rows, :]

        last = yield from _scan8(a, u, h_init, d == 1, emit_bwd if d == 1 else emit_fwd)
        if st_ref is not None:
            st_ref[d:d + 1, :] = last
        yield


def _build_decay_tiles(dec_ref, dm_sc, nt):
    r = lax.broadcasted_iota(jnp.int32, (Q_TILE, Q_TILE), 0)
    c = lax.broadcasted_iota(jnp.int32, (Q_TILE, Q_TILE), 1)
    base = (r - c).astype(f32)
    for h in range(4):
        lgf = -_softplus(-dec_ref[h])
        lgb = -_softplus(-dec_ref[4 + h])
        for m in range(-(nt - 1), nt):
            diff = base + float(m * Q_TILE)
            if m > 0:
                mat = jnp.exp(lgf * diff)
            elif m < 0:
                mat = jnp.exp(lgb * (-diff))
            else:
                mat = (jnp.where(diff >= 0, jnp.exp(lgf * jnp.maximum(diff, 0.0)), 0.0)
                       + jnp.where(diff <= 0, jnp.exp(lgb * jnp.maximum(-diff, 0.0)), 0.0))
            dm_sc[h * (2 * nt - 1) + m + nt - 1] = mat


def _retention_group(h_ref, win_ref, dec_ref, gn_ref, s0_ref, ycat_ref, sret_ref,
                     q_sc, k_sc, v_sc, g_sc, dm_sc):
    T = h_ref.shape[0]
    nt = T // Q_TILE
    p = _dot(h_ref[...], win_ref[:, C_QD:C_QD + 4 * GW])
    yield
    q_sc[...] = p[:, 0:GW].astype(bf16)
    kf = p[:, GW:2 * GW] * ATT_SCALE
    k_sc[...] = kf.astype(bf16)
    v_sc[...] = p[:, 2 * GW:3 * GW].astype(bf16)
    g_sc[...] = p[:, 3 * GW:4 * GW]
    gn = gn_ref[...]
    tcol = lax.broadcasted_iota(jnp.int32, (T, 1), 0).astype(f32)
    yield

    def head(h):
        hs = slice(h * HD, (h + 1) * HD)
        lgf = -_softplus(-dec_ref[h])
        lgb = -_softplus(-dec_ref[4 + h])
        kh = k_sc[:, hs]
        vh = v_sc[:, hs]
        if sret_ref is not None:
            zf = jnp.exp(lgf * ((T - 1.0) - tcol))
            zb = jnp.exp(lgb * tcol)
            kff = kf[:, hs]
            sret_ref[h * HD:(h + 1) * HD, :] = _dot_tn((kff * zf).astype(bf16), vh)
            sret_ref[(4 + h) * HD:(5 + h) * HD, :] = _dot_tn((kff * zb).astype(bf16), vh)
        if s0_ref is not None:
            s0f = s0_ref[h * HD:(h + 1) * HD, :].astype(bf16)
            s0b = s0_ref[(4 + h) * HD:(5 + h) * HD, :].astype(bf16)
        gnh = gn[:, hs]

        def tile(i):
            t0 = pl.multiple_of(i * Q_TILE, Q_TILE)
            rows = pl.ds(t0, Q_TILE)
            qh = q_sc[rows, hs]
            o = None
            for j in range(nt):
                ks = slice(j * Q_TILE, (j + 1) * Q_TILE)
                dm = dm_sc[h * (2 * nt - 1) + (nt - 1 - j) + i]
                sd = (_dot_nt(qh, kh[ks]) * dm).astype(bf16)
                yield
                oj = _dot(sd, vh[ks])
                o = oj if o is None else o + oj
                yield
            if s0_ref is not None:
                tq = (lax.broadcasted_iota(jnp.int32, (Q_TILE, 1), 0) + t0).astype(f32)
                o = o + _dot(qh, s0f) * jnp.exp(lgf * (tq + 1.0))
                o = o + _dot(qh, s0b) * jnp.exp(lgb * (T - tq))
            mu = jnp.mean(o, axis=-1, keepdims=True)
            oc = o - mu
            var = jnp.mean(oc * oc, axis=-1, keepdims=True)
            yield
            on = oc * lax.rsqrt(var + EPS) * gnh
            ycat_ref[rows, 3 * GW + h * HD:3 * GW + (h + 1) * HD] = on * _silu(g_sc[rows, hs])
            yield

        return tile

    if nt == 1:
        for h in range(4):
            tile = head(h)
            yield
            yield from tile(0)
    else:
        tiles = [head(h) for h in range(4)]
        _loop(nt, lambda i: _run(_interleave(*[tile(i) for tile in tiles])))


N_CTX_IN = 18


def _ctx_mixer_kernel(*refs):
    n_out = 7
    n_scratch = 8
    ins = refs[:N_CTX_IN]
    outs = refs[len(refs) - n_out - n_scratch:]
    _ctx_mixer_body(*ins, *outs)


def _ctx_mixer_body(x_ref, sh_ref, sc_ref, gt_ref, ng_ref, win_ref, wout_ref,
                    cw_ref, cb_ref, wg_ref, bg_ref, lam_ref,
                    qnb_ref, knb_ref, qnc_ref, knc_ref, dec_ref, gn_ref,
                    o_ref, kb_ref, vb_ref, kc_ref, vc_ref, st_ref, sret_ref,
                    h_ref, ycat_ref, q_sc, k_sc, v_sc, g_sc, gate_sc, dm_sc):
    @pl.when(pl.program_id(0) == 0)
    def _():
        _build_decay_tiles(dec_ref, dm_sc, T_CTX // Q_TILE)

    x = x_ref[...]
    h_ref[...] = _rms_mod(x, ng_ref[...], sc_ref[...], sh_ref[...]).astype(bf16)

    _run(_interleave(
        _lru_group(h_ref, win_ref, cw_ref, cb_ref, wg_ref, bg_ref, lam_ref, None, ycat_ref,
                   st_ref, gate_sc),
        _ctx_attention_group(h_ref, win_ref, C_QB, 2, qnb_ref, knb_ref, kb_ref, vb_ref,
                             ycat_ref, GW),
        _ctx_attention_group(h_ref, win_ref, C_QC, 4, qnc_ref, knc_ref, kc_ref, vc_ref,
                             ycat_ref, 2 * GW),
        _retention_group(h_ref, win_ref, dec_ref, gn_ref, None, ycat_ref, sret_ref,
                         q_sc, k_sc, v_sc, g_sc, dm_sc),
    ))
    y = _dot(ycat_ref[...].astype(bf16), wout_ref[...])
    o_ref[...] = x_ref[...] + gt_ref[...] * y


def _ctx_attention_group(h_ref, win_ref, col, n_kv, qn_ref, kn_ref, k_out_ref, v_out_ref,
                         ycat_ref, ycol):
    kw = n_kv * HD
    p = _dot(h_ref[...], win_ref[:, col:col + GW + 2 * kw])
    yield
    q = _headnorm(p[:, 0:GW], qn_ref[...] * ATT_SCALE).astype(bf16)
    yield
    kn = _headnorm(p[:, GW:GW + kw], kn_ref[...])
    k_out_ref[...] = kn
    v_out_ref[...] = p[:, GW + kw:GW + 2 * kw]
    k = kn.astype(bf16)
    v = p[:, GW + kw:GW + 2 * kw].astype(bf16)
    yield

    def head(hh):
        kvs = slice((hh * n_kv // 4) * HD, (hh * n_kv // 4 + 1) * HD)

        def store(o):
            ycat_ref[:, ycol + hh * HD:ycol + (hh + 1) * HD] = o

        return _attend_stages(lambda: [_dot_nt(q[:, hh * HD:(hh + 1) * HD], k[:, kvs])],
                              [v[:, kvs]], store)

    yield from _interleave(*[head(hh) for hh in range(4)])


def _lat_mixer_kernel(x_ref, sh_ref, sc_ref, gt_ref, ng_ref, win_ref, wout_ref,
                      cw_ref, cb_ref, wg_ref, bg_ref, lam_ref,
                      qnb_ref, knb_ref, qnc_ref, knc_ref, dec_ref, gn_ref,
                      cos_ref, sin_ref, bias_ref,
                      cbk_ref, cbv_ref, cck_ref, ccv_ref, lru0_ref, ret0_ref,
                      o_ref,
                      h_ref, ycat_ref, q_sc, k_sc, v_sc, g_sc, gate_sc, dm_sc, kall_sc, vall_sc):
    T = T_LAT

    @pl.when(pl.program_id(0) == 0)
    def _():
        _build_decay_tiles(dec_ref, dm_sc, T // Q_TILE)

    x = x_ref[...]
    h_ref[...] = _rms_mod(x, ng_ref[...], sc_ref[...], sh_ref[...]).astype(bf16)

    _run(_lru_group(h_ref, win_ref, cw_ref, cb_ref, wg_ref, bg_ref, lam_ref, lru0_ref[...],
                    ycat_ref, None, gate_sc))

    p = _dot(h_ref[...], win_ref[:, C_QB:C_QB + 2 * GW])
    cos = cos_ref[...]
    sin = sin_ref[...]
    q_sc[...] = _rope(_headnorm(p[:, 0:GW], qnb_ref[...] * ATT_SCALE), cos, sin).astype(bf16)
    kall_sc[0:T, 0:128] = _rope(_headnorm(p[:, GW:GW + 128], knb_ref[...]),
                                cos, sin).astype(bf16)
    kall_sc[T:T + PAST, 0:128] = cbk_ref[...].astype(bf16)
    vall_sc[0:T, 0:128] = p[:, GW + 128:2 * GW].astype(bf16)
    vall_sc[T:T + PAST, 0:128] = cbv_ref[...].astype(bf16)
    kvk = [kall_sc[:, kv * HD:(kv + 1) * HD] for kv in range(2)]
    kvv = [vall_sc[:, kv * HD:(kv + 1) * HD] for kv in range(2)]

    def tile_b(i):
        rows = pl.ds(pl.multiple_of(i * Q_TILE, Q_TILE), Q_TILE)

        def head(hh):
            def store(o):
                ycat_ref[rows, GW + hh * HD:GW + (hh + 1) * HD] = o

            return _attend_stages(
                lambda: [_dot_nt(q_sc[rows, hh * HD:(hh + 1) * HD], kvk[hh // 2])],
                [kvv[hh // 2]], store)

        _run(_interleave(*[head(hh) for hh in range(4)]))

    _loop(T // Q_TILE, tile_b)

    p = _dot(h_ref[...], win_ref[:, C_QC:C_QC + 3 * GW])
    q_sc[...] = _headnorm(p[:, 0:GW], qnc_ref[...] * ATT_SCALE).astype(bf16)
    kall_sc[0:T, :] = _headnorm(p[:, GW:2 * GW], knc_ref[...]).astype(bf16)
    kall_sc[T:T + PAST, :] = cck_ref[...].astype(bf16)
    vall_sc[0:T, :] = p[:, 2 * GW:3 * GW].astype(bf16)
    vall_sc[T:T + PAST, :] = ccv_ref[...].astype(bf16)
    n_loc = NAT_WR * GRID_W
    qcol = lax.broadcasted_iota(jnp.int32, (GRID_W, n_loc), 0)
    kcol = lax.broadcasted_iota(jnp.int32, (GRID_W, n_loc), 1) & (GRID_W - 1)
    cstart = jnp.clip(qcol - NAT_WC // 2, 0, GRID_W - NAT_WC)
    in_win = (kcol >= cstart) & (kcol < cstart + NAT_WC)

    def row_block(r):
        rs = jnp.clip(r - NAT_WR // 2, 0, GRID_R - NAT_WR)
        didx = rs - r + (NAT_WR - 1)
        qrows = pl.ds(pl.multiple_of(r * GRID_W, GRID_W), GRID_W)
        krows = pl.ds(pl.multiple_of(rs * GRID_W, GRID_W), n_loc)

        def head(hh):
            hs = slice(hh * HD, (hh + 1) * HD)

            def scores():
                qh = q_sc[qrows, hs]
                s_loc = _dot_nt(qh, kall_sc[krows, hs])
                s_loc = jnp.where(in_win, s_loc + bias_ref[hh, didx], NEG_INF)
                return [s_loc, _dot_nt(qh, kall_sc[T:T + PAST, hs])]

            def store(o):
                ycat_ref[qrows, 2 * GW + hh * HD:2 * GW + (hh + 1) * HD] = o

            return _attend_stages(scores, [vall_sc[krows, hs], vall_sc[T:T + PAST, hs]], store)

        _run(_interleave(*[head(hh) for hh in range(4)]))

    _loop(GRID_R, row_block, unroll=LOOP_UNROLL)

    _run(_retention_group(h_ref, win_ref, dec_ref, gn_ref, ret0_ref, ycat_ref, None,
                          q_sc, k_sc, v_sc, g_sc, dm_sc))

    y = _dot(ycat_ref[...].astype(bf16), wout_ref[...])
    o_ref[...] = x + gt_ref[...] * y


def _layer_specs(l, mod_row, grid_rank=1):
    const = dict(pipeline_mode=pl.Buffered(1))

    def lspec(shape):
        nd = len(shape)
        return pl.BlockSpec((None,) + shape, lambda i: (l,) + (0,) * nd, **const)

    def mod_spec(j):
        return pl.BlockSpec((None, None, None, 1, D), lambda i: (l, j, mod_row(i), 0, 0))

    return [
        mod_spec(3), mod_spec(4), mod_spec(5),
        pl.BlockSpec((None, None, 1, D), lambda i: (l, 1, 0, 0)),
        lspec((D, IN_COLS)), lspec((D, D)),
        lspec((4, GW)), lspec((1, GW)), lspec((GW, 4 * GW)), lspec((1, 4 * GW)), lspec((2, GW)),
        lspec((1, GW)), lspec((1, 128)), lspec((1, GW)), lspec((1, GW)),
        lspec((8, 1, 1)), lspec((1, GW)),
    ]


def _ctx_mixer(x, mods, norm_g, lw, l, prev):
    T = T_CTX

    def layer_spec(*tail):
        return pl.BlockSpec((None, None) + tail, lambda i: (i, l) + (0,) * len(tail))

    tails = [(T, 128), (T, 128), (T, GW), (T, GW), (2, GW), (8 * HD, HD)]
    operands = [x, mods, mods, mods, norm_g, lw["w_in"], lw["w_out"],
                lw["conv_w"], lw["conv_b"], lw["wg"], lw["bg"], lw["lam"],
                lw["qnb"], lw["knb"], lw["qnc"], lw["knc"], lw["dec"], lw["gn"]]
    assert len(operands) == N_CTX_IN
    aliases = {0: 0}
    prev_specs = []
    if prev is not None:
        prev_specs = [pl.BlockSpec(memory_space=pl.ANY)] * len(prev)
        aliases.update({N_CTX_IN + k: 1 + k for k in range(len(prev))})
        operands = operands + list(prev)

    outs = pl.pallas_call(
        _ctx_mixer_kernel,
        grid=(N_CTX,),
        in_specs=[pl.BlockSpec((T, D), lambda i: (i, 0))] + _layer_specs(l, lambda i: 0)
        + prev_specs,
        out_specs=[pl.BlockSpec((T, D), lambda i: (i, 0))] + [layer_spec(*t) for t in tails],
        out_shape=[jax.ShapeDtypeStruct((M_ALL, D), f32)]
        + [jax.ShapeDtypeStruct((N_CTX, DEPTH) + t, f32) for t in tails],
        scratch_shapes=[
            pltpu.VMEM((T, D), bf16), pltpu.VMEM((T, D), f32),
            pltpu.VMEM((T, GW), bf16), pltpu.VMEM((T, GW), bf16), pltpu.VMEM((T, GW), bf16),
            pltpu.VMEM((T, GW), f32), pltpu.VMEM((T, GW), f32),
            pltpu.VMEM((4, Q_TILE, Q_TILE), f32),
        ],
        input_output_aliases=aliases,
        compiler_params=pltpu.CompilerParams(
            dimension_semantics=("arbitrary",), vmem_limit_bytes=VMEM_LIMIT),
        name="ctx_mixer",
    )(*operands)
    return outs[0], tuple(outs[1:])


def _lat_mixer(x, mods, norm_g, lw, l, cos, sin, caches):
    T = T_LAT
    blk0 = M_CTX // T
    const = dict(pipeline_mode=pl.Buffered(1))
    cbk, cbv, cck, ccv, lru0, ret0 = caches

    def cache_spec(w):
        return pl.BlockSpec((None, None, PAST, w), lambda i: (i, l, 0, 0))

    return pl.pallas_call(
        _lat_mixer_kernel,
        grid=(N_LAT,),
        in_specs=[pl.BlockSpec((T, D), lambda i: (blk0 + i, 0), **const)]
        + _layer_specs(l, lambda i: 1 + i)
        + [
            pl.BlockSpec((T, 128), lambda i: (0, 0), **const),
            pl.BlockSpec((T, 128), lambda i: (0, 0), **const),
            pl.BlockSpec((None, 4, NAT_WR, GRID_W, NAT_WR * GRID_W),
                         lambda i: (l, 0, 0, 0, 0), **const),
            cache_spec(128), cache_spec(128), cache_spec(GW), cache_spec(GW),
            pl.BlockSpec((None, None, 2, GW), lambda i: (i, l, 0, 0)),
            pl.BlockSpec((None, None, 8 * HD, HD), lambda i: (i, l, 0, 0)),
        ],
        out_specs=pl.BlockSpec((T, D), lambda i: (blk0 + i, 0), **const),
        out_shape=jax.ShapeDtypeStruct((M_ALL, D), f32),
        scratch_shapes=[
            pltpu.VMEM((T, D), bf16), pltpu.VMEM((T, D), f32),
            pltpu.VMEM((T, GW), bf16), pltpu.VMEM((T, GW), bf16), pltpu.VMEM((T, GW), bf16),
            pltpu.VMEM((T, GW), f32), pltpu.VMEM((T, GW), f32),
            pltpu.VMEM((4 * (2 * (T // Q_TILE) - 1), Q_TILE, Q_TILE), f32),
            pltpu.VMEM((T + PAST, GW), bf16), pltpu.VMEM((T + PAST, GW), bf16),
        ],
        input_output_aliases={0: 0},
        compiler_params=pltpu.CompilerParams(
            dimension_semantics=("arbitrary",), vmem_limit_bytes=VMEM_LIMIT),
        name="lat_mixer",
    )(x, mods, mods, mods, norm_g, lw["w_in"], lw["w_out"],
      lw["conv_w"], lw["conv_b"], lw["wg"], lw["bg"], lw["lam"],
      lw["qnb"], lw["knb"], lw["qnc"], lw["knc"], lw["dec"], lw["gn"],
      cos, sin, lw["nat_bias"], cbk, cbv, cck, ccv, lru0, ret0)


def _block_diag(w):
    eye = jnp.eye(4, dtype=w.dtype)
    return jnp.einsum("lncd,nm->lncmd", w, eye).reshape(DEPTH, GW, GW)


def _rope_tables():
    t = jnp.arange(T_LAT)
    row = (t // GRID_W).astype(f32)
    col = (t % GRID_W).astype(f32)
    n_freq = HD // 4
    inv = ROPE_BASE ** (-jnp.arange(n_freq, dtype=f32) / n_freq)
    ang = jnp.concatenate([row[:, None] * inv, col[:, None] * inv], axis=-1)
    cos = jnp.repeat(jnp.cos(ang), 2, axis=-1)
    sin = jnp.repeat(jnp.sin(ang), 2, axis=-1)
    sign = jnp.where(jnp.arange(HD) % 2 == 0, -1.0, 1.0).astype(f32)
    return jnp.tile(cos, (1, 2)), jnp.tile(sin * sign, (1, 2))


def _nat_bias_windows(nat_bias):
    pad = GRID_W - NAT_WC
    ext = jnp.concatenate([jnp.repeat(nat_bias[..., :1], pad, -1), nat_bias,
                           jnp.repeat(nat_bias[..., -1:], pad + 1, -1)], axis=-1)
    t = jnp.tile(ext, (1, 1, 1, GRID_W))[..., :GRID_W * 127]
    t = t.reshape(DEPTH, 4, 2 * NAT_WR - 1, GRID_W, 127)[..., GRID_W - 1:]
    tab = jnp.stack([t[:, :, d:d + NAT_WR] for d in range(NAT_WR)], axis=2)
    tab = tab.transpose(0, 1, 2, 4, 3, 5)
    return tab.reshape(DEPTH, 4, NAT_WR, GRID_W, NAT_WR * GRID_W)


def kernel(x_prompt, x_sample, cache_b_k, cache_b_v, cache_c_k, cache_c_v, state_lru, state_ret, c, c_ctx, w_mod, b_mod, norm_g, ffn_w_in, ffn_w_out, w_in, w_out, conv_w, conv_b, lru_w_r, lru_b_r, lru_w_i, lru_b_i, lru_lambda, gqa_qn, gqa_kn, nat_qn, nat_kn, nat_bias, ret_decay, ret_gn):
    cvec = jnp.concatenate([c_ctx[None, :], c, jnp.zeros((8 - 1 - N_LAT, D), f32)], axis=0)
    mods = _modulation(cvec, w_mod, b_mod).reshape(DEPTH, N_MOD, 8, 1, D)
    ng = norm_g.reshape(DEPTH, 3, 1, D)

    wi = ffn_w_in.astype(bf16)
    wo = ffn_w_out.astype(bf16)
    lw = dict(
        w_in=w_in.astype(bf16),
        w_out=w_out.astype(bf16),
        conv_w=conv_w,
        conv_b=conv_b.reshape(DEPTH, 1, GW),
        wg=jnp.concatenate([_block_diag(lru_w_r[:, 0]), _block_diag(lru_w_i[:, 0]),
                            _block_diag(lru_w_r[:, 1]), _block_diag(lru_w_i[:, 1])],
                           axis=-1).astype(bf16),
        bg=jnp.concatenate([lru_b_r[:, 0], lru_b_i[:, 0], lru_b_r[:, 1], lru_b_i[:, 1]],
                           axis=-1).reshape(DEPTH, 1, 4 * GW),
        lam=lru_lambda,
        qnb=jnp.tile(gqa_qn, (1, 4)).reshape(DEPTH, 1, GW),
        knb=jnp.tile(gqa_kn, (1, 2)).reshape(DEPTH, 1, 128),
        qnc=jnp.tile(nat_qn, (1, 4)).reshape(DEPTH, 1, GW),
        knc=jnp.tile(nat_kn, (1, 4)).reshape(DEPTH, 1, GW),
        dec=ret_decay.reshape(DEPTH, 8, 1, 1),
        gn=ret_gn.reshape(DEPTH, 1, GW),
        nat_bias=_nat_bias_windows(nat_bias),
    )
    cos, sin = _rope_tables()
    caches = (cache_b_k.reshape(N_LAT, DEPTH, PAST, 128), cache_b_v.reshape(N_LAT, DEPTH, PAST, 128),
              cache_c_k.reshape(N_LAT, DEPTH, PAST, GW), cache_c_v.reshape(N_LAT, DEPTH, PAST, GW),
              state_lru, state_ret.reshape(N_LAT, DEPTH, 8 * HD, HD))

    x = (x_prompt.reshape(M_CTX, D), x_sample.reshape(M_LAT, D))
    new = None
    for l in range(DEPTH):
        x = _ffn(x, mods, ng, wi, wo, l, 0)
        x, new = _ctx_mixer(x, mods, ng, lw, l, new)
        x = _lat_mixer(x, mods, ng, lw, l, cos, sin, caches)
        x = _ffn(x, mods, ng, wi, wo, l, 1, split_out=(l == DEPTH - 1))

    new_bk, new_bv, new_ck, new_cv, new_lru, new_ret = new
    return (x[0].reshape(N_CTX, T_CTX, D), x[1].reshape(N_LAT, T_LAT, D),
            new_bk.reshape(N_CTX, DEPTH, T_CTX, 2, HD), new_bv.reshape(N_CTX, DEPTH, T_CTX, 2, HD),
            new_ck.reshape(N_CTX, DEPTH, T_CTX, 4, HD), new_cv.reshape(N_CTX, DEPTH, T_CTX, 4, HD),
            new_lru, new_ret.reshape(N_CTX, DEPTH, 2, 4, HD, HD))
```

```python
import functools
import math

import jax
import jax.numpy as jnp
from jax import lax
from jax.experimental import pallas as pl
from jax.experimental.pallas import tpu as pltpu

f32 = jnp.float32
bf16 = jnp.bfloat16

D = 1024
DEPTH = 4
N_CTX, T_CTX = 32, 256
N_LAT, T_LAT = 4, 1024
M_CTX = N_CTX * T_CTX
M_LAT = N_LAT * T_LAT
M_ALL = M_CTX + M_LAT
PAST = 256
GRID_W = 64
GRID_R = T_LAT // GRID_W
HD = 64
GW = 256
LRU_C = 8.0
NAT_WR, NAT_WC = 8, 16
D_FF = 2816
IN_COLS = 2816
EPS = 1e-6
NEG_INF = -1e30
N_MOD = 9
ATT_SCALE = HD ** -0.5
ROPE_BASE = 10000.0

C_XA, C_GA = 0, 256
C_QB, C_KB, C_VB = 512, 768, 896
C_QC, C_KC, C_VC = 1024, 1280, 1536
C_QD, C_KD, C_VD, C_GD = 1792, 2048, 2304, 2560

FFN_TM = 1024
FFN_SUB = 512
FFN_CHUNKS = ((0, 768), (768, 1536), (1536, 2304), (2304, 2816))
Q_TILE = 256
LRU_STAGES_PER_ROUND = 4
RET_PREP_STAGES = 4
VMEM_LIMIT = 58 * 1024 * 1024


def _dot(a, b):
    return jnp.dot(a, b, preferred_element_type=f32)


def _dot_nt(a, b):
    return lax.dot_general(a, b, (((1,), (1,)), ((), ())), preferred_element_type=f32)


def _dot_tn(a, b):
    return lax.dot_general(a, b, (((0,), (0,)), ((), ())), preferred_element_type=f32)


def _softplus(z):
    return jnp.maximum(z, 0.0) + jnp.log1p(jnp.exp(-jnp.abs(z)))


def _sigmoid(z):
    return 1.0 / (1.0 + jnp.exp(-z))


def _silu(z):
    return z * _sigmoid(z)


def _gelu_tanh(z):
    c = math.sqrt(2.0 / math.pi)
    return z * (0.5 * (1.0 + jnp.tanh(c * (z + 0.044715 * (z * z * z)))))


def _rms_mod(x, ng, sc, sh):
    ms = jnp.mean(x * x, axis=-1, keepdims=True)
    y = x * lax.rsqrt(ms + EPS) * ng
    return y * (1.0 + sc) + sh


def _headnorm(x, gain):
    T, W = x.shape
    lane = lax.broadcasted_iota(jnp.int32, (T, 128), 1)
    lo = lane < HD
    cols = []
    for c in range(W // 128):
        xc = x[:, c * 128:(c + 1) * 128]
        x2 = xc * xc
        s_lo = jnp.sum(jnp.where(lo, x2, 0.0), axis=-1, keepdims=True)
        s_hi = jnp.sum(jnp.where(lo, 0.0, x2), axis=-1, keepdims=True)
        r_lo = lax.rsqrt(s_lo * (1.0 / HD) + EPS)
        r_hi = lax.rsqrt(s_hi * (1.0 / HD) + EPS)
        cols.append(xc * jnp.where(lo, r_lo, r_hi))
    y = cols[0] if len(cols) == 1 else jnp.concatenate(cols, axis=-1)
    return y * gain


def _rope(x, cos, sin_signed):
    T, W = x.shape
    lane = lax.broadcasted_iota(jnp.int32, (T, 128), 1)
    even = (lane & 1) == 0
    cols = []
    for c in range(W // 128):
        xc = x[:, c * 128:(c + 1) * 128]
        sw = jnp.where(even, pltpu.roll(xc, 127, 1), pltpu.roll(xc, 1, 1))
        cols.append(xc * cos + sw * sin_signed)
    return cols[0] if len(cols) == 1 else jnp.concatenate(cols, axis=-1)


def _attend_stages(scores, v_list, store):
    s_list = scores()
    yield
    m = s_list[0].max(axis=-1, keepdims=True)
    for s in s_list[1:]:
        m = jnp.maximum(m, s.max(axis=-1, keepdims=True))
    yield
    p_list = [jnp.exp(s - m) for s in s_list]
    l = None
    for p in p_list:
        ls = jnp.sum(p, axis=-1, keepdims=True)
        l = ls if l is None else l + ls
    yield
    o = None
    for p, v in zip(p_list, v_list):
        os_ = _dot(p.astype(bf16), v)
        o = os_ if o is None else o + os_
    store(o * (1.0 / l))
    yield


def _interleave(*stages):
    live = list(stages)
    while live:
        for g in list(live):
            try:
                next(g)
            except StopIteration:
                live.remove(g)
        yield


def _every(stages, n):
    while True:
        for _ in range(n):
            try:
                next(stages)
            except StopIteration:
                return
        yield


def _delayed(stages, n):
    for _ in range(n):
        yield
    yield from stages


def _run(stages):
    for _ in stages:
        pass


def _loop(n, body):
    def step(i, carry):
        body(i)
        return carry
    lax.fori_loop(0, n, step, 0)


def _mod_kernel(c_ref, w_ref, b_ref, o_ref):
    cs = _silu(c_ref[...]).astype(bf16)
    o_ref[...] = _dot(cs, w_ref[...].astype(bf16)) + b_ref[...]


def _modulation(cvec, w_mod, b_mod):
    return pl.pallas_call(
        _mod_kernel,
        grid=(DEPTH, N_MOD),
        in_specs=[
            pl.BlockSpec((8, D), lambda l, j: (0, 0)),
            pl.BlockSpec((None, D, D), lambda l, j: (l, 0, j)),
            pl.BlockSpec((None, None, 1, D), lambda l, j: (l, j, 0, 0)),
        ],
        out_specs=pl.BlockSpec((None, None, 8, D), lambda l, j: (l, j, 0, 0)),
        out_shape=jax.ShapeDtypeStruct((DEPTH, N_MOD, 8, D), f32),
        compiler_params=pltpu.CompilerParams(
            dimension_semantics=("parallel", "parallel"), vmem_limit_bytes=VMEM_LIMIT),
        name="adaln_mod",
    )(cvec, w_mod, b_mod.reshape(DEPTH, N_MOD, 1, D))


def _ffn_rows(load, store, sh_ref, sc_ref, gt_ref, ng_ref, wi_ref, wo_ref):
    x = load()
    h = _rms_mod(x, ng_ref[...], sc_ref[...], sh_ref[...]).astype(bf16)
    yield
    acc = None
    for c0, c1 in FFN_CHUNKS:
        a = _dot(h, wi_ref[:, c0:c1])
        b = _dot(h, wi_ref[:, D_FF + c0:D_FF + c1])
        g = (_silu(a) * b).astype(bf16)
        y = _dot(g, wo_ref[c0:c1, :])
        acc = y if acc is None else acc + y
        yield
    store(x + (0.5 * gt_ref[...]) * acc)


_FFN_PASSES = tuple(slice(r, r + FFN_SUB) for r in range(0, FFN_TM, FFN_SUB))


def _ffn_passes(load, store, *refs):
    _run(_interleave(*[
        _delayed(_ffn_rows(functools.partial(load, rows), functools.partial(store, rows), *refs), k)
        for k, rows in enumerate(_FFN_PASSES)]))


def _ffn_kernel(x_ref, sh_ref, sc_ref, gt_ref, ng_ref, wi_ref, wo_ref, o_ref):
    def store(rows, v):
        o_ref[rows] = v

    _ffn_passes(lambda rows: x_ref[rows], store, sh_ref, sc_ref, gt_ref, ng_ref, wi_ref, wo_ref)


def _ffn_split_in_kernel(xc_ref, xl_ref, sh_ref, sc_ref, gt_ref, ng_ref, wi_ref, wo_ref, o_ref):
    is_ctx = pl.program_id(0) < M_CTX // FFN_TM

    def store(rows, v):
        o_ref[rows] = v

    _ffn_passes(lambda rows: jnp.where(is_ctx, xc_ref[rows], xl_ref[rows]), store,
                sh_ref, sc_ref, gt_ref, ng_ref, wi_ref, wo_ref)


def _ffn_split_out_kernel(x_ref, sh_ref, sc_ref, gt_ref, ng_ref, wi_ref, wo_ref,
                          oc_ref, ol_ref):
    outs = {}

    def keep(rows, v):
        outs[rows.start] = v

    _ffn_passes(lambda rows: x_ref[rows], keep, sh_ref, sc_ref, gt_ref, ng_ref, wi_ref, wo_ref)
    is_ctx = pl.program_id(0) < M_CTX // FFN_TM

    @pl.when(is_ctx)
    def _():
        for rows in _FFN_PASSES:
            oc_ref[rows] = outs[rows.start]

    @pl.when(jnp.logical_not(is_ctx))
    def _():
        for rows in _FFN_PASSES:
            ol_ref[rows] = outs[rows.start]


def _mod_row_of_tile(i, rows_per_tile):
    n_ctx = M_CTX // rows_per_tile
    per_b = T_LAT // rows_per_tile
    return jnp.where(i < n_ctx, 0, 1 + jnp.maximum(i - n_ctx, 0) // per_b)


def _ffn(xs, mods, norm_g, wi, wo, l, f, split_out=False):
    j0 = 6 * f
    n_ctx = M_CTX // FFN_TM
    row = functools.partial(_mod_row_of_tile, rows_per_tile=FFN_TM)

    def mod_spec(j):
        return pl.BlockSpec((None, None, None, 1, D), lambda i: (l, j, row(i), 0, 0))

    ctx_rows = pl.BlockSpec((FFN_TM, D), lambda i: (jnp.minimum(i, n_ctx - 1), 0))
    lat_rows = pl.BlockSpec((FFN_TM, D), lambda i: (jnp.maximum(i - n_ctx, 0), 0))
    all_rows = pl.BlockSpec((FFN_TM, D), lambda i: (i, 0))
    split_in = isinstance(xs, tuple)
    assert not (split_in and split_out)
    if split_in:
        body, x_specs, xs = _ffn_split_in_kernel, [ctx_rows, lat_rows], list(xs)
    else:
        body, x_specs, xs = (_ffn_split_out_kernel if split_out else _ffn_kernel), [all_rows], [xs]
    if split_out:
        out_specs = [ctx_rows, lat_rows]
        out_shape = [jax.ShapeDtypeStruct((M_CTX, D), f32), jax.ShapeDtypeStruct((M_LAT, D), f32)]
    else:
        out_specs = all_rows
        out_shape = jax.ShapeDtypeStruct((M_ALL, D), f32)

    const = dict(pipeline_mode=pl.Buffered(1))
    return pl.pallas_call(
        body,
        grid=(M_ALL // FFN_TM,),
        in_specs=x_specs + [
            mod_spec(j0), mod_spec(j0 + 1), mod_spec(j0 + 2),
            pl.BlockSpec((None, None, 1, D), lambda i: (l, 2 * f, 0, 0)),
            pl.BlockSpec((None, None, D, 2 * D_FF), lambda i: (l, f, 0, 0), **const),
            pl.BlockSpec((None, None, D_FF, D), lambda i: (l, f, 0, 0), **const),
        ],
        out_specs=out_specs,
        out_shape=out_shape,
        compiler_params=pltpu.CompilerParams(
            dimension_semantics=("arbitrary",), vmem_limit_bytes=VMEM_LIMIT),
        name=f"ffn{f}",
    )(*xs, mods, mods, mods, norm_g, wi, wo)


def _scan8(a, b, h_init, reverse, emit):
    T, C = a.shape
    nb = T // 8
    row8 = lax.broadcasted_iota(jnp.int32, (8, C), 0)
    for s in (1, 2, 4):
        if reverse:
            a_r = pltpu.roll(a, T - s, 0)
            b_r = pltpu.roll(b, T - s, 0)
            edge = row8 < 8 - s
            a_s = jnp.concatenate([a_r[:T - 8], jnp.where(edge, a_r[T - 8:], 1.0)], axis=0)
            b_s = jnp.concatenate([b_r[:T - 8], jnp.where(edge, b_r[T - 8:], 0.0)], axis=0)
        else:
            a_r = pltpu.roll(a, s, 0)
            b_r = pltpu.roll(b, s, 0)
            edge = row8 >= s
            a_s = jnp.concatenate([jnp.where(edge, a_r[:8], 1.0), a_r[8:]], axis=0)
            b_s = jnp.concatenate([jnp.where(edge, b_r[:8], 0.0), b_r[8:]], axis=0)
        b = a * b_s + b
        a = a * a_s
        yield
    h = jnp.broadcast_to(h_init, (8, C))
    for n, v in enumerate(range(nb - 1, -1, -1) if reverse else range(nb)):
        h = a[v * 8:(v + 1) * 8] * h + b[v * 8:(v + 1) * 8]
        emit(v, h)
        if n % 4 == 3:
            yield
    return h[0:1] if reverse else h[7:8]


def _lru_group(h_ref, win_ref, cw_ref, cb_ref, wg_ref, bg_ref, lam_ref, h0, ycat_ref, st_ref,
               gate_sc):
    T = h_ref.shape[0]
    p = _dot(h_ref[...], win_ref[:, C_XA:C_XA + 2 * GW])
    xa = p[:, :GW]
    yield
    gate_sc[...] = _gelu_tanh(p[:, GW:])
    yield
    tpos = lax.broadcasted_iota(jnp.int32, (T, GW), 0)

    def shifted(k):
        if k > 0:
            return jnp.where(tpos >= k, pltpu.roll(xa, k, 0), 0.0)
        return jnp.where(tpos < T + k, pltpu.roll(xa, T + k, 0), 0.0)

    cw = cw_ref[...]
    xc = shifted(2) * cw[0:1] + shifted(1) * cw[1:2] + xa * cw[2:3] + shifted(-1) * cw[3:4]
    xc = xc + cb_ref[...]
    xcb = xc.astype(bf16)
    lam = lam_ref[...]
    yield
    for d in range(2):
        ds_ = slice(d * 2 * GW, (d + 1) * 2 * GW)
        pre = _dot(xcb, wg_ref[:, ds_]) + bg_ref[:, ds_]
        yield
        r = _sigmoid(pre[:, 0:GW])
        gi = _sigmoid(pre[:, GW:2 * GW])
        yield
        log_a = (-LRU_C * r) * _softplus(-lam[d:d + 1])
        a = jnp.exp(log_a)
        u = jnp.sqrt(1.0 - a * a) * (gi * xc)
        yield
        h_init = jnp.zeros((1, GW), f32) if h0 is None else h0[d:d + 1]

        def emit_fwd(v, hv):
            ycat_ref[v * 8:(v + 1) * 8, 0:GW] = hv

        def emit_bwd(v, hv):
            rows = slice(v * 8, (v + 1) * 8)
            ycat_ref[rows, 0:GW] = (ycat_ref[rows, 0:GW] + hv) * gate_sc[rows, :]

        last = yield from _scan8(a, u, h_init, d == 1, emit_bwd if d == 1 else emit_fwd)
        if st_ref is not None:
            st_ref[d:d + 1, :] = last
        yield


def _build_decay_tiles(dec_ref, dm_sc, nt):
    r = lax.broadcasted_iota(jnp.int32, (Q_TILE, Q_TILE), 0)
    c = lax.broadcasted_iota(jnp.int32, (Q_TILE, Q_TILE), 1)
    base = (r - c).astype(f32)
    for h in range(4):
        lgf = -_softplus(-dec_ref[h])
        lgb = -_softplus(-dec_ref[4 + h])
        for m in range(-(nt - 1), nt):
            diff = base + float(m * Q_TILE)
            if m > 0:
                mat = jnp.exp(lgf * diff)
            elif m < 0:
                mat = jnp.exp(lgb * (-diff))
            else:
                mat = (jnp.where(diff >= 0, jnp.exp(lgf * jnp.maximum(diff, 0.0)), 0.0)
                       + jnp.where(diff <= 0, jnp.exp(lgb * jnp.maximum(-diff, 0.0)), 0.0))
            dm_sc[h * (2 * nt - 1) + m + nt - 1] = mat


def _retention_group(h_ref, win_ref, dec_ref, gn_ref, s0_ref, ycat_ref, sret_ref,
                     q_sc, k_sc, v_sc, g_sc, dm_sc, tiles_out=None):
    T = h_ref.shape[0]
    nt = T // Q_TILE
    q_sc[...] = _dot(h_ref[...], win_ref[:, C_QD:C_QD + GW]).astype(bf16)
    yield
    kf = _dot(h_ref[...], win_ref[:, C_KD:C_KD + GW]) * ATT_SCALE
    k_sc[...] = kf.astype(bf16)
    yield
    v_sc[...] = _dot(h_ref[...], win_ref[:, C_VD:C_VD + GW]).astype(bf16)
    yield
    g_sc[...] = _dot(h_ref[...], win_ref[:, C_GD:C_GD + GW])
    gn = gn_ref[...]
    tcol = lax.broadcasted_iota(jnp.int32, (T, 1), 0).astype(f32)
    yield

    def head(h):
        hs = slice(h * HD, (h + 1) * HD)
        lgf = -_softplus(-dec_ref[h])
        lgb = -_softplus(-dec_ref[4 + h])
        kh = k_sc[:, hs]
        vh = v_sc[:, hs]
        if sret_ref is not None:
            zf = jnp.exp(lgf * ((T - 1.0) - tcol))
            zb = jnp.exp(lgb * tcol)
            kff = kf[:, hs]
            sret_ref[h * HD:(h + 1) * HD, :] = _dot_tn((kff * zf).astype(bf16), vh)
            sret_ref[(4 + h) * HD:(5 + h) * HD, :] = _dot_tn((kff * zb).astype(bf16), vh)
        if s0_ref is not None:
            s0f = s0_ref[h * HD:(h + 1) * HD, :].astype(bf16)
            s0b = s0_ref[(4 + h) * HD:(5 + h) * HD, :].astype(bf16)
        gnh = gn[:, hs]

        def tile(i):
            t0 = pl.multiple_of(i * Q_TILE, Q_TILE)
            rows = pl.ds(t0, Q_TILE)
            qh = q_sc[rows, hs]
            o = None
            for j in range(nt):
                ks = slice(j * Q_TILE, (j + 1) * Q_TILE)
                dm = dm_sc[h * (2 * nt - 1) + (nt - 1 - j) + i]
                sd = (_dot_nt(qh, kh[ks]) * dm).astype(bf16)
                yield
                oj = _dot(sd, vh[ks])
                o = oj if o is None else o + oj
                yield
            if s0_ref is not None:
                tq = (lax.broadcasted_iota(jnp.int32, (Q_TILE, 1), 0) + t0).astype(f32)
                o = o + _dot(qh, s0f) * jnp.exp(lgf * (tq + 1.0))
                o = o + _dot(qh, s0b) * jnp.exp(lgb * (T - tq))
            mu = jnp.mean(o, axis=-1, keepdims=True)
            oc = o - mu
            var = jnp.mean(oc * oc, axis=-1, keepdims=True)
            yield
            on = oc * lax.rsqrt(var + EPS) * gnh
            ycat_ref[rows, 3 * GW + h * HD:3 * GW + (h + 1) * HD] = on * _silu(g_sc[rows, hs])
            yield

        return tile

    if nt == 1:
        for h in range(4):
            tile = head(h)
            yield
            yield from tile(0)
    else:
        tiles_out.extend(head(h) for h in range(4))


N_CTX_IN = 18


def _ctx_mixer_kernel(*refs):
    n_out = 7
    n_scratch = 8
    ins = refs[:N_CTX_IN]
    outs = refs[len(refs) - n_out - n_scratch:]
    _ctx_mixer_body(*ins, *outs)


def _ctx_mixer_body(x_ref, sh_ref, sc_ref, gt_ref, ng_ref, win_ref, wout_ref,
                    cw_ref, cb_ref, wg_ref, bg_ref, lam_ref,
                    qnb_ref, knb_ref, qnc_ref, knc_ref, dec_ref, gn_ref,
                    o_ref, kb_ref, vb_ref, kc_ref, vc_ref, st_ref, sret_ref,
                    h_ref, ycat_ref, q_sc, k_sc, v_sc, g_sc, gate_sc, dm_sc):
    @pl.when(pl.program_id(0) == 0)
    def _():
        _build_decay_tiles(dec_ref, dm_sc, T_CTX // Q_TILE)

    h_ref[...] = _rms_mod(x_ref[...], ng_ref[...], sc_ref[...], sh_ref[...]).astype(bf16)

    _run(_interleave(
        _lru_group(h_ref, win_ref, cw_ref, cb_ref, wg_ref, bg_ref, lam_ref, None, ycat_ref,
                   st_ref, gate_sc),
        _ctx_attention_group(h_ref, win_ref, C_QB, 2, qnb_ref, knb_ref, kb_ref, vb_ref,
                             ycat_ref, GW),
        _ctx_attention_group(h_ref, win_ref, C_QC, 4, qnc_ref, knc_ref, kc_ref, vc_ref,
                             ycat_ref, 2 * GW),
        _retention_group(h_ref, win_ref, dec_ref, gn_ref, None, ycat_ref, sret_ref,
                         q_sc, k_sc, v_sc, g_sc, dm_sc),
    ))
    y = _dot(ycat_ref[...].astype(bf16), wout_ref[...])
    o_ref[...] = x_ref[...] + gt_ref[...] * y


def _ctx_attention_group(h_ref, win_ref, col, n_kv, qn_ref, kn_ref, k_out_ref, v_out_ref,
                         ycat_ref, ycol):
    kw = n_kv * HD
    p = _dot(h_ref[...], win_ref[:, col:col + GW + 2 * kw])
    yield
    q = _headnorm(p[:, 0:GW], qn_ref[...] * ATT_SCALE).astype(bf16)
    yield
    kn = _headnorm(p[:, GW:GW + kw], kn_ref[...])
    k_out_ref[...] = kn
    v_out_ref[...] = p[:, GW + kw:GW + 2 * kw]
    k = kn.astype(bf16)
    v = p[:, GW + kw:GW + 2 * kw].astype(bf16)
    yield

    def head(hh):
        kvs = slice((hh * n_kv // 4) * HD, (hh * n_kv // 4 + 1) * HD)

        def store(o):
            ycat_ref[:, ycol + hh * HD:ycol + (hh + 1) * HD] = o

        return _attend_stages(lambda: [_dot_nt(q[:, hh * HD:(hh + 1) * HD], k[:, kvs])],
                              [v[:, kvs]], store)

    yield from _interleave(*[head(hh) for hh in range(4)])


def _lat_mixer_kernel(x_ref, sh_ref, sc_ref, gt_ref, ng_ref, win_ref, wout_ref,
                      cw_ref, cb_ref, wg_ref, bg_ref, lam_ref,
                      qnb_ref, knb_ref, qnc_ref, knc_ref, dec_ref, gn_ref,
                      cos_ref, sin_ref, bias_ref,
                      cbk_ref, cbv_ref, cck_ref, ccv_ref, lru0_ref, ret0_ref,
                      o_ref,
                      h_ref, ycat_ref, q_sc, k_sc, v_sc, g_sc, gate_sc, dm_sc,
                      qb_sc, kb_sc, vb_sc, qc_sc, kc_sc, vc_sc):
    T = T_LAT

    @pl.when(pl.program_id(0) == 0)
    def _():
        _build_decay_tiles(dec_ref, dm_sc, T // Q_TILE)

    h_ref[...] = _rms_mod(x_ref[...], ng_ref[...], sc_ref[...], sh_ref[...]).astype(bf16)

    def prep_b():
        cos = cos_ref[...]
        sin = sin_ref[...]
        p = _dot(h_ref[...], win_ref[:, C_QB:C_QB + GW])
        yield
        qb_sc[...] = _rope(_headnorm(p, qnb_ref[...] * ATT_SCALE), cos, sin).astype(bf16)
        yield
        p = _dot(h_ref[...], win_ref[:, C_KB:C_KB + GW])
        yield
        kb_sc[0:T, :] = _rope(_headnorm(p[:, 0:128], knb_ref[...]), cos, sin).astype(bf16)
        kb_sc[T:T + PAST, :] = cbk_ref[...].astype(bf16)
        vb_sc[0:T, :] = p[:, 128:GW].astype(bf16)
        vb_sc[T:T + PAST, :] = cbv_ref[...].astype(bf16)
        yield

    def prep_c():
        p = _dot(h_ref[...], win_ref[:, C_QC:C_QC + GW])
        yield
        qc_sc[...] = _headnorm(p, qnc_ref[...] * ATT_SCALE).astype(bf16)
        yield
        p = _dot(h_ref[...], win_ref[:, C_KC:C_KC + GW])
        yield
        kc_sc[0:T, :] = _headnorm(p, knc_ref[...]).astype(bf16)
        kc_sc[T:T + PAST, :] = cck_ref[...].astype(bf16)
        yield
        vc_sc[0:T, :] = _dot(h_ref[...], win_ref[:, C_VC:C_VC + GW]).astype(bf16)
        vc_sc[T:T + PAST, :] = ccv_ref[...].astype(bf16)
        yield

    ret_tiles = []
    retention = _retention_group(h_ref, win_ref, dec_ref, gn_ref, ret0_ref, ycat_ref, None,
                                 q_sc, k_sc, v_sc, g_sc, dm_sc, ret_tiles)

    def preps():
        yield from prep_b()
        yield from prep_c()
        for _ in range(RET_PREP_STAGES):
            next(retention)
            yield

    _run(_interleave(
        _every(_lru_group(h_ref, win_ref, cw_ref, cb_ref, wg_ref, bg_ref, lam_ref, lru0_ref[...],
                          ycat_ref, None, gate_sc), LRU_STAGES_PER_ROUND),
        preps()))
    _run(retention)

    kvk = [kb_sc[:, kv * HD:(kv + 1) * HD] for kv in range(2)]
    kvv = [vb_sc[:, kv * HD:(kv + 1) * HD] for kv in range(2)]

    def tile_b(i):
        rows = pl.ds(pl.multiple_of(i * Q_TILE, Q_TILE), Q_TILE)

        def head(hh):
            def store(o):
                ycat_ref[rows, GW + hh * HD:GW + (hh + 1) * HD] = o

            return _attend_stages(
                lambda: [_dot_nt(qb_sc[rows, hh * HD:(hh + 1) * HD], kvk[hh // 2])],
                [kvv[hh // 2]], store)

        return _interleave(*[head(hh) for hh in range(4)])

    n_loc = NAT_WR * GRID_W
    qcol = lax.broadcasted_iota(jnp.int32, (GRID_W, n_loc), 0)
    kcol = lax.broadcasted_iota(jnp.int32, (GRID_W, n_loc), 1) & (GRID_W - 1)
    cstart = jnp.clip(qcol - NAT_WC // 2, 0, GRID_W - NAT_WC)
    in_win = (kcol >= cstart) & (kcol < cstart + NAT_WC)

    def row_block(r):
        rs = jnp.clip(r - NAT_WR // 2, 0, GRID_R - NAT_WR)
        didx = rs - r + (NAT_WR - 1)
        qrows = pl.ds(pl.multiple_of(r * GRID_W, GRID_W), GRID_W)
        krows = pl.ds(pl.multiple_of(rs * GRID_W, GRID_W), n_loc)

        def head(hh):
            hs = slice(hh * HD, (hh + 1) * HD)

            def scores():
                qh = qc_sc[qrows, hs]
                s_loc = _dot_nt(qh, kc_sc[krows, hs])
                s_loc = jnp.where(in_win, s_loc + bias_ref[hh, didx], NEG_INF)
                return [s_loc, _dot_nt(qh, kc_sc[T:T + PAST, hs])]

            def store(o):
                ycat_ref[qrows, 2 * GW + hh * HD:2 * GW + (hh + 1) * HD] = o

            return _attend_stages(scores, [vc_sc[krows, hs], vc_sc[T:T + PAST, hs]], store)

        return _interleave(*[head(hh) for hh in range(4)])

    rows_per_tile = Q_TILE // GRID_W

    def fused_tile(i):
        def b_then_d():
            yield from tile_b(i)
            yield from _interleave(*[tile(i) for tile in ret_tiles])

        def c_rows():
            for rb in range(rows_per_tile):
                yield from row_block(i * rows_per_tile + rb)

        _run(_interleave(b_then_d(), c_rows()))

    _loop(T // Q_TILE, fused_tile)

    y = _dot(ycat_ref[...].astype(bf16), wout_ref[...])
    o_ref[...] = x_ref[...] + gt_ref[...] * y


def _layer_specs(l, mod_row, grid_rank=1):
    const = dict(pipeline_mode=pl.Buffered(1))

    def lspec(shape):
        nd = len(shape)
        return pl.BlockSpec((None,) + shape, lambda i: (l,) + (0,) * nd, **const)

    def mod_spec(j):
        return pl.BlockSpec((None, None, None, 1, D), lambda i: (l, j, mod_row(i), 0, 0))

    return [
        mod_spec(3), mod_spec(4), mod_spec(5),
        pl.BlockSpec((None, None, 1, D), lambda i: (l, 1, 0, 0)),
        lspec((D, IN_COLS)), lspec((D, D)),
        lspec((4, GW)), lspec((1, GW)), lspec((GW, 4 * GW)), lspec((1, 4 * GW)), lspec((2, GW)),
        lspec((1, GW)), lspec((1, 128)), lspec((1, GW)), lspec((1, GW)),
        lspec((8, 1, 1)), lspec((1, GW)),
    ]


def _ctx_mixer(x, mods, norm_g, lw, l, prev):
    T = T_CTX

    def layer_spec(*tail):
        return pl.BlockSpec((None, None) + tail, lambda i: (i, l) + (0,) * len(tail))

    tails = [(T, 128), (T, 128), (T, GW), (T, GW), (2, GW), (8 * HD, HD)]
    operands = [x, mods, mods, mods, norm_g, lw["w_in"], lw["w_out"],
                lw["conv_w"], lw["conv_b"], lw["wg"], lw["bg"], lw["lam"],
                lw["qnb"], lw["knb"], lw["qnc"], lw["knc"], lw["dec"], lw["gn"]]
    assert len(operands) == N_CTX_IN
    aliases = {0: 0}
    prev_specs = []
    if prev is not None:
        prev_specs = [pl.BlockSpec(memory_space=pl.ANY)] * len(prev)
        aliases.update({N_CTX_IN + k: 1 + k for k in range(len(prev))})
        operands = operands + list(prev)

    outs = pl.pallas_call(
        _ctx_mixer_kernel,
        grid=(N_CTX,),
        in_specs=[pl.BlockSpec((T, D), lambda i: (i, 0))] + _layer_specs(l, lambda i: 0)
        + prev_specs,
        out_specs=[pl.BlockSpec((T, D), lambda i: (i, 0))] + [layer_spec(*t) for t in tails],
        out_shape=[jax.ShapeDtypeStruct((M_ALL, D), f32)]
        + [jax.ShapeDtypeStruct((N_CTX, DEPTH) + t, f32) for t in tails],
        scratch_shapes=[
            pltpu.VMEM((T, D), bf16), pltpu.VMEM((T, D), f32),
            pltpu.VMEM((T, GW), bf16), pltpu.VMEM((T, GW), bf16), pltpu.VMEM((T, GW), bf16),
            pltpu.VMEM((T, GW), f32), pltpu.VMEM((T, GW), f32),
            pltpu.VMEM((4, Q_TILE, Q_TILE), f32),
        ],
        input_output_aliases=aliases,
        compiler_params=pltpu.CompilerParams(
            dimension_semantics=("arbitrary",), vmem_limit_bytes=VMEM_LIMIT),
        name="ctx_mixer",
    )(*operands)
    return outs[0], tuple(outs[1:])


def _lat_mixer(x, mods, norm_g, lw, l, cos, sin, caches):
    T = T_LAT
    blk0 = M_CTX // T
    const = dict(pipeline_mode=pl.Buffered(1))
    cbk, cbv, cck, ccv, lru0, ret0 = caches

    def cache_spec(w):
        return pl.BlockSpec((None, None, PAST, w), lambda i: (i, l, 0, 0))

    return pl.pallas_call(
        _lat_mixer_kernel,
        grid=(N_LAT,),
        in_specs=[pl.BlockSpec((T, D), lambda i: (blk0 + i, 0), **const)]
        + _layer_specs(l, lambda i: 1 + i)
        + [
            pl.BlockSpec((T, 128), lambda i: (0, 0), **const),
            pl.BlockSpec((T, 128), lambda i: (0, 0), **const),
            pl.BlockSpec((None, 4, NAT_WR, GRID_W, NAT_WR * GRID_W),
                         lambda i: (l, 0, 0, 0, 0), **const),
            cache_spec(128), cache_spec(128), cache_spec(GW), cache_spec(GW),
            pl.BlockSpec((None, None, 2, GW), lambda i: (i, l, 0, 0)),
            pl.BlockSpec((None, None, 8 * HD, HD), lambda i: (i, l, 0, 0)),
        ],
        out_specs=pl.BlockSpec((T, D), lambda i: (blk0 + i, 0), **const),
        out_shape=jax.ShapeDtypeStruct((M_ALL, D), f32),
        scratch_shapes=[
            pltpu.VMEM((T, D), bf16), pltpu.VMEM((T, D), f32),
            pltpu.VMEM((T, GW), bf16), pltpu.VMEM((T, GW), bf16), pltpu.VMEM((T, GW), bf16),
            pltpu.VMEM((T, GW), f32), pltpu.VMEM((T, GW), f32),
            pltpu.VMEM((4 * (2 * (T // Q_TILE) - 1), Q_TILE, Q_TILE), f32),
            pltpu.VMEM((T, GW), bf16),
            pltpu.VMEM((T + PAST, 128), bf16), pltpu.VMEM((T + PAST, 128), bf16),
            pltpu.VMEM((T, GW), bf16),
            pltpu.VMEM((T + PAST, GW), bf16), pltpu.VMEM((T + PAST, GW), bf16),
        ],
        input_output_aliases={0: 0},
        compiler_params=pltpu.CompilerParams(
            dimension_semantics=("arbitrary",), vmem_limit_bytes=VMEM_LIMIT),
        name="lat_mixer",
    )(x, mods, mods, mods, norm_g, lw["w_in"], lw["w_out"],
      lw["conv_w"], lw["conv_b"], lw["wg"], lw["bg"], lw["lam"],
      lw["qnb"], lw["knb"], lw["qnc"], lw["knc"], lw["dec"], lw["gn"],
      cos, sin, lw["nat_bias"], cbk, cbv, cck, ccv, lru0, ret0)


def _block_diag(w):
    eye = jnp.eye(4, dtype=w.dtype)
    return jnp.einsum("lncd,nm->lncmd", w, eye).reshape(DEPTH, GW, GW)


def _rope_tables():
    t = jnp.arange(T_LAT)
    row = (t // GRID_W).astype(f32)
    col = (t % GRID_W).astype(f32)
    n_freq = HD // 4
    inv = ROPE_BASE ** (-jnp.arange(n_freq, dtype=f32) / n_freq)
    ang = jnp.concatenate([row[:, None] * inv, col[:, None] * inv], axis=-1)
    cos = jnp.repeat(jnp.cos(ang), 2, axis=-1)
    sin = jnp.repeat(jnp.sin(ang), 2, axis=-1)
    sign = jnp.where(jnp.arange(HD) % 2 == 0, -1.0, 1.0).astype(f32)
    return jnp.tile(cos, (1, 2)), jnp.tile(sin * sign, (1, 2))


def _nat_bias_windows(nat_bias):
    pad = GRID_W - NAT_WC
    ext = jnp.concatenate([jnp.repeat(nat_bias[..., :1], pad, -1), nat_bias,
                           jnp.repeat(nat_bias[..., -1:], pad + 1, -1)], axis=-1)
    t = jnp.tile(ext, (1, 1, 1, GRID_W))[..., :GRID_W * 127]
    t = t.reshape(DEPTH, 4, 2 * NAT_WR - 1, GRID_W, 127)[..., GRID_W - 1:]
    tab = jnp.stack([t[:, :, d:d + NAT_WR] for d in range(NAT_WR)], axis=2)
    tab = tab.transpose(0, 1, 2, 4, 3, 5)
    return tab.reshape(DEPTH, 4, NAT_WR, GRID_W, NAT_WR * GRID_W)


def kernel(x_prompt, x_sample, cache_b_k, cache_b_v, cache_c_k, cache_c_v, state_lru, state_ret, c, c_ctx, w_mod, b_mod, norm_g, ffn_w_in, ffn_w_out, w_in, w_out, conv_w, conv_b, lru_w_r, lru_b_r, lru_w_i, lru_b_i, lru_lambda, gqa_qn, gqa_kn, nat_qn, nat_kn, nat_bias, ret_decay, ret_gn):
    cvec = jnp.concatenate([c_ctx[None, :], c, jnp.zeros((8 - 1 - N_LAT, D), f32)], axis=0)
    mods = _modulation(cvec, w_mod, b_mod).reshape(DEPTH, N_MOD, 8, 1, D)
    ng = norm_g.reshape(DEPTH, 3, 1, D)

    wi = ffn_w_in.astype(bf16)
    wo = ffn_w_out.astype(bf16)
    lw = dict(
        w_in=w_in.astype(bf16),
        w_out=w_out.astype(bf16),
        conv_w=conv_w,
        conv_b=conv_b.reshape(DEPTH, 1, GW),
        wg=jnp.concatenate([_block_diag(lru_w_r[:, 0]), _block_diag(lru_w_i[:, 0]),
                            _block_diag(lru_w_r[:, 1]), _block_diag(lru_w_i[:, 1])],
                           axis=-1).astype(bf16),
        bg=jnp.concatenate([lru_b_r[:, 0], lru_b_i[:, 0], lru_b_r[:, 1], lru_b_i[:, 1]],
                           axis=-1).reshape(DEPTH, 1, 4 * GW),
        lam=lru_lambda,
        qnb=jnp.tile(gqa_qn, (1, 4)).reshape(DEPTH, 1, GW),
        knb=jnp.tile(gqa_kn, (1, 2)).reshape(DEPTH, 1, 128),
        qnc=jnp.tile(nat_qn, (1, 4)).reshape(DEPTH, 1, GW),
        knc=jnp.tile(nat_kn, (1, 4)).reshape(DEPTH, 1, GW),
        dec=ret_decay.reshape(DEPTH, 8, 1, 1),
        gn=ret_gn.reshape(DEPTH, 1, GW),
        nat_bias=_nat_bias_windows(nat_bias),
    )
    cos, sin = _rope_tables()
    caches = (cache_b_k.reshape(N_LAT, DEPTH, PAST, 128), cache_b_v.reshape(N_LAT, DEPTH, PAST, 128),
              cache_c_k.reshape(N_LAT, DEPTH, PAST, GW), cache_c_v.reshape(N_LAT, DEPTH, PAST, GW),
              state_lru, state_ret.reshape(N_LAT, DEPTH, 8 * HD, HD))

    x = (x_prompt.reshape(M_CTX, D), x_sample.reshape(M_LAT, D))
    new = None
    for l in range(DEPTH):
        x = _ffn(x, mods, ng, wi, wo, l, 0)
        x, new = _ctx_mixer(x, mods, ng, lw, l, new)
        x = _lat_mixer(x, mods, ng, lw, l, cos, sin, caches)
        x = _ffn(x, mods, ng, wi, wo, l, 1, split_out=(l == DEPTH - 1))

    new_bk, new_bv, new_ck, new_cv, new_lru, new_ret = new
    return (x[0].reshape(N_CTX, T_CTX, D), x[1].reshape(N_LAT, T_LAT, D),
            new_bk.reshape(N_CTX, DEPTH, T_CTX, 2, HD), new_bv.reshape(N_CTX, DEPTH, T_CTX, 2, HD),
            new_ck.reshape(N_CTX, DEPTH, T_CTX, 4, HD), new_cv.reshape(N_CTX, DEPTH, T_CTX, 4, HD),
            new_lru, new_ret.reshape(N_CTX, DEPTH, 2, 4, HD, HD))
```

```python
import functools
import math

import jax
import jax.numpy as jnp
from jax import lax
from jax.experimental import pallas as pl
from jax.experimental.pallas import tpu as pltpu

f32 = jnp.float32
bf16 = jnp.bfloat16

D = 1024
DEPTH = 4
N_CTX, T_CTX = 32, 256
N_LAT, T_LAT = 4, 1024
M_CTX = N_CTX * T_CTX
M_LAT = N_LAT * T_LAT
M_ALL = M_CTX + M_LAT
PAST = 256
GRID_W = 64
GRID_R = T_LAT // GRID_W
HD = 64
GW = 256
LRU_C = 8.0
NAT_WR, NAT_WC = 8, 16
D_FF = 2816
IN_COLS = 2816
EPS = 1e-6
NEG_INF = -1e30
N_MOD = 9
ATT_SCALE = HD ** -0.5
ROPE_BASE = 10000.0

C_XA, C_GA = 0, 256
C_QB, C_KB, C_VB = 512, 768, 896
C_QC, C_KC, C_VC = 1024, 1280, 1536
C_QD, C_KD, C_VD, C_GD = 1792, 2048, 2304, 2560

FFN_TM = 1024
FFN_SUB = 512
FFN_CHUNKS = ((0, 768), (768, 1536), (1536, 2304), (2304, 2816))
Q_TILE = 256
LRU_STAGES_PER_ROUND = 4
RET_PREP_STAGES = 4
VMEM_LIMIT = 58 * 1024 * 1024


def _dot(a, b):
    return jnp.dot(a, b, preferred_element_type=f32)


def _dot_nt(a, b):
    return lax.dot_general(a, b, (((1,), (1,)), ((), ())), preferred_element_type=f32)


def _dot_tn(a, b):
    return lax.dot_general(a, b, (((0,), (0,)), ((), ())), preferred_element_type=f32)


def _softplus(z):
    return jnp.maximum(z, 0.0) + jnp.log1p(jnp.exp(-jnp.abs(z)))


def _sigmoid(z):
    return 1.0 / (1.0 + jnp.exp(-z))


def _silu(z):
    return z * _sigmoid(z)


def _gelu_tanh(z):
    c = math.sqrt(2.0 / math.pi)
    return z * (0.5 * (1.0 + jnp.tanh(c * (z + 0.044715 * (z * z * z)))))


def _rms_mod(x, ng, sc, sh):
    ms = jnp.mean(x * x, axis=-1, keepdims=True)
    y = x * lax.rsqrt(ms + EPS) * ng
    return y * (1.0 + sc) + sh


def _headnorm(x, gain):
    T, W = x.shape
    lane = lax.broadcasted_iota(jnp.int32, (T, 128), 1)
    lo = lane < HD
    cols = []
    for c in range(W // 128):
        xc = x[:, c * 128:(c + 1) * 128]
        x2 = xc * xc
        s_lo = jnp.sum(jnp.where(lo, x2, 0.0), axis=-1, keepdims=True)
        s_hi = jnp.sum(jnp.where(lo, 0.0, x2), axis=-1, keepdims=True)
        r_lo = lax.rsqrt(s_lo * (1.0 / HD) + EPS)
        r_hi = lax.rsqrt(s_hi * (1.0 / HD) + EPS)
        cols.append(xc * jnp.where(lo, r_lo, r_hi))
    y = cols[0] if len(cols) == 1 else jnp.concatenate(cols, axis=-1)
    return y * gain


def _keep_half(x, e):
    T, W = x.shape
    lo = lax.broadcasted_iota(jnp.int32, (T, 128), 1) < HD
    keep = lo if e == 0 else jnp.logical_not(lo)
    cols = [jnp.where(keep, x[:, c * 128:(c + 1) * 128], 0.0) for c in range(W // 128)]
    return cols[0] if len(cols) == 1 else jnp.concatenate(cols, axis=-1)


def _rope(x, cos, sin_signed):
    T, W = x.shape
    lane = lax.broadcasted_iota(jnp.int32, (T, 128), 1)
    even = (lane & 1) == 0
    cols = []
    for c in range(W // 128):
        xc = x[:, c * 128:(c + 1) * 128]
        sw = jnp.where(even, pltpu.roll(xc, 127, 1), pltpu.roll(xc, 1, 1))
        cols.append(xc * cos + sw * sin_signed)
    return cols[0] if len(cols) == 1 else jnp.concatenate(cols, axis=-1)


def _attend_stages(scores, v_list, store):
    s_list = scores()
    yield
    m = s_list[0].max(axis=-1, keepdims=True)
    for s in s_list[1:]:
        m = jnp.maximum(m, s.max(axis=-1, keepdims=True))
    yield
    p_list = [jnp.exp(s - m) for s in s_list]
    l = None
    for p in p_list:
        ls = jnp.sum(p, axis=-1, keepdims=True)
        l = ls if l is None else l + ls
    yield
    o = None
    for p, v in zip(p_list, v_list):
        os_ = _dot(p.astype(bf16), v)
        o = os_ if o is None else o + os_
    store(o * (1.0 / l))
    yield


def _interleave(*stages):
    live = list(stages)
    while live:
        for g in list(live):
            try:
                next(g)
            except StopIteration:
                live.remove(g)
        yield


def _every(stages, n):
    while True:
        for _ in range(n):
            try:
                next(stages)
            except StopIteration:
                return
        yield


def _delayed(stages, n):
    for _ in range(n):
        yield
    yield from stages


def _run(stages):
    for _ in stages:
        pass


def _loop(n, body):
    def step(i, carry):
        body(i)
        return carry
    lax.fori_loop(0, n, step, 0)


def _mod_kernel(c_ref, w_ref, b_ref, o_ref):
    cs = _silu(c_ref[...]).astype(bf16)
    o_ref[...] = _dot(cs, w_ref[...].astype(bf16)) + b_ref[...]


def _modulation(cvec, w_mod, b_mod):
    return pl.pallas_call(
        _mod_kernel,
        grid=(DEPTH, N_MOD),
        in_specs=[
            pl.BlockSpec((8, D), lambda l, j: (0, 0)),
            pl.BlockSpec((None, D, D), lambda l, j: (l, 0, j)),
            pl.BlockSpec((None, None, 1, D), lambda l, j: (l, j, 0, 0)),
        ],
        out_specs=pl.BlockSpec((None, None, 8, D), lambda l, j: (l, j, 0, 0)),
        out_shape=jax.ShapeDtypeStruct((DEPTH, N_MOD, 8, D), f32),
        compiler_params=pltpu.CompilerParams(
            dimension_semantics=("parallel", "parallel"), vmem_limit_bytes=VMEM_LIMIT),
        name="adaln_mod",
    )(cvec, w_mod, b_mod.reshape(DEPTH, N_MOD, 1, D))


def _ffn_rows(load, store, sh_ref, sc_ref, gt_ref, ng_ref, wi_ref, wo_ref):
    x = load()
    h = _rms_mod(x, ng_ref[...], sc_ref[...], sh_ref[...]).astype(bf16)
    yield
    acc = None
    for c0, c1 in FFN_CHUNKS:
        a = _dot(h, wi_ref[:, c0:c1])
        b = _dot(h, wi_ref[:, D_FF + c0:D_FF + c1])
        g = (_silu(a) * b).astype(bf16)
        y = _dot(g, wo_ref[c0:c1, :])
        acc = y if acc is None else acc + y
        yield
    store(x + (0.5 * gt_ref[...]) * acc)


_FFN_PASSES = tuple(slice(r, r + FFN_SUB) for r in range(0, FFN_TM, FFN_SUB))


def _ffn_passes(load, store, *refs):
    _run(_interleave(*[
        _delayed(_ffn_rows(functools.partial(load, rows), functools.partial(store, rows), *refs), k)
        for k, rows in enumerate(_FFN_PASSES)]))


def _ffn_kernel(x_ref, sh_ref, sc_ref, gt_ref, ng_ref, wi_ref, wo_ref, o_ref):
    def store(rows, v):
        o_ref[rows] = v

    _ffn_passes(lambda rows: x_ref[rows], store, sh_ref, sc_ref, gt_ref, ng_ref, wi_ref, wo_ref)


def _ffn_split_in_kernel(xc_ref, xl_ref, sh_ref, sc_ref, gt_ref, ng_ref, wi_ref, wo_ref, o_ref):
    is_ctx = pl.program_id(0) < M_CTX // FFN_TM

    def store(rows, v):
        o_ref[rows] = v

    _ffn_passes(lambda rows: jnp.where(is_ctx, xc_ref[rows], xl_ref[rows]), store,
                sh_ref, sc_ref, gt_ref, ng_ref, wi_ref, wo_ref)


def _ffn_split_out_kernel(x_ref, sh_ref, sc_ref, gt_ref, ng_ref, wi_ref, wo_ref,
                          oc_ref, ol_ref):
    outs = {}

    def keep(rows, v):
        outs[rows.start] = v

    _ffn_passes(lambda rows: x_ref[rows], keep, sh_ref, sc_ref, gt_ref, ng_ref, wi_ref, wo_ref)
    is_ctx = pl.program_id(0) < M_CTX // FFN_TM

    @pl.when(is_ctx)
    def _():
        for rows in _FFN_PASSES:
            oc_ref[rows] = outs[rows.start]

    @pl.when(jnp.logical_not(is_ctx))
    def _():
        for rows in _FFN_PASSES:
            ol_ref[rows] = outs[rows.start]


def _mod_row_of_tile(i, rows_per_tile):
    n_ctx = M_CTX // rows_per_tile
    per_b = T_LAT // rows_per_tile
    return jnp.where(i < n_ctx, 0, 1 + jnp.maximum(i - n_ctx, 0) // per_b)


def _ffn(xs, mods, norm_g, wi, wo, l, f, split_out=False):
    j0 = 6 * f
    n_ctx = M_CTX // FFN_TM
    row = functools.partial(_mod_row_of_tile, rows_per_tile=FFN_TM)

    def mod_spec(j):
        return pl.BlockSpec((None, None, None, 1, D), lambda i: (l, j, row(i), 0, 0))

    ctx_rows = pl.BlockSpec((FFN_TM, D), lambda i: (jnp.minimum(i, n_ctx - 1), 0))
    lat_rows = pl.BlockSpec((FFN_TM, D), lambda i: (jnp.maximum(i - n_ctx, 0), 0))
    all_rows = pl.BlockSpec((FFN_TM, D), lambda i: (i, 0))
    split_in = isinstance(xs, tuple)
    assert not (split_in and split_out)
    if split_in:
        body, x_specs, xs = _ffn_split_in_kernel, [ctx_rows, lat_rows], list(xs)
    else:
        body, x_specs, xs = (_ffn_split_out_kernel if split_out else _ffn_kernel), [all_rows], [xs]
    if split_out:
        out_specs = [ctx_rows, lat_rows]
        out_shape = [jax.ShapeDtypeStruct((M_CTX, D), f32), jax.ShapeDtypeStruct((M_LAT, D), f32)]
    else:
        out_specs = all_rows
        out_shape = jax.ShapeDtypeStruct((M_ALL, D), f32)

    const = dict(pipeline_mode=pl.Buffered(1))
    return pl.pallas_call(
        body,
        grid=(M_ALL // FFN_TM,),
        in_specs=x_specs + [
            mod_spec(j0), mod_spec(j0 + 1), mod_spec(j0 + 2),
            pl.BlockSpec((None, None, 1, D), lambda i: (l, 2 * f, 0, 0)),
            pl.BlockSpec((None, None, D, 2 * D_FF), lambda i: (l, f, 0, 0), **const),
            pl.BlockSpec((None, None, D_FF, D), lambda i: (l, f, 0, 0), **const),
        ],
        out_specs=out_specs,
        out_shape=out_shape,
        compiler_params=pltpu.CompilerParams(
            dimension_semantics=("arbitrary",), vmem_limit_bytes=VMEM_LIMIT),
        name=f"ffn{f}",
    )(*xs, mods, mods, mods, norm_g, wi, wo)


def _scan8(a, b, h_init, reverse, emit):
    T, C = a.shape
    nb = T // 8
    row8 = lax.broadcasted_iota(jnp.int32, (8, C), 0)
    for s in (1, 2, 4):
        if reverse:
            a_r = pltpu.roll(a, T - s, 0)
            b_r = pltpu.roll(b, T - s, 0)
            edge = row8 < 8 - s
            a_s = jnp.concatenate([a_r[:T - 8], jnp.where(edge, a_r[T - 8:], 1.0)], axis=0)
            b_s = jnp.concatenate([b_r[:T - 8], jnp.where(edge, b_r[T - 8:], 0.0)], axis=0)
        else:
            a_r = pltpu.roll(a, s, 0)
            b_r = pltpu.roll(b, s, 0)
            edge = row8 >= s
            a_s = jnp.concatenate([jnp.where(edge, a_r[:8], 1.0), a_r[8:]], axis=0)
            b_s = jnp.concatenate([jnp.where(edge, b_r[:8], 0.0), b_r[8:]], axis=0)
        b = a * b_s + b
        a = a * a_s
        yield
    h = jnp.broadcast_to(h_init, (8, C))
    for n, v in enumerate(range(nb - 1, -1, -1) if reverse else range(nb)):
        h = a[v * 8:(v + 1) * 8] * h + b[v * 8:(v + 1) * 8]
        emit(v, h)
        if n % 4 == 3:
            yield
    return h[0:1] if reverse else h[7:8]


def _lru_group(h_ref, win_ref, cw_ref, cb_ref, wg_ref, bg_ref, lam_ref, h0, ycat_ref, st_ref,
               gate_sc):
    T = h_ref.shape[0]
    p = _dot(h_ref[...], win_ref[:, C_XA:C_XA + 2 * GW])
    xa = p[:, :GW]
    yield
    gate_sc[...] = _gelu_tanh(p[:, GW:])
    yield
    tpos = lax.broadcasted_iota(jnp.int32, (T, GW), 0)

    def shifted(k):
        if k > 0:
            return jnp.where(tpos >= k, pltpu.roll(xa, k, 0), 0.0)
        return jnp.where(tpos < T + k, pltpu.roll(xa, T + k, 0), 0.0)

    cw = cw_ref[...]
    xc = shifted(2) * cw[0:1] + shifted(1) * cw[1:2] + xa * cw[2:3] + shifted(-1) * cw[3:4]
    xc = xc + cb_ref[...]
    xcb = xc.astype(bf16)
    lam = lam_ref[...]
    yield
    for d in range(2):
        ds_ = slice(d * 2 * GW, (d + 1) * 2 * GW)
        pre = _dot(xcb, wg_ref[:, ds_]) + bg_ref[:, ds_]
        yield
        r = _sigmoid(pre[:, 0:GW])
        gi = _sigmoid(pre[:, GW:2 * GW])
        yield
        log_a = (-LRU_C * r) * _softplus(-lam[d:d + 1])
        a = jnp.exp(log_a)
        u = jnp.sqrt(1.0 - a * a) * (gi * xc)
        yield
        h_init = jnp.zeros((1, GW), f32) if h0 is None else h0[d:d + 1]

        def emit_fwd(v, hv):
            ycat_ref[v * 8:(v + 1) * 8, 0:GW] = hv

        def emit_bwd(v, hv):
            rows = slice(v * 8, (v + 1) * 8)
            ycat_ref[rows, 0:GW] = (ycat_ref[rows, 0:GW] + hv) * gate_sc[rows, :]

        last = yield from _scan8(a, u, h_init, d == 1, emit_bwd if d == 1 else emit_fwd)
        if st_ref is not None:
            st_ref[d:d + 1, :] = last
        yield


def _build_decay_tiles(dec_ref, dm_sc, nt):
    r = lax.broadcasted_iota(jnp.int32, (Q_TILE, Q_TILE), 0)
    c = lax.broadcasted_iota(jnp.int32, (Q_TILE, Q_TILE), 1)
    base = (r - c).astype(f32)
    for h in range(4):
        lgf = -_softplus(-dec_ref[h])
        lgb = -_softplus(-dec_ref[4 + h])
        for m in range(-(nt - 1), nt):
            diff = base + float(m * Q_TILE)
            if m > 0:
                mat = jnp.exp(lgf * diff)
            elif m < 0:
                mat = jnp.exp(lgb * (-diff))
            else:
                mat = (jnp.where(diff >= 0, jnp.exp(lgf * jnp.maximum(diff, 0.0)), 0.0)
                       + jnp.where(diff <= 0, jnp.exp(lgb * jnp.maximum(-diff, 0.0)), 0.0))
            dm_sc[h * (2 * nt - 1) + m + nt - 1] = mat


def _retention_group(h_ref, win_ref, dec_ref, gn_ref, s0_ref, ycat_ref, sret_ref,
                     q_sc, k_sc, v_sc, g_sc, dm_sc, tiles_out=None):
    T = h_ref.shape[0]
    nt = T // Q_TILE
    q_sc[...] = _dot(h_ref[...], win_ref[:, C_QD:C_QD + GW]).astype(bf16)
    yield
    kf = _dot(h_ref[...], win_ref[:, C_KD:C_KD + GW]) * ATT_SCALE
    k_sc[...] = kf.astype(bf16)
    yield
    v_sc[...] = _dot(h_ref[...], win_ref[:, C_VD:C_VD + GW]).astype(bf16)
    yield
    g_sc[...] = _dot(h_ref[...], win_ref[:, C_GD:C_GD + GW])
    gn = gn_ref[...]
    tcol = lax.broadcasted_iota(jnp.int32, (T, 1), 0).astype(f32)
    yield

    def head(h):
        hs = slice(h * HD, (h + 1) * HD)
        lgf = -_softplus(-dec_ref[h])
        lgb = -_softplus(-dec_ref[4 + h])
        kh = k_sc[:, hs]
        vh = v_sc[:, hs]
        if sret_ref is not None:
            zf = jnp.exp(lgf * ((T - 1.0) - tcol))
            zb = jnp.exp(lgb * tcol)
            kff = kf[:, hs]
            sret_ref[h * HD:(h + 1) * HD, :] = _dot_tn((kff * zf).astype(bf16), vh)
            sret_ref[(4 + h) * HD:(5 + h) * HD, :] = _dot_tn((kff * zb).astype(bf16), vh)
        if s0_ref is not None:
            s0f = s0_ref[h * HD:(h + 1) * HD, :].astype(bf16)
            s0b = s0_ref[(4 + h) * HD:(5 + h) * HD, :].astype(bf16)
        gnh = gn[:, hs]

        def tile(i):
            t0 = pl.multiple_of(i * Q_TILE, Q_TILE)
            rows = pl.ds(t0, Q_TILE)
            qh = q_sc[rows, hs]
            o = None
            for j in range(nt):
                ks = slice(j * Q_TILE, (j + 1) * Q_TILE)
                dm = dm_sc[h * (2 * nt - 1) + (nt - 1 - j) + i]
                sd = (_dot_nt(qh, kh[ks]) * dm).astype(bf16)
                yield
                oj = _dot(sd, vh[ks])
                o = oj if o is None else o + oj
                yield
            if s0_ref is not None:
                tq = (lax.broadcasted_iota(jnp.int32, (Q_TILE, 1), 0) + t0).astype(f32)
                o = o + _dot(qh, s0f) * jnp.exp(lgf * (tq + 1.0))
                o = o + _dot(qh, s0b) * jnp.exp(lgb * (T - tq))
            mu = jnp.mean(o, axis=-1, keepdims=True)
            oc = o - mu
            var = jnp.mean(oc * oc, axis=-1, keepdims=True)
            yield
            on = oc * lax.rsqrt(var + EPS) * gnh
            ycat_ref[rows, 3 * GW + h * HD:3 * GW + (h + 1) * HD] = on * _silu(g_sc[rows, hs])
            yield

        return tile

    if nt == 1:
        for h in range(4):
            tile = head(h)
            yield
            yield from tile(0)
    else:
        tiles_out.extend(head(h) for h in range(4))


N_CTX_IN = 18


def _ctx_mixer_kernel(*refs):
    n_out = 7
    n_scratch = 8
    ins = refs[:N_CTX_IN]
    outs = refs[len(refs) - n_out - n_scratch:]
    _ctx_mixer_body(*ins, *outs)


def _ctx_mixer_body(x_ref, sh_ref, sc_ref, gt_ref, ng_ref, win_ref, wout_ref,
                    cw_ref, cb_ref, wg_ref, bg_ref, lam_ref,
                    qnb_ref, knb_ref, qnc_ref, knc_ref, dec_ref, gn_ref,
                    o_ref, kb_ref, vb_ref, kc_ref, vc_ref, st_ref, sret_ref,
                    h_ref, ycat_ref, q_sc, k_sc, v_sc, g_sc, gate_sc, dm_sc):
    @pl.when(pl.program_id(0) == 0)
    def _():
        _build_decay_tiles(dec_ref, dm_sc, T_CTX // Q_TILE)

    h_ref[...] = _rms_mod(x_ref[...], ng_ref[...], sc_ref[...], sh_ref[...]).astype(bf16)

    _run(_interleave(
        _lru_group(h_ref, win_ref, cw_ref, cb_ref, wg_ref, bg_ref, lam_ref, None, ycat_ref,
                   st_ref, gate_sc),
        _ctx_attention_group(h_ref, win_ref, C_QB, 2, qnb_ref, knb_ref, kb_ref, vb_ref,
                             ycat_ref, GW),
        _ctx_attention_group(h_ref, win_ref, C_QC, 4, qnc_ref, knc_ref, kc_ref, vc_ref,
                             ycat_ref, 2 * GW),
        _retention_group(h_ref, win_ref, dec_ref, gn_ref, None, ycat_ref, sret_ref,
                         q_sc, k_sc, v_sc, g_sc, dm_sc),
    ))
    y = _dot(ycat_ref[...].astype(bf16), wout_ref[...])
    o_ref[...] = x_ref[...] + gt_ref[...] * y


def _ctx_attention_group(h_ref, win_ref, col, n_kv, qn_ref, kn_ref, k_out_ref, v_out_ref,
                         ycat_ref, ycol):
    kw = n_kv * HD
    p = _dot(h_ref[...], win_ref[:, col:col + GW + 2 * kw])
    yield
    qf = _headnorm(p[:, 0:GW], qn_ref[...] * ATT_SCALE)
    yield
    kn = _headnorm(p[:, GW:GW + kw], kn_ref[...])
    vf = p[:, GW + kw:GW + 2 * kw]
    k_out_ref[...] = kn
    v_out_ref[...] = vf
    k = kn.astype(bf16)
    v = vf.astype(bf16)
    if n_kv == 2:
        k_sw = pltpu.roll(kn, HD, 1).astype(bf16)
        v_sw = pltpu.roll(vf, HD, 1).astype(bf16)
    yield
    lo = lax.broadcasted_iota(jnp.int32, (p.shape[0], 128), 1) < HD
    pending = {}

    def head(hh):
        j, e = divmod(hh, 2)
        pair = slice(j * 128, (j + 1) * 128)
        if n_kv == 4:
            kk, vv = k[:, pair], v[:, pair]
        else:
            kk, vv = (k, v) if j == e else (k_sw, v_sw)

        def scores():
            qm = jnp.where(lo if e == 0 else jnp.logical_not(lo), qf[:, pair], 0.0).astype(bf16)
            return [_dot_nt(qm, kk)]

        def store(o):
            if e == 0:
                pending[j] = o
            else:
                ycat_ref[:, ycol + j * 128:ycol + (j + 1) * 128] = jnp.where(lo, pending[j], o)

        return _attend_stages(scores, [vv], store)

    yield from _interleave(*[head(hh) for hh in range(4)])


def _lat_mixer_kernel(x_ref, sh_ref, sc_ref, gt_ref, ng_ref, win_ref, wout_ref,
                      cw_ref, cb_ref, wg_ref, bg_ref, lam_ref,
                      qnb_ref, knb_ref, qnc_ref, knc_ref, dec_ref, gn_ref,
                      cos_ref, sin_ref, bias_ref,
                      cbk_ref, cbv_ref, cck_ref, ccv_ref, lru0_ref, ret0_ref,
                      o_ref,
                      h_ref, q_sc, k_sc, v_sc, g_sc, gate_sc, dm_sc,
                      qb_sc, kb_sc, vb_sc, qc_sc, kc_sc, vc_sc):
    T = T_LAT
    ycat_ref = o_ref

    @pl.when(pl.program_id(0) == 0)
    def _():
        _build_decay_tiles(dec_ref, dm_sc, T // Q_TILE)

    h_ref[...] = _rms_mod(x_ref[...], ng_ref[...], sc_ref[...], sh_ref[...]).astype(bf16)

    def prep_b():
        cos = cos_ref[...]
        sin = sin_ref[...]
        p = _dot(h_ref[...], win_ref[:, C_QB:C_QB + GW])
        yield
        qr = _rope(_headnorm(p, qnb_ref[...] * ATT_SCALE), cos, sin)
        for e in range(2):
            qb_sc[e] = _keep_half(qr, e).astype(bf16)
        yield
        p = _dot(h_ref[...], win_ref[:, C_KB:C_KB + GW])
        yield
        for dst, lat, ctx in ((kb_sc, _rope(_headnorm(p[:, 0:128], knb_ref[...]), cos, sin),
                               cbk_ref[...]),
                              (vb_sc, p[:, 128:GW], cbv_ref[...])):
            dst[0, 0:T, :] = lat.astype(bf16)
            dst[0, T:T + PAST, :] = ctx.astype(bf16)
            dst[1, 0:T, :] = pltpu.roll(lat, HD, 1).astype(bf16)
            dst[1, T:T + PAST, :] = pltpu.roll(ctx, HD, 1).astype(bf16)
        yield

    def prep_c():
        p = _dot(h_ref[...], win_ref[:, C_QC:C_QC + GW])
        yield
        qn = _headnorm(p, qnc_ref[...] * ATT_SCALE)
        for e in range(2):
            qc_sc[e] = _keep_half(qn, e).astype(bf16)
        yield
        p = _dot(h_ref[...], win_ref[:, C_KC:C_KC + GW])
        yield
        kc_sc[0:T, :] = _headnorm(p, knc_ref[...]).astype(bf16)
        kc_sc[T:T + PAST, :] = cck_ref[...].astype(bf16)
        yield
        vc_sc[0:T, :] = _dot(h_ref[...], win_ref[:, C_VC:C_VC + GW]).astype(bf16)
        vc_sc[T:T + PAST, :] = ccv_ref[...].astype(bf16)
        yield

    ret_tiles = []
    retention = _retention_group(h_ref, win_ref, dec_ref, gn_ref, ret0_ref, ycat_ref, None,
                                 q_sc, k_sc, v_sc, g_sc, dm_sc, ret_tiles)

    def preps():
        yield from prep_b()
        yield from prep_c()
        for _ in range(RET_PREP_STAGES):
            next(retention)
            yield

    _run(_interleave(
        _every(_lru_group(h_ref, win_ref, cw_ref, cb_ref, wg_ref, bg_ref, lam_ref, lru0_ref[...],
                          ycat_ref, None, gate_sc), LRU_STAGES_PER_ROUND),
        preps()))
    _run(retention)

    lo_t = lax.broadcasted_iota(jnp.int32, (Q_TILE, 128), 1) < HD
    lo_r = lax.broadcasted_iota(jnp.int32, (GRID_W, 128), 1) < HD
    kvk = [kb_sc[plane] for plane in range(2)]
    kvv = [vb_sc[plane] for plane in range(2)]

    def tile_b(i):
        rows = pl.ds(pl.multiple_of(i * Q_TILE, Q_TILE), Q_TILE)
        pending = {}

        def head(hh):
            j, e = divmod(hh, 2)
            pair = slice(j * 128, (j + 1) * 128)
            plane = 0 if j == e else 1

            def store(o):
                if e == 0:
                    pending[j] = o
                else:
                    ycat_ref[rows, GW + j * 128:GW + (j + 1) * 128] = jnp.where(lo_t, pending[j], o)

            return _attend_stages(lambda: [_dot_nt(qb_sc[e, rows, pair], kvk[plane])],
                                  [kvv[plane]], store)

        return _interleave(*[head(hh) for hh in range(4)])

    n_loc = NAT_WR * GRID_W
    qcol = lax.broadcasted_iota(jnp.int32, (GRID_W, n_loc), 0)
    kcol = lax.broadcasted_iota(jnp.int32, (GRID_W, n_loc), 1) & (GRID_W - 1)
    cstart = jnp.clip(qcol - NAT_WC // 2, 0, GRID_W - NAT_WC)
    in_win = (kcol >= cstart) & (kcol < cstart + NAT_WC)

    def row_block(r):
        rs = jnp.clip(r - NAT_WR // 2, 0, GRID_R - NAT_WR)
        didx = rs - r + (NAT_WR - 1)
        qrows = pl.ds(pl.multiple_of(r * GRID_W, GRID_W), GRID_W)
        krows = pl.ds(pl.multiple_of(rs * GRID_W, GRID_W), n_loc)

        pending = {}

        def head(hh):
            j, e = divmod(hh, 2)
            pair = slice(j * 128, (j + 1) * 128)

            def scores():
                qh = qc_sc[e, qrows, pair]
                s_loc = _dot_nt(qh, kc_sc[krows, pair])
                s_loc = jnp.where(in_win, s_loc + bias_ref[hh, didx], NEG_INF)
                return [s_loc, _dot_nt(qh, kc_sc[T:T + PAST, pair])]

            def store(o):
                if e == 0:
                    pending[j] = o
                else:
                    ycat_ref[qrows, 2 * GW + j * 128:2 * GW + (j + 1) * 128] = jnp.where(
                        lo_r, pending[j], o)

            return _attend_stages(scores, [vc_sc[krows, pair], vc_sc[T:T + PAST, pair]], store)

        return _interleave(*[head(hh) for hh in range(4)])

    rows_per_tile = Q_TILE // GRID_W

    def fused_tile(i):
        def b_then_d():
            yield from tile_b(i)
            yield from _interleave(*[tile(i) for tile in ret_tiles])

        def c_rows():
            for rb in range(rows_per_tile):
                yield from row_block(i * rows_per_tile + rb)

        _run(_interleave(b_then_d(), c_rows()))

    _loop(T // Q_TILE, fused_tile)

    y = _dot(ycat_ref[...].astype(bf16), wout_ref[...])
    o_ref[...] = x_ref[...] + gt_ref[...] * y


def _layer_specs(l, mod_row, grid_rank=1):
    const = dict(pipeline_mode=pl.Buffered(1))

    def lspec(shape):
        nd = len(shape)
        return pl.BlockSpec((None,) + shape, lambda i: (l,) + (0,) * nd, **const)

    def mod_spec(j):
        return pl.BlockSpec((None, None, None, 1, D), lambda i: (l, j, mod_row(i), 0, 0))

    return [
        mod_spec(3), mod_spec(4), mod_spec(5),
        pl.BlockSpec((None, None, 1, D), lambda i: (l, 1, 0, 0)),
        lspec((D, IN_COLS)), lspec((D, D)),
        lspec((4, GW)), lspec((1, GW)), lspec((GW, 4 * GW)), lspec((1, 4 * GW)), lspec((2, GW)),
        lspec((1, GW)), lspec((1, 128)), lspec((1, GW)), lspec((1, GW)),
        lspec((8, 1, 1)), lspec((1, GW)),
    ]


def _ctx_mixer(x, mods, norm_g, lw, l, prev):
    T = T_CTX

    def layer_spec(*tail):
        return pl.BlockSpec((None, None) + tail, lambda i: (i, l) + (0,) * len(tail))

    tails = [(T, 128), (T, 128), (T, GW), (T, GW), (2, GW), (8 * HD, HD)]
    operands = [x, mods, mods, mods, norm_g, lw["w_in"], lw["w_out"],
                lw["conv_w"], lw["conv_b"], lw["wg"], lw["bg"], lw["lam"],
                lw["qnb"], lw["knb"], lw["qnc"], lw["knc"], lw["dec"], lw["gn"]]
    assert len(operands) == N_CTX_IN
    aliases = {0: 0}
    prev_specs = []
    if prev is not None:
        prev_specs = [pl.BlockSpec(memory_space=pl.ANY)] * len(prev)
        aliases.update({N_CTX_IN + k: 1 + k for k in range(len(prev))})
        operands = operands + list(prev)

    outs = pl.pallas_call(
        _ctx_mixer_kernel,
        grid=(N_CTX,),
        in_specs=[pl.BlockSpec((T, D), lambda i: (i, 0))] + _layer_specs(l, lambda i: 0)
        + prev_specs,
        out_specs=[pl.BlockSpec((T, D), lambda i: (i, 0))] + [layer_spec(*t) for t in tails],
        out_shape=[jax.ShapeDtypeStruct((M_ALL, D), f32)]
        + [jax.ShapeDtypeStruct((N_CTX, DEPTH) + t, f32) for t in tails],
        scratch_shapes=[
            pltpu.VMEM((T, D), bf16), pltpu.VMEM((T, D), f32),
            pltpu.VMEM((T, GW), bf16), pltpu.VMEM((T, GW), bf16), pltpu.VMEM((T, GW), bf16),
            pltpu.VMEM((T, GW), f32), pltpu.VMEM((T, GW), f32),
            pltpu.VMEM((4, Q_TILE, Q_TILE), f32),
        ],
        input_output_aliases=aliases,
        compiler_params=pltpu.CompilerParams(
            dimension_semantics=("arbitrary",), vmem_limit_bytes=VMEM_LIMIT),
        name="ctx_mixer",
    )(*operands)
    return outs[0], tuple(outs[1:])


def _lat_mixer(x, mods, norm_g, lw, l, cos, sin, caches):
    T = T_LAT
    blk0 = M_CTX // T
    const = dict(pipeline_mode=pl.Buffered(1))
    cbk, cbv, cck, ccv, lru0, ret0 = caches

    def cache_spec(w):
        return pl.BlockSpec((None, None, PAST, w), lambda i: (i, l, 0, 0))

    return pl.pallas_call(
        _lat_mixer_kernel,
        grid=(N_LAT,),
        in_specs=[pl.BlockSpec((T, D), lambda i: (blk0 + i, 0), **const)]
        + _layer_specs(l, lambda i: 1 + i)
        + [
            pl.BlockSpec((T, 128), lambda i: (0, 0), **const),
            pl.BlockSpec((T, 128), lambda i: (0, 0), **const),
            pl.BlockSpec((None, 4, NAT_WR, GRID_W, NAT_WR * GRID_W),
                         lambda i: (l, 0, 0, 0, 0), **const),
            cache_spec(128), cache_spec(128), cache_spec(GW), cache_spec(GW),
            pl.BlockSpec((None, None, 2, GW), lambda i: (i, l, 0, 0)),
            pl.BlockSpec((None, None, 8 * HD, HD), lambda i: (i, l, 0, 0)),
        ],
        out_specs=pl.BlockSpec((T, D), lambda i: (blk0 + i, 0), **const),
        out_shape=jax.ShapeDtypeStruct((M_ALL, D), f32),
        scratch_shapes=[
            pltpu.VMEM((T, D), bf16),
            pltpu.VMEM((T, GW), bf16), pltpu.VMEM((T, GW), bf16), pltpu.VMEM((T, GW), bf16),
            pltpu.VMEM((T, GW), f32), pltpu.VMEM((T, GW), f32),
            pltpu.VMEM((4 * (2 * (T // Q_TILE) - 1), Q_TILE, Q_TILE), f32),
            pltpu.VMEM((2, T, GW), bf16),
            pltpu.VMEM((2, T + PAST, 128), bf16), pltpu.VMEM((2, T + PAST, 128), bf16),
            pltpu.VMEM((2, T, GW), bf16),
            pltpu.VMEM((T + PAST, GW), bf16), pltpu.VMEM((T + PAST, GW), bf16),
        ],
        input_output_aliases={0: 0},
        compiler_params=pltpu.CompilerParams(
            dimension_semantics=("arbitrary",), vmem_limit_bytes=VMEM_LIMIT),
        name="lat_mixer",
    )(x, mods, mods, mods, norm_g, lw["w_in"], lw["w_out"],
      lw["conv_w"], lw["conv_b"], lw["wg"], lw["bg"], lw["lam"],
      lw["qnb"], lw["knb"], lw["qnc"], lw["knc"], lw["dec"], lw["gn"],
      cos, sin, lw["nat_bias"], cbk, cbv, cck, ccv, lru0, ret0)


def _block_diag(w):
    eye = jnp.eye(4, dtype=w.dtype)
    return jnp.einsum("lncd,nm->lncmd", w, eye).reshape(DEPTH, GW, GW)


def _rope_tables():
    t = jnp.arange(T_LAT)
    row = (t // GRID_W).astype(f32)
    col = (t % GRID_W).astype(f32)
    n_freq = HD // 4
    inv = ROPE_BASE ** (-jnp.arange(n_freq, dtype=f32) / n_freq)
    ang = jnp.concatenate([row[:, None] * inv, col[:, None] * inv], axis=-1)
    cos = jnp.repeat(jnp.cos(ang), 2, axis=-1)
    sin = jnp.repeat(jnp.sin(ang), 2, axis=-1)
    sign = jnp.where(jnp.arange(HD) % 2 == 0, -1.0, 1.0).astype(f32)
    return jnp.tile(cos, (1, 2)), jnp.tile(sin * sign, (1, 2))


def _nat_bias_windows(nat_bias):
    pad = GRID_W - NAT_WC
    ext = jnp.concatenate([jnp.repeat(nat_bias[..., :1], pad, -1), nat_bias,
                           jnp.repeat(nat_bias[..., -1:], pad + 1, -1)], axis=-1)
    t = jnp.tile(ext, (1, 1, 1, GRID_W))[..., :GRID_W * 127]
    t = t.reshape(DEPTH, 4, 2 * NAT_WR - 1, GRID_W, 127)[..., GRID_W - 1:]
    tab = jnp.stack([t[:, :, d:d + NAT_WR] for d in range(NAT_WR)], axis=2)
    tab = tab.transpose(0, 1, 2, 4, 3, 5)
    return tab.reshape(DEPTH, 4, NAT_WR, GRID_W, NAT_WR * GRID_W)


def kernel(x_prompt, x_sample, cache_b_k, cache_b_v, cache_c_k, cache_c_v, state_lru, state_ret, c, c_ctx, w_mod, b_mod, norm_g, ffn_w_in, ffn_w_out, w_in, w_out, conv_w, conv_b, lru_w_r, lru_b_r, lru_w_i, lru_b_i, lru_lambda, gqa_qn, gqa_kn, nat_qn, nat_kn, nat_bias, ret_decay, ret_gn):
    cvec = jnp.concatenate([c_ctx[None, :], c, jnp.zeros((8 - 1 - N_LAT, D), f32)], axis=0)
    mods = _modulation(cvec, w_mod, b_mod).reshape(DEPTH, N_MOD, 8, 1, D)
    ng = norm_g.reshape(DEPTH, 3, 1, D)

    wi = ffn_w_in.astype(bf16)
    wo = ffn_w_out.astype(bf16)
    lw = dict(
        w_in=w_in.astype(bf16),
        w_out=w_out.astype(bf16),
        conv_w=conv_w,
        conv_b=conv_b.reshape(DEPTH, 1, GW),
        wg=jnp.concatenate([_block_diag(lru_w_r[:, 0]), _block_diag(lru_w_i[:, 0]),
                            _block_diag(lru_w_r[:, 1]), _block_diag(lru_w_i[:, 1])],
                           axis=-1).astype(bf16),
        bg=jnp.concatenate([lru_b_r[:, 0], lru_b_i[:, 0], lru_b_r[:, 1], lru_b_i[:, 1]],
                           axis=-1).reshape(DEPTH, 1, 4 * GW),
        lam=lru_lambda,
        qnb=jnp.tile(gqa_qn, (1, 4)).reshape(DEPTH, 1, GW),
        knb=jnp.tile(gqa_kn, (1, 2)).reshape(DEPTH, 1, 128),
        qnc=jnp.tile(nat_qn, (1, 4)).reshape(DEPTH, 1, GW),
        knc=jnp.tile(nat_kn, (1, 4)).reshape(DEPTH, 1, GW),
        dec=ret_decay.reshape(DEPTH, 8, 1, 1),
        gn=ret_gn.reshape(DEPTH, 1, GW),
        nat_bias=_nat_bias_windows(nat_bias),
    )
    cos, sin = _rope_tables()
    caches = (cache_b_k.reshape(N_LAT, DEPTH, PAST, 128), cache_b_v.reshape(N_LAT, DEPTH, PAST, 128),
              cache_c_k.reshape(N_LAT, DEPTH, PAST, GW), cache_c_v.reshape(N_LAT, DEPTH, PAST, GW),
              state_lru, state_ret.reshape(N_LAT, DEPTH, 8 * HD, HD))

    x = (x_prompt.reshape(M_CTX, D), x_sample.reshape(M_LAT, D))
    new = None
    for l in range(DEPTH):
        x = _ffn(x, mods, ng, wi, wo, l, 0)
        x, new = _ctx_mixer(x, mods, ng, lw, l, new)
        x = _lat_mixer(x, mods, ng, lw, l, cos, sin, caches)
        x = _ffn(x, mods, ng, wi, wo, l, 1, split_out=(l == DEPTH - 1))

    new_bk, new_bv, new_ck, new_cv, new_lru, new_ret = new
    return (x[0].reshape(N_CTX, T_CTX, D), x[1].reshape(N_LAT, T_LAT, D),
            new_bk.reshape(N_CTX, DEPTH, T_CTX, 2, HD), new_bv.reshape(N_CTX, DEPTH, T_CTX, 2, HD),
            new_ck.reshape(N_CTX, DEPTH, T_CTX, 4, HD), new_cv.reshape(N_CTX, DEPTH, T_CTX, 4, HD),
            new_lru, new_ret.reshape(N_CTX, DEPTH, 2, 4, HD, HD))
```

```python
import functools
import math

import jax
import jax.numpy as jnp
from jax import lax
from jax.experimental import pallas as pl
from jax.experimental.pallas import tpu as pltpu

f32 = jnp.float32
bf16 = jnp.bfloat16

D = 1024
DEPTH = 4
N_CTX, T_CTX = 32, 256
N_LAT, T_LAT = 4, 1024
M_CTX = N_CTX * T_CTX
M_LAT = N_LAT * T_LAT
M_ALL = M_CTX + M_LAT
PAST = 256
GRID_W = 64
GRID_R = T_LAT // GRID_W
HD = 64
GW = 256
LRU_C = 8.0
NAT_WR, NAT_WC = 8, 16
D_FF = 2816
IN_COLS = 2816
EPS = 1e-6
NEG_INF = -1e30
N_MOD = 9
ATT_SCALE = HD ** -0.5
ROPE_BASE = 10000.0

C_XA, C_GA = 0, 256
C_QB, C_KB, C_VB = 512, 768, 896
C_QC, C_KC, C_VC = 1024, 1280, 1536
C_QD, C_KD, C_VD, C_GD = 1792, 2048, 2304, 2560

FFN_TM = 1024
FFN_SUB = 512
FFN_CHUNKS = ((0, 768), (768, 1536), (1536, 2304), (2304, 2816))
W_IN_CHUNK = 256
W_OUT_CHUNK = 128
Q_TILE = 256
LRU_STAGES_PER_ROUND = 4
RET_PREP_STAGES = 4
VMEM_LIMIT = 58 * 1024 * 1024


def _dot(a, b):
    return jnp.dot(a, b, preferred_element_type=f32)


def _dot_nt(a, b):
    return lax.dot_general(a, b, (((1,), (1,)), ((), ())), preferred_element_type=f32)


def _dot_tn(a, b):
    return lax.dot_general(a, b, (((0,), (0,)), ((), ())), preferred_element_type=f32)


def _softplus(z):
    return jnp.maximum(z, 0.0) + jnp.log1p(jnp.exp(-jnp.abs(z)))


def _sigmoid(z):
    return 1.0 / (1.0 + jnp.exp(-z))


def _silu(z):
    return z * _sigmoid(z)


def _gelu_tanh(z):
    c = math.sqrt(2.0 / math.pi)
    return z * (0.5 * (1.0 + jnp.tanh(c * (z + 0.044715 * (z * z * z)))))


def _rms_mod(x, ng, sc, sh):
    ms = jnp.mean(x * x, axis=-1, keepdims=True)
    y = x * lax.rsqrt(ms + EPS) * ng
    return y * (1.0 + sc) + sh


def _headnorm(x, gain):
    T, W = x.shape
    lane = lax.broadcasted_iota(jnp.int32, (T, 128), 1)
    lo = lane < HD
    cols = []
    for c in range(W // 128):
        xc = x[:, c * 128:(c + 1) * 128]
        x2 = xc * xc
        s_lo = jnp.sum(jnp.where(lo, x2, 0.0), axis=-1, keepdims=True)
        s_hi = jnp.sum(jnp.where(lo, 0.0, x2), axis=-1, keepdims=True)
        r_lo = lax.rsqrt(s_lo * (1.0 / HD) + EPS)
        r_hi = lax.rsqrt(s_hi * (1.0 / HD) + EPS)
        cols.append(xc * jnp.where(lo, r_lo, r_hi))
    y = cols[0] if len(cols) == 1 else jnp.concatenate(cols, axis=-1)
    return y * gain


def _keep_half(x, e):
    T, W = x.shape
    lo = lax.broadcasted_iota(jnp.int32, (T, 128), 1) < HD
    keep = lo if e == 0 else jnp.logical_not(lo)
    cols = [jnp.where(keep, x[:, c * 128:(c + 1) * 128], 0.0) for c in range(W // 128)]
    return cols[0] if len(cols) == 1 else jnp.concatenate(cols, axis=-1)


def _rope(x, cos, sin_signed):
    T, W = x.shape
    lane = lax.broadcasted_iota(jnp.int32, (T, 128), 1)
    even = (lane & 1) == 0
    cols = []
    for c in range(W // 128):
        xc = x[:, c * 128:(c + 1) * 128]
        sw = jnp.where(even, pltpu.roll(xc, 127, 1), pltpu.roll(xc, 1, 1))
        cols.append(xc * cos + sw * sin_signed)
    return cols[0] if len(cols) == 1 else jnp.concatenate(cols, axis=-1)


def _attend_stages(scores, v_list, store):
    s_list = scores()
    yield
    m = s_list[0].max(axis=-1, keepdims=True)
    for s in s_list[1:]:
        m = jnp.maximum(m, s.max(axis=-1, keepdims=True))
    yield
    p_list = [jnp.exp(s - m) for s in s_list]
    l = None
    for p in p_list:
        ls = jnp.sum(p, axis=-1, keepdims=True)
        l = ls if l is None else l + ls
    yield
    o = None
    for p, v in zip(p_list, v_list):
        os_ = _dot(p.astype(bf16), v)
        o = os_ if o is None else o + os_
    store(o * (1.0 / l))
    yield


def _interleave(*stages):
    live = list(stages)
    while live:
        for g in list(live):
            try:
                next(g)
            except StopIteration:
                live.remove(g)
        yield


def _every(stages, n):
    while True:
        for _ in range(n):
            try:
                next(stages)
            except StopIteration:
                return
        yield


def _delayed(stages, n):
    for _ in range(n):
        yield
    yield from stages


def _run(stages):
    for _ in stages:
        pass


def _loop(n, body):
    def step(i, carry):
        body(i)
        return carry
    lax.fori_loop(0, n, step, 0)


def _mod_kernel(c_ref, w_ref, b_ref, o_ref):
    cs = _silu(c_ref[...]).astype(bf16)
    o_ref[...] = _dot(cs, w_ref[...].astype(bf16)) + b_ref[...]


def _modulation(cvec, w_mod, b_mod):
    return pl.pallas_call(
        _mod_kernel,
        grid=(DEPTH, N_MOD),
        in_specs=[
            pl.BlockSpec((8, D), lambda l, j: (0, 0)),
            pl.BlockSpec((None, D, D), lambda l, j: (l, 0, j)),
            pl.BlockSpec((None, None, 1, D), lambda l, j: (l, j, 0, 0)),
        ],
        out_specs=pl.BlockSpec((None, None, 8, D), lambda l, j: (l, j, 0, 0)),
        out_shape=jax.ShapeDtypeStruct((DEPTH, N_MOD, 8, D), f32),
        compiler_params=pltpu.CompilerParams(
            dimension_semantics=("parallel", "parallel"), vmem_limit_bytes=VMEM_LIMIT),
        name="adaln_mod",
    )(cvec, w_mod, b_mod.reshape(DEPTH, N_MOD, 1, D))


def _ffn_rows(load, store, sh_ref, sc_ref, gt_ref, ng_ref, wi_ref, wo_ref):
    x = load()
    h = _rms_mod(x, ng_ref[...], sc_ref[...], sh_ref[...]).astype(bf16)
    yield
    acc = None
    for c0, c1 in FFN_CHUNKS:
        a = _dot(h, wi_ref[:, c0:c1])
        b = _dot(h, wi_ref[:, D_FF + c0:D_FF + c1])
        g = (_silu(a) * b).astype(bf16)
        y = _dot(g, wo_ref[c0:c1, :])
        acc = y if acc is None else acc + y
        yield
    store(x + (0.5 * gt_ref[...]) * acc)


_FFN_PASSES = tuple(slice(r, r + FFN_SUB) for r in range(0, FFN_TM, FFN_SUB))


def _ffn_passes(load, store, *refs):
    _run(_interleave(*[
        _delayed(_ffn_rows(functools.partial(load, rows), functools.partial(store, rows), *refs), k)
        for k, rows in enumerate(_FFN_PASSES)]))


def _ffn_kernel(x_ref, sh_ref, sc_ref, gt_ref, ng_ref, wi_ref, wo_ref, o_ref):
    def store(rows, v):
        o_ref[rows] = v

    _ffn_passes(lambda rows: x_ref[rows], store, sh_ref, sc_ref, gt_ref, ng_ref, wi_ref, wo_ref)


def _ffn_split_in_kernel(xc_ref, xl_ref, sh_ref, sc_ref, gt_ref, ng_ref, wi_ref, wo_ref, o_ref):
    is_ctx = pl.program_id(0) < M_CTX // FFN_TM

    def store(rows, v):
        o_ref[rows] = v

    _ffn_passes(lambda rows: jnp.where(is_ctx, xc_ref[rows], xl_ref[rows]), store,
                sh_ref, sc_ref, gt_ref, ng_ref, wi_ref, wo_ref)


def _ffn_split_out_kernel(x_ref, sh_ref, sc_ref, gt_ref, ng_ref, wi_ref, wo_ref,
                          oc_ref, ol_ref):
    outs = {}

    def keep(rows, v):
        outs[rows.start] = v

    _ffn_passes(lambda rows: x_ref[rows], keep, sh_ref, sc_ref, gt_ref, ng_ref, wi_ref, wo_ref)
    is_ctx = pl.program_id(0) < M_CTX // FFN_TM

    @pl.when(is_ctx)
    def _():
        for rows in _FFN_PASSES:
            oc_ref[rows] = outs[rows.start]

    @pl.when(jnp.logical_not(is_ctx))
    def _():
        for rows in _FFN_PASSES:
            ol_ref[rows] = outs[rows.start]


def _stream_ffn_weights(wi_hbm, wo_hbm, wi_bf, wo_bf, st_in, st_out, sem):
    jobs = []
    for c in range(2 * D_FF // W_IN_CHUNK):
        cols = pl.ds(c * W_IN_CHUNK, W_IN_CHUNK)
        jobs.append((wi_hbm.at[:, cols], st_in, wi_bf.at[:, cols]))
    for c in range(D_FF // W_OUT_CHUNK):
        rows = pl.ds(c * W_OUT_CHUNK, W_OUT_CHUNK)
        jobs.append((wo_hbm.at[rows, :], st_out, wo_bf.at[rows, :]))

    def copy(k):
        src, stage, _ = jobs[k]
        return pltpu.make_async_copy(src, stage.at[k % 2], sem.at[k % 2])

    copy(0).start()
    for k, (_, stage, dst) in enumerate(jobs):
        if k + 1 < len(jobs):
            copy(k + 1).start()
        copy(k).wait()
        dst[...] = stage[k % 2].astype(bf16)


def _with_streamed_weights(body, l, f, n_x):
    n_in = n_x + 4

    def kernel(*refs):
        wi_hbm, wo_hbm = refs[n_in], refs[n_in + 1]
        wi_bf, wo_bf, st_in, st_out, sem = refs[-5:]

        @pl.when(pl.program_id(0) == 0)
        def _():
            _stream_ffn_weights(wi_hbm.at[l, f], wo_hbm.at[l, f], wi_bf, wo_bf, st_in, st_out, sem)

        body(*refs[:n_in], wi_bf, wo_bf, *refs[n_in + 2:-5])

    return kernel


def _mod_row_of_tile(i, rows_per_tile):
    n_ctx = M_CTX // rows_per_tile
    per_b = T_LAT // rows_per_tile
    return jnp.where(i < n_ctx, 0, 1 + jnp.maximum(i - n_ctx, 0) // per_b)


def _ffn(xs, mods, norm_g, wi, wo, l, f, split_out=False):
    j0 = 6 * f
    n_ctx = M_CTX // FFN_TM
    row = functools.partial(_mod_row_of_tile, rows_per_tile=FFN_TM)

    def mod_spec(j):
        return pl.BlockSpec((None, None, None, 1, D), lambda i: (l, j, row(i), 0, 0))

    ctx_rows = pl.BlockSpec((FFN_TM, D), lambda i: (jnp.minimum(i, n_ctx - 1), 0))
    lat_rows = pl.BlockSpec((FFN_TM, D), lambda i: (jnp.maximum(i - n_ctx, 0), 0))
    all_rows = pl.BlockSpec((FFN_TM, D), lambda i: (i, 0))
    split_in = isinstance(xs, tuple)
    assert not (split_in and split_out)
    if split_in:
        one = dict(pipeline_mode=pl.Buffered(1))
        x_specs = [pl.BlockSpec(s.block_shape, s.index_map, **one) for s in (ctx_rows, lat_rows)]
        body, xs = _ffn_split_in_kernel, list(xs)
    else:
        body, x_specs, xs = (_ffn_split_out_kernel if split_out else _ffn_kernel), [all_rows], [xs]
    if split_out:
        out_specs = [ctx_rows, lat_rows]
        out_shape = [jax.ShapeDtypeStruct((M_CTX, D), f32), jax.ShapeDtypeStruct((M_LAT, D), f32)]
    else:
        out_specs = all_rows
        out_shape = jax.ShapeDtypeStruct((M_ALL, D), f32)

    return pl.pallas_call(
        _with_streamed_weights(body, l, f, len(xs)),
        grid=(M_ALL // FFN_TM,),
        in_specs=x_specs + [
            mod_spec(j0), mod_spec(j0 + 1), mod_spec(j0 + 2),
            pl.BlockSpec((None, None, 1, D), lambda i: (l, 2 * f, 0, 0)),
            pl.BlockSpec(memory_space=pl.ANY), pl.BlockSpec(memory_space=pl.ANY),
        ],
        out_specs=out_specs,
        out_shape=out_shape,
        scratch_shapes=[
            pltpu.VMEM((D, 2 * D_FF), bf16), pltpu.VMEM((D_FF, D), bf16),
            pltpu.VMEM((2, D, W_IN_CHUNK), f32), pltpu.VMEM((2, W_OUT_CHUNK, D), f32),
            pltpu.SemaphoreType.DMA((2,)),
        ],
        compiler_params=pltpu.CompilerParams(
            dimension_semantics=("arbitrary",), vmem_limit_bytes=VMEM_LIMIT),
        name=f"ffn{f}",
    )(*xs, mods, mods, mods, norm_g, wi, wo)


def _scan8(a, b, h_init, reverse, emit):
    T, C = a.shape
    nb = T // 8
    row8 = lax.broadcasted_iota(jnp.int32, (8, C), 0)
    for s in (1, 2, 4):
        if reverse:
            a_r = pltpu.roll(a, T - s, 0)
            b_r = pltpu.roll(b, T - s, 0)
            edge = row8 < 8 - s
            a_s = jnp.concatenate([a_r[:T - 8], jnp.where(edge, a_r[T - 8:], 1.0)], axis=0)
            b_s = jnp.concatenate([b_r[:T - 8], jnp.where(edge, b_r[T - 8:], 0.0)], axis=0)
        else:
            a_r = pltpu.roll(a, s, 0)
            b_r = pltpu.roll(b, s, 0)
            edge = row8 >= s
            a_s = jnp.concatenate([jnp.where(edge, a_r[:8], 1.0), a_r[8:]], axis=0)
            b_s = jnp.concatenate([jnp.where(edge, b_r[:8], 0.0), b_r[8:]], axis=0)
        b = a * b_s + b
        a = a * a_s
        yield
    h = jnp.broadcast_to(h_init, (8, C))
    for n, v in enumerate(range(nb - 1, -1, -1) if reverse else range(nb)):
        h = a[v * 8:(v + 1) * 8] * h + b[v * 8:(v + 1) * 8]
        emit(v, h)
        if n % 4 == 3:
            yield
    return h[0:1] if reverse else h[7:8]


def _lru_group(h_ref, win_ref, cw_ref, cb_ref, wg_ref, bg_ref, lam_ref, h0, ycat_ref, st_ref,
               gate_sc):
    T = h_ref.shape[0]
    p = _dot(h_ref[...], win_ref[:, C_XA:C_XA + 2 * GW])
    xa = p[:, :GW]
    yield
    gate_sc[...] = _gelu_tanh(p[:, GW:])
    yield
    tpos = lax.broadcasted_iota(jnp.int32, (T, GW), 0)

    def shifted(k):
        if k > 0:
            return jnp.where(tpos >= k, pltpu.roll(xa, k, 0), 0.0)
        return jnp.where(tpos < T + k, pltpu.roll(xa, T + k, 0), 0.0)

    cw = cw_ref[...]
    xc = shifted(2) * cw[0:1] + shifted(1) * cw[1:2] + xa * cw[2:3] + shifted(-1) * cw[3:4]
    xc = xc + cb_ref[...]
    xcb = xc.astype(bf16)
    lam = lam_ref[...]
    yield
    for d in range(2):
        ds_ = slice(d * 2 * GW, (d + 1) * 2 * GW)
        pre = _dot(xcb, wg_ref[:, ds_]) + bg_ref[:, ds_]
        yield
        r = _sigmoid(pre[:, 0:GW])
        gi = _sigmoid(pre[:, GW:2 * GW])
        yield
        log_a = (-LRU_C * r) * _softplus(-lam[d:d + 1])
        a = jnp.exp(log_a)
        u = jnp.sqrt(1.0 - a * a) * (gi * xc)
        yield
        h_init = jnp.zeros((1, GW), f32) if h0 is None else h0[d:d + 1]

        def emit_fwd(v, hv):
            ycat_ref[v * 8:(v + 1) * 8, 0:GW] = hv

        def emit_bwd(v, hv):
            rows = slice(v * 8, (v + 1) * 8)
            ycat_ref[rows, 0:GW] = (ycat_ref[rows, 0:GW] + hv) * gate_sc[rows, :]

        last = yield from _scan8(a, u, h_init, d == 1, emit_bwd if d == 1 else emit_fwd)
        if st_ref is not None:
            st_ref[d:d + 1, :] = last
        yield


def _build_decay_tiles(dec_ref, dm_sc, nt):
    r = lax.broadcasted_iota(jnp.int32, (Q_TILE, Q_TILE), 0)
    c = lax.broadcasted_iota(jnp.int32, (Q_TILE, Q_TILE), 1)
    base = (r - c).astype(f32)
    for h in range(4):
        lgf = -_softplus(-dec_ref[h])
        lgb = -_softplus(-dec_ref[4 + h])
        for m in range(-(nt - 1), nt):
            diff = base + float(m * Q_TILE)
            if m > 0:
                mat = jnp.exp(lgf * diff)
            elif m < 0:
                mat = jnp.exp(lgb * (-diff))
            else:
                mat = (jnp.where(diff >= 0, jnp.exp(lgf * jnp.maximum(diff, 0.0)), 0.0)
                       + jnp.where(diff <= 0, jnp.exp(lgb * jnp.maximum(-diff, 0.0)), 0.0))
            dm_sc[h * (2 * nt - 1) + m + nt - 1] = mat


def _retention_group(h_ref, win_ref, dec_ref, gn_ref, s0_ref, ycat_ref, sret_ref,
                     q_sc, k_sc, v_sc, g_sc, dm_sc, tiles_out=None):
    T = h_ref.shape[0]
    nt = T // Q_TILE
    q_sc[...] = _dot(h_ref[...], win_ref[:, C_QD:C_QD + GW]).astype(bf16)
    yield
    kf = _dot(h_ref[...], win_ref[:, C_KD:C_KD + GW]) * ATT_SCALE
    k_sc[...] = kf.astype(bf16)
    yield
    v_sc[...] = _dot(h_ref[...], win_ref[:, C_VD:C_VD + GW]).astype(bf16)
    yield
    g_sc[...] = _dot(h_ref[...], win_ref[:, C_GD:C_GD + GW])
    gn = gn_ref[...]
    tcol = lax.broadcasted_iota(jnp.int32, (T, 1), 0).astype(f32)
    yield

    def head(h):
        hs = slice(h * HD, (h + 1) * HD)
        lgf = -_softplus(-dec_ref[h])
        lgb = -_softplus(-dec_ref[4 + h])
        kh = k_sc[:, hs]
        vh = v_sc[:, hs]
        if sret_ref is not None:
            zf = jnp.exp(lgf * ((T - 1.0) - tcol))
            zb = jnp.exp(lgb * tcol)
            kff = kf[:, hs]
            sret_ref[h * HD:(h + 1) * HD, :] = _dot_tn((kff * zf).astype(bf16), vh)
            sret_ref[(4 + h) * HD:(5 + h) * HD, :] = _dot_tn((kff * zb).astype(bf16), vh)
        if s0_ref is not None:
            s0f = s0_ref[h * HD:(h + 1) * HD, :].astype(bf16)
            s0b = s0_ref[(4 + h) * HD:(5 + h) * HD, :].astype(bf16)
        gnh = gn[:, hs]

        def tile(i):
            t0 = pl.multiple_of(i * Q_TILE, Q_TILE)
            rows = pl.ds(t0, Q_TILE)
            qh = q_sc[rows, hs]
            o = None
            for j in range(nt):
                ks = slice(j * Q_TILE, (j + 1) * Q_TILE)
                dm = dm_sc[h * (2 * nt - 1) + (nt - 1 - j) + i]
                sd = (_dot_nt(qh, kh[ks]) * dm).astype(bf16)
                yield
                oj = _dot(sd, vh[ks])
                o = oj if o is None else o + oj
                yield
            if s0_ref is not None:
                tq = (lax.broadcasted_iota(jnp.int32, (Q_TILE, 1), 0) + t0).astype(f32)
                o = o + _dot(qh, s0f) * jnp.exp(lgf * (tq + 1.0))
                o = o + _dot(qh, s0b) * jnp.exp(lgb * (T - tq))
            mu = jnp.mean(o, axis=-1, keepdims=True)
            oc = o - mu
            var = jnp.mean(oc * oc, axis=-1, keepdims=True)
            yield
            on = oc * lax.rsqrt(var + EPS) * gnh
            ycat_ref[rows, 3 * GW + h * HD:3 * GW + (h + 1) * HD] = on * _silu(g_sc[rows, hs])
            yield

        return tile

    if nt == 1:
        for h in range(4):
            tile = head(h)
            yield
            yield from tile(0)
    else:
        tiles_out.extend(head(h) for h in range(4))


N_CTX_IN = 18


def _ctx_mixer_kernel(*refs):
    n_out = 7
    n_scratch = 8
    ins = refs[:N_CTX_IN]
    outs = refs[len(refs) - n_out - n_scratch:]
    _ctx_mixer_body(*ins, *outs)


def _ctx_mixer_body(x_ref, sh_ref, sc_ref, gt_ref, ng_ref, win_ref, wout_ref,
                    cw_ref, cb_ref, wg_ref, bg_ref, lam_ref,
                    qnb_ref, knb_ref, qnc_ref, knc_ref, dec_ref, gn_ref,
                    o_ref, kb_ref, vb_ref, kc_ref, vc_ref, st_ref, sret_ref,
                    h_ref, ycat_ref, q_sc, k_sc, v_sc, g_sc, gate_sc, dm_sc):
    @pl.when(pl.program_id(0) == 0)
    def _():
        _build_decay_tiles(dec_ref, dm_sc, T_CTX // Q_TILE)

    h_ref[...] = _rms_mod(x_ref[...], ng_ref[...], sc_ref[...], sh_ref[...]).astype(bf16)

    _run(_interleave(
        _lru_group(h_ref, win_ref, cw_ref, cb_ref, wg_ref, bg_ref, lam_ref, None, ycat_ref,
                   st_ref, gate_sc),
        _ctx_attention_group(h_ref, win_ref, C_QB, 2, qnb_ref, knb_ref, kb_ref, vb_ref,
                             ycat_ref, GW),
        _ctx_attention_group(h_ref, win_ref, C_QC, 4, qnc_ref, knc_ref, kc_ref, vc_ref,
                             ycat_ref, 2 * GW),
        _retention_group(h_ref, win_ref, dec_ref, gn_ref, None, ycat_ref, sret_ref,
                         q_sc, k_sc, v_sc, g_sc, dm_sc),
    ))
    y = _dot(ycat_ref[...].astype(bf16), wout_ref[...])
    o_ref[...] = x_ref[...] + gt_ref[...] * y


def _ctx_attention_group(h_ref, win_ref, col, n_kv, qn_ref, kn_ref, k_out_ref, v_out_ref,
                         ycat_ref, ycol):
    kw = n_kv * HD
    p = _dot(h_ref[...], win_ref[:, col:col + GW + 2 * kw])
    yield
    qf = _headnorm(p[:, 0:GW], qn_ref[...] * ATT_SCALE)
    yield
    kn = _headnorm(p[:, GW:GW + kw], kn_ref[...])
    vf = p[:, GW + kw:GW + 2 * kw]
    k_out_ref[...] = kn
    v_out_ref[...] = vf
    k = kn.astype(bf16)
    v = vf.astype(bf16)
    if n_kv == 2:
        k_sw = pltpu.roll(kn, HD, 1).astype(bf16)
        v_sw = pltpu.roll(vf, HD, 1).astype(bf16)
    yield
    lo = lax.broadcasted_iota(jnp.int32, (p.shape[0], 128), 1) < HD
    pending = {}

    def head(hh):
        j, e = divmod(hh, 2)
        pair = slice(j * 128, (j + 1) * 128)
        if n_kv == 4:
            kk, vv = k[:, pair], v[:, pair]
        else:
            kk, vv = (k, v) if j == e else (k_sw, v_sw)

        def scores():
            qm = jnp.where(lo if e == 0 else jnp.logical_not(lo), qf[:, pair], 0.0).astype(bf16)
            return [_dot_nt(qm, kk)]

        def store(o):
            if e == 0:
                pending[j] = o
            else:
                ycat_ref[:, ycol + j * 128:ycol + (j + 1) * 128] = jnp.where(lo, pending[j], o)

        return _attend_stages(scores, [vv], store)

    yield from _interleave(*[head(hh) for hh in range(4)])


def _lat_mixer_kernel(x_ref, sh_ref, sc_ref, gt_ref, ng_ref, win_ref, wout_ref,
                      cw_ref, cb_ref, wg_ref, bg_ref, lam_ref,
                      qnb_ref, knb_ref, qnc_ref, knc_ref, dec_ref, gn_ref,
                      cos_ref, sin_ref, bias_ref,
                      cbk_ref, cbv_ref, cck_ref, ccv_ref, lru0_ref, ret0_ref,
                      o_ref,
                      h_ref, q_sc, k_sc, v_sc, g_sc, gate_sc, dm_sc,
                      qb_sc, kb_sc, vb_sc, qc_sc, kc_sc, vc_sc):
    T = T_LAT
    ycat_ref = o_ref

    @pl.when(pl.program_id(0) == 0)
    def _():
        _build_decay_tiles(dec_ref, dm_sc, T // Q_TILE)

    h_ref[...] = _rms_mod(x_ref[...], ng_ref[...], sc_ref[...], sh_ref[...]).astype(bf16)

    def prep_b():
        cos = cos_ref[...]
        sin = sin_ref[...]
        p = _dot(h_ref[...], win_ref[:, C_QB:C_QB + GW])
        yield
        qr = _rope(_headnorm(p, qnb_ref[...] * ATT_SCALE), cos, sin)
        for e in range(2):
            qb_sc[e] = _keep_half(qr, e).astype(bf16)
        yield
        p = _dot(h_ref[...], win_ref[:, C_KB:C_KB + GW])
        yield
        for dst, lat, ctx in ((kb_sc, _rope(_headnorm(p[:, 0:128], knb_ref[...]), cos, sin),
                               cbk_ref[...]),
                              (vb_sc, p[:, 128:GW], cbv_ref[...])):
            dst[0, 0:T, :] = lat.astype(bf16)
            dst[0, T:T + PAST, :] = ctx.astype(bf16)
            dst[1, 0:T, :] = pltpu.roll(lat, HD, 1).astype(bf16)
            dst[1, T:T + PAST, :] = pltpu.roll(ctx, HD, 1).astype(bf16)
        yield

    def prep_c():
        p = _dot(h_ref[...], win_ref[:, C_QC:C_QC + GW])
        yield
        qn = _headnorm(p, qnc_ref[...] * ATT_SCALE)
        for e in range(2):
            qc_sc[e] = _keep_half(qn, e).astype(bf16)
        yield
        p = _dot(h_ref[...], win_ref[:, C_KC:C_KC + GW])
        yield
        kc_sc[0:T, :] = _headnorm(p, knc_ref[...]).astype(bf16)
        kc_sc[T:T + PAST, :] = cck_ref[...].astype(bf16)
        yield
        vc_sc[0:T, :] = _dot(h_ref[...], win_ref[:, C_VC:C_VC + GW]).astype(bf16)
        vc_sc[T:T + PAST, :] = ccv_ref[...].astype(bf16)
        yield

    ret_tiles = []
    retention = _retention_group(h_ref, win_ref, dec_ref, gn_ref, ret0_ref, ycat_ref, None,
                                 q_sc, k_sc, v_sc, g_sc, dm_sc, ret_tiles)

    def preps():
        yield from prep_b()
        yield from prep_c()
        for _ in range(RET_PREP_STAGES):
            next(retention)
            yield

    _run(_interleave(
        _every(_lru_group(h_ref, win_ref, cw_ref, cb_ref, wg_ref, bg_ref, lam_ref, lru0_ref[...],
                          ycat_ref, None, gate_sc), LRU_STAGES_PER_ROUND),
        preps()))
    _run(retention)

    lo_t = lax.broadcasted_iota(jnp.int32, (Q_TILE, 128), 1) < HD
    lo_r = lax.broadcasted_iota(jnp.int32, (GRID_W, 128), 1) < HD
    kvk = [kb_sc[plane] for plane in range(2)]
    kvv = [vb_sc[plane] for plane in range(2)]

    def tile_b(i):
        rows = pl.ds(pl.multiple_of(i * Q_TILE, Q_TILE), Q_TILE)
        pending = {}

        def head(hh):
            j, e = divmod(hh, 2)
            pair = slice(j * 128, (j + 1) * 128)
            plane = 0 if j == e else 1

            def store(o):
                if e == 0:
                    pending[j] = o
                else:
                    ycat_ref[rows, GW + j * 128:GW + (j + 1) * 128] = jnp.where(lo_t, pending[j], o)

            return _attend_stages(lambda: [_dot_nt(qb_sc[e, rows, pair], kvk[plane])],
                                  [kvv[plane]], store)

        return _interleave(*[head(hh) for hh in range(4)])

    n_loc = NAT_WR * GRID_W
    qcol = lax.broadcasted_iota(jnp.int32, (GRID_W, n_loc), 0)
    kcol = lax.broadcasted_iota(jnp.int32, (GRID_W, n_loc), 1) & (GRID_W - 1)
    cstart = jnp.clip(qcol - NAT_WC // 2, 0, GRID_W - NAT_WC)
    in_win = (kcol >= cstart) & (kcol < cstart + NAT_WC)

    def row_block(r):
        rs = jnp.clip(r - NAT_WR // 2, 0, GRID_R - NAT_WR)
        didx = rs - r + (NAT_WR - 1)
        qrows = pl.ds(pl.multiple_of(r * GRID_W, GRID_W), GRID_W)
        krows = pl.ds(pl.multiple_of(rs * GRID_W, GRID_W), n_loc)

        pending = {}

        def head(hh):
            j, e = divmod(hh, 2)
            pair = slice(j * 128, (j + 1) * 128)

            def scores():
                qh = qc_sc[e, qrows, pair]
                s_loc = _dot_nt(qh, kc_sc[krows, pair])
                s_loc = jnp.where(in_win, s_loc + bias_ref[hh, didx], NEG_INF)
                return [s_loc, _dot_nt(qh, kc_sc[T:T + PAST, pair])]

            def store(o):
                if e == 0:
                    pending[j] = o
                else:
                    ycat_ref[qrows, 2 * GW + j * 128:2 * GW + (j + 1) * 128] = jnp.where(
                        lo_r, pending[j], o)

            return _attend_stages(scores, [vc_sc[krows, pair], vc_sc[T:T + PAST, pair]], store)

        return _interleave(*[head(hh) for hh in range(4)])

    rows_per_tile = Q_TILE // GRID_W

    def fused_tile(i):
        def b_then_d():
            yield from tile_b(i)
            yield from _interleave(*[tile(i) for tile in ret_tiles])

        def c_rows():
            for rb in range(rows_per_tile):
                yield from row_block(i * rows_per_tile + rb)

        _run(_interleave(b_then_d(), c_rows()))

    _loop(T // Q_TILE, fused_tile)

    y = _dot(ycat_ref[...].astype(bf16), wout_ref[...])
    o_ref[...] = x_ref[...] + gt_ref[...] * y


def _layer_specs(l, mod_row, grid_rank=1):
    const = dict(pipeline_mode=pl.Buffered(1))

    def lspec(shape):
        nd = len(shape)
        return pl.BlockSpec((None,) + shape, lambda i: (l,) + (0,) * nd, **const)

    def mod_spec(j):
        return pl.BlockSpec((None, None, None, 1, D), lambda i: (l, j, mod_row(i), 0, 0))

    return [
        mod_spec(3), mod_spec(4), mod_spec(5),
        pl.BlockSpec((None, None, 1, D), lambda i: (l, 1, 0, 0)),
        lspec((D, IN_COLS)), lspec((D, D)),
        lspec((4, GW)), lspec((1, GW)), lspec((GW, 4 * GW)), lspec((1, 4 * GW)), lspec((2, GW)),
        lspec((1, GW)), lspec((1, 128)), lspec((1, GW)), lspec((1, GW)),
        lspec((8, 1, 1)), lspec((1, GW)),
    ]


CTX_STATE_TAILS = ((T_CTX, 128), (T_CTX, 128), (T_CTX, GW), (T_CTX, GW), (2, GW), (8 * HD, HD))


def _ctx_mixer(x, mods, norm_g, lw, l, prev):
    T = T_CTX

    def layer_spec(*tail):
        return pl.BlockSpec((None, None) + tail, lambda i: (i, l) + (0,) * len(tail))

    tails = CTX_STATE_TAILS
    operands = [x, mods, mods, mods, norm_g, lw["w_in"], lw["w_out"],
                lw["conv_w"], lw["conv_b"], lw["wg"], lw["bg"], lw["lam"],
                lw["qnb"], lw["knb"], lw["qnc"], lw["knc"], lw["dec"], lw["gn"]]
    assert len(operands) == N_CTX_IN
    aliases = {0: 0}
    prev_specs = [pl.BlockSpec(memory_space=pl.ANY)] * len(prev)
    aliases.update({N_CTX_IN + k: 1 + k for k in range(len(prev))})
    operands = operands + list(prev)

    outs = pl.pallas_call(
        _ctx_mixer_kernel,
        grid=(N_CTX,),
        in_specs=[pl.BlockSpec((T, D), lambda i: (i, 0))] + _layer_specs(l, lambda i: 0)
        + prev_specs,
        out_specs=[pl.BlockSpec((T, D), lambda i: (i, 0))] + [layer_spec(*t) for t in tails],
        out_shape=[jax.ShapeDtypeStruct((M_ALL, D), f32)]
        + [jax.ShapeDtypeStruct((N_CTX, DEPTH) + t, f32) for t in tails],
        scratch_shapes=[
            pltpu.VMEM((T, D), bf16), pltpu.VMEM((T, D), f32),
            pltpu.VMEM((T, GW), bf16), pltpu.VMEM((T, GW), bf16), pltpu.VMEM((T, GW), bf16),
            pltpu.VMEM((T, GW), f32), pltpu.VMEM((T, GW), f32),
            pltpu.VMEM((4, Q_TILE, Q_TILE), f32),
        ],
        input_output_aliases=aliases,
        compiler_params=pltpu.CompilerParams(
            dimension_semantics=("arbitrary",), vmem_limit_bytes=VMEM_LIMIT),
        name="ctx_mixer",
    )(*operands)
    return outs[0], tuple(outs[1:])


def _lat_mixer(x, mods, norm_g, lw, l, cos, sin, caches):
    T = T_LAT
    blk0 = M_CTX // T
    const = dict(pipeline_mode=pl.Buffered(1))
    cbk, cbv, cck, ccv, lru0, ret0 = caches

    def cache_spec(w):
        return pl.BlockSpec((None, None, PAST, w), lambda i: (i, l, 0, 0))

    return pl.pallas_call(
        _lat_mixer_kernel,
        grid=(N_LAT,),
        in_specs=[pl.BlockSpec((T, D), lambda i: (blk0 + i, 0), **const)]
        + _layer_specs(l, lambda i: 1 + i)
        + [
            pl.BlockSpec((T, 128), lambda i: (0, 0), **const),
            pl.BlockSpec((T, 128), lambda i: (0, 0), **const),
            pl.BlockSpec((None, 4, NAT_WR, GRID_W, NAT_WR * GRID_W),
                         lambda i: (l, 0, 0, 0, 0), **const),
            cache_spec(128), cache_spec(128), cache_spec(GW), cache_spec(GW),
            pl.BlockSpec((None, None, 2, GW), lambda i: (i, l, 0, 0)),
            pl.BlockSpec((None, None, 8 * HD, HD), lambda i: (i, l, 0, 0)),
        ],
        out_specs=pl.BlockSpec((T, D), lambda i: (blk0 + i, 0), **const),
        out_shape=jax.ShapeDtypeStruct((M_ALL, D), f32),
        scratch_shapes=[
            pltpu.VMEM((T, D), bf16),
            pltpu.VMEM((T, GW), bf16), pltpu.VMEM((T, GW), bf16), pltpu.VMEM((T, GW), bf16),
            pltpu.VMEM((T, GW), f32), pltpu.VMEM((T, GW), f32),
            pltpu.VMEM((4 * (2 * (T // Q_TILE) - 1), Q_TILE, Q_TILE), f32),
            pltpu.VMEM((2, T, GW), bf16),
            pltpu.VMEM((2, T + PAST, 128), bf16), pltpu.VMEM((2, T + PAST, 128), bf16),
            pltpu.VMEM((2, T, GW), bf16),
            pltpu.VMEM((T + PAST, GW), bf16), pltpu.VMEM((T + PAST, GW), bf16),
        ],
        input_output_aliases={0: 0},
        compiler_params=pltpu.CompilerParams(
            dimension_semantics=("arbitrary",), vmem_limit_bytes=VMEM_LIMIT),
        name="lat_mixer",
    )(x, mods, mods, mods, norm_g, lw["w_in"], lw["w_out"],
      lw["conv_w"], lw["conv_b"], lw["wg"], lw["bg"], lw["lam"],
      lw["qnb"], lw["knb"], lw["qnc"], lw["knc"], lw["dec"], lw["gn"],
      cos, sin, lw["nat_bias"], cbk, cbv, cck, ccv, lru0, ret0)


def _block_diag(w):
    eye = jnp.eye(4, dtype=w.dtype)
    return jnp.einsum("lncd,nm->lncmd", w, eye).reshape(DEPTH, GW, GW)


def _rope_tables():
    t = jnp.arange(T_LAT)
    row = (t // GRID_W).astype(f32)
    col = (t % GRID_W).astype(f32)
    n_freq = HD // 4
    inv = ROPE_BASE ** (-jnp.arange(n_freq, dtype=f32) / n_freq)
    ang = jnp.concatenate([row[:, None] * inv, col[:, None] * inv], axis=-1)
    cos = jnp.repeat(jnp.cos(ang), 2, axis=-1)
    sin = jnp.repeat(jnp.sin(ang), 2, axis=-1)
    sign = jnp.where(jnp.arange(HD) % 2 == 0, -1.0, 1.0).astype(f32)
    return jnp.tile(cos, (1, 2)), jnp.tile(sin * sign, (1, 2))


def _nat_bias_windows(nat_bias):
    pad = GRID_W - NAT_WC
    ext = jnp.concatenate([jnp.repeat(nat_bias[..., :1], pad, -1), nat_bias,
                           jnp.repeat(nat_bias[..., -1:], pad + 1, -1)], axis=-1)
    t = jnp.tile(ext, (1, 1, 1, GRID_W))[..., :GRID_W * 127]
    t = t.reshape(DEPTH, 4, 2 * NAT_WR - 1, GRID_W, 127)[..., GRID_W - 1:]
    tab = jnp.stack([t[:, :, d:d + NAT_WR] for d in range(NAT_WR)], axis=2)
    tab = tab.transpose(0, 1, 2, 4, 3, 5)
    return tab.reshape(DEPTH, 4, NAT_WR, GRID_W, NAT_WR * GRID_W)


def kernel(x_prompt, x_sample, cache_b_k, cache_b_v, cache_c_k, cache_c_v, state_lru, state_ret, c, c_ctx, w_mod, b_mod, norm_g, ffn_w_in, ffn_w_out, w_in, w_out, conv_w, conv_b, lru_w_r, lru_b_r, lru_w_i, lru_b_i, lru_lambda, gqa_qn, gqa_kn, nat_qn, nat_kn, nat_bias, ret_decay, ret_gn):
    cvec = jnp.concatenate([c_ctx[None, :], c, jnp.zeros((8 - 1 - N_LAT, D), f32)], axis=0)
    mods = _modulation(cvec, w_mod, b_mod).reshape(DEPTH, N_MOD, 8, 1, D)
    ng = norm_g.reshape(DEPTH, 3, 1, D)

    wi, wo = ffn_w_in, ffn_w_out
    lw = dict(
        w_in=w_in.astype(bf16),
        w_out=w_out.astype(bf16),
        conv_w=conv_w,
        conv_b=conv_b.reshape(DEPTH, 1, GW),
        wg=jnp.concatenate([_block_diag(lru_w_r[:, 0]), _block_diag(lru_w_i[:, 0]),
                            _block_diag(lru_w_r[:, 1]), _block_diag(lru_w_i[:, 1])],
                           axis=-1).astype(bf16),
        bg=jnp.concatenate([lru_b_r[:, 0], lru_b_i[:, 0], lru_b_r[:, 1], lru_b_i[:, 1]],
                           axis=-1).reshape(DEPTH, 1, 4 * GW),
        lam=lru_lambda,
        qnb=jnp.tile(gqa_qn, (1, 4)).reshape(DEPTH, 1, GW),
        knb=jnp.tile(gqa_kn, (1, 2)).reshape(DEPTH, 1, 128),
        qnc=jnp.tile(nat_qn, (1, 4)).reshape(DEPTH, 1, GW),
        knc=jnp.tile(nat_kn, (1, 4)).reshape(DEPTH, 1, GW),
        dec=ret_decay.reshape(DEPTH, 8, 1, 1),
        gn=ret_gn.reshape(DEPTH, 1, GW),
        nat_bias=_nat_bias_windows(nat_bias),
    )
    cos, sin = _rope_tables()
    caches = (cache_b_k.reshape(N_LAT, DEPTH, PAST, 128), cache_b_v.reshape(N_LAT, DEPTH, PAST, 128),
              cache_c_k.reshape(N_LAT, DEPTH, PAST, GW), cache_c_v.reshape(N_LAT, DEPTH, PAST, GW),
              state_lru, state_ret.reshape(N_LAT, DEPTH, 8 * HD, HD))

    x = (x_prompt.reshape(M_CTX, D), x_sample.reshape(M_LAT, D))
    new = tuple(jnp.zeros((N_CTX, DEPTH) + t, f32) for t in CTX_STATE_TAILS)
    for l in range(DEPTH):
        x = _ffn(x, mods, ng, wi, wo, l, 0)
        x, new = _ctx_mixer(x, mods, ng, lw, l, new)
        x = _lat_mixer(x, mods, ng, lw, l, cos, sin, caches)
        x = _ffn(x, mods, ng, wi, wo, l, 1, split_out=(l == DEPTH - 1))

    new_bk, new_bv, new_ck, new_cv, new_lru, new_ret = new
    return (x[0].reshape(N_CTX, T_CTX, D), x[1].reshape(N_LAT, T_LAT, D),
            new_bk.reshape(N_CTX, DEPTH, T_CTX, 2, HD), new_bv.reshape(N_CTX, DEPTH, T_CTX, 2, HD),
            new_ck.reshape(N_CTX, DEPTH, T_CTX, 4, HD), new_cv.reshape(N_CTX, DEPTH, T_CTX, 4, HD),
            new_lru, new_ret.reshape(N_CTX, DEPTH, 2, 4, HD, HD))
```

```python
import functools
import math

import jax
import jax.numpy as jnp
from jax import lax
from jax.experimental import pallas as pl
from jax.experimental.pallas import tpu as pltpu

f32 = jnp.float32
bf16 = jnp.bfloat16

D = 1024
DEPTH = 4
N_CTX, T_CTX = 32, 256
N_LAT, T_LAT = 4, 1024
M_CTX = N_CTX * T_CTX
M_LAT = N_LAT * T_LAT
M_ALL = M_CTX + M_LAT
PAST = 256
GRID_W = 64
GRID_R = T_LAT // GRID_W
HD = 64
GW = 256
LRU_C = 8.0
NAT_WR, NAT_WC = 8, 16
D_FF = 2816
IN_COLS = 2816
EPS = 1e-6
NEG_INF = -1e30
N_MOD = 9
ATT_SCALE = HD ** -0.5
ROPE_BASE = 10000.0

C_XA, C_GA = 0, 256
C_QB, C_KB, C_VB = 512, 768, 896
C_QC, C_KC, C_VC = 1024, 1280, 1536
C_QD, C_KD, C_VD, C_GD = 1792, 2048, 2304, 2560

FFN_TM = 1024
FFN_SUB = 512
FFN_CHUNKS = ((0, 768), (768, 1536), (1536, 2304), (2304, 2816))
Q_TILE = 256
LRU_STAGES_PER_ROUND = 3
RET_PREP_STAGES = 4
VMEM_LIMIT = 58 * 1024 * 1024


def _dot(a, b):
    return jnp.dot(a, b, preferred_element_type=f32)


def _dot_nt(a, b):
    return lax.dot_general(a, b, (((1,), (1,)), ((), ())), preferred_element_type=f32)


def _dot_tn(a, b):
    return lax.dot_general(a, b, (((0,), (0,)), ((), ())), preferred_element_type=f32)


def _softplus(z):
    return jnp.maximum(z, 0.0) + jnp.log1p(jnp.exp(-jnp.abs(z)))


def _sigmoid(z):
    return 1.0 / (1.0 + jnp.exp(-z))


def _silu(z):
    return z * _sigmoid(z)


def _gelu_tanh(z):
    c = math.sqrt(2.0 / math.pi)
    return z * (0.5 * (1.0 + jnp.tanh(c * (z + 0.044715 * (z * z * z)))))


def _rms_mod(x, ng, sc, sh):
    ms = jnp.mean(x * x, axis=-1, keepdims=True)
    y = x * lax.rsqrt(ms + EPS) * ng
    return y * (1.0 + sc) + sh


def _headnorm(x, gain):
    T, W = x.shape
    lane = lax.broadcasted_iota(jnp.int32, (T, 128), 1)
    lo = lane < HD
    cols = []
    for c in range(W // 128):
        xc = x[:, c * 128:(c + 1) * 128]
        x2 = xc * xc
        s_lo = jnp.sum(jnp.where(lo, x2, 0.0), axis=-1, keepdims=True)
        s_hi = jnp.sum(jnp.where(lo, 0.0, x2), axis=-1, keepdims=True)
        r_lo = lax.rsqrt(s_lo * (1.0 / HD) + EPS)
        r_hi = lax.rsqrt(s_hi * (1.0 / HD) + EPS)
        cols.append(xc * jnp.where(lo, r_lo, r_hi))
    y = cols[0] if len(cols) == 1 else jnp.concatenate(cols, axis=-1)
    return y * gain


def _keep_half(x, e):
    T, W = x.shape
    lo = lax.broadcasted_iota(jnp.int32, (T, 128), 1) < HD
    keep = lo if e == 0 else jnp.logical_not(lo)
    cols = [jnp.where(keep, x[:, c * 128:(c + 1) * 128], 0.0) for c in range(W // 128)]
    return cols[0] if len(cols) == 1 else jnp.concatenate(cols, axis=-1)


def _rope(x, cos, sin_signed):
    T, W = x.shape
    lane = lax.broadcasted_iota(jnp.int32, (T, 128), 1)
    even = (lane & 1) == 0
    cols = []
    for c in range(W // 128):
        xc = x[:, c * 128:(c + 1) * 128]
        sw = jnp.where(even, pltpu.roll(xc, 127, 1), pltpu.roll(xc, 1, 1))
        cols.append(xc * cos + sw * sin_signed)
    return cols[0] if len(cols) == 1 else jnp.concatenate(cols, axis=-1)


def _attend_stages(scores, v_list, store):
    s_list = scores()
    yield
    m = s_list[0].max(axis=-1, keepdims=True)
    for s in s_list[1:]:
        m = jnp.maximum(m, s.max(axis=-1, keepdims=True))
    yield
    p_list = [jnp.exp(s - m) for s in s_list]
    l = None
    for p in p_list:
        ls = jnp.sum(p, axis=-1, keepdims=True)
        l = ls if l is None else l + ls
    yield
    o = None
    for p, v in zip(p_list, v_list):
        os_ = _dot(p.astype(bf16), v)
        o = os_ if o is None else o + os_
    store(o * (1.0 / l))
    yield


def _interleave(*stages):
    live = list(stages)
    while live:
        for g in list(live):
            try:
                next(g)
            except StopIteration:
                live.remove(g)
        yield


def _every(stages, n):
    while True:
        for _ in range(n):
            try:
                next(stages)
            except StopIteration:
                return
        yield


def _delayed(stages, n):
    for _ in range(n):
        yield
    yield from stages


def _run(stages):
    for _ in stages:
        pass


def _loop(n, body):
    def step(i, carry):
        body(i)
        return carry
    lax.fori_loop(0, n, step, 0)


def _mod_kernel(c_ref, w_ref, b_ref, o_ref):
    cs = _silu(c_ref[...]).astype(bf16)
    o_ref[...] = _dot(cs, w_ref[...].astype(bf16)) + b_ref[...]


def _modulation(cvec, w_mod, b_mod):
    return pl.pallas_call(
        _mod_kernel,
        grid=(DEPTH, N_MOD),
        in_specs=[
            pl.BlockSpec((8, D), lambda l, j: (0, 0)),
            pl.BlockSpec((None, D, D), lambda l, j: (l, 0, j)),
            pl.BlockSpec((None, None, 1, D), lambda l, j: (l, j, 0, 0)),
        ],
        out_specs=pl.BlockSpec((None, None, 8, D), lambda l, j: (l, j, 0, 0)),
        out_shape=jax.ShapeDtypeStruct((DEPTH, N_MOD, 8, D), f32),
        compiler_params=pltpu.CompilerParams(
            dimension_semantics=("parallel", "parallel"), vmem_limit_bytes=VMEM_LIMIT),
        name="adaln_mod",
    )(cvec, w_mod, b_mod.reshape(DEPTH, N_MOD, 1, D))


def _ffn_rows(load, store, sh_ref, sc_ref, gt_ref, ng_ref, wi_ref, wo_ref):
    x = load()
    h = _rms_mod(x, ng_ref[...], sc_ref[...], sh_ref[...]).astype(bf16)
    yield
    acc = None
    for c0, c1 in FFN_CHUNKS:
        a = _dot(h, wi_ref[:, c0:c1])
        b = _dot(h, wi_ref[:, D_FF + c0:D_FF + c1])
        g = (_silu(a) * b).astype(bf16)
        y = _dot(g, wo_ref[c0:c1, :])
        acc = y if acc is None else acc + y
        yield
    store(x + (0.5 * gt_ref[...]) * acc)


_FFN_PASSES = tuple(slice(r, r + FFN_SUB) for r in range(0, FFN_TM, FFN_SUB))


def _ffn_passes(load, store, *refs):
    _run(_interleave(*[
        _delayed(_ffn_rows(functools.partial(load, rows), functools.partial(store, rows), *refs), k)
        for k, rows in enumerate(_FFN_PASSES)]))


def _ffn_kernel(x_ref, sh_ref, sc_ref, gt_ref, ng_ref, wi_ref, wo_ref, o_ref):
    def store(rows, v):
        o_ref[rows] = v

    _ffn_passes(lambda rows: x_ref[rows], store, sh_ref, sc_ref, gt_ref, ng_ref, wi_ref, wo_ref)


def _ffn_split_in_kernel(xc_ref, xl_ref, sh_ref, sc_ref, gt_ref, ng_ref, wi_ref, wo_ref, o_ref):
    is_ctx = pl.program_id(0) < M_CTX // FFN_TM

    def store(rows, v):
        o_ref[rows] = v

    _ffn_passes(lambda rows: jnp.where(is_ctx, xc_ref[rows], xl_ref[rows]), store,
                sh_ref, sc_ref, gt_ref, ng_ref, wi_ref, wo_ref)


def _ffn_split_out_kernel(x_ref, sh_ref, sc_ref, gt_ref, ng_ref, wi_ref, wo_ref,
                          oc_ref, ol_ref):
    outs = {}

    def keep(rows, v):
        outs[rows.start] = v

    _ffn_passes(lambda rows: x_ref[rows], keep, sh_ref, sc_ref, gt_ref, ng_ref, wi_ref, wo_ref)
    is_ctx = pl.program_id(0) < M_CTX // FFN_TM

    @pl.when(is_ctx)
    def _():
        for rows in _FFN_PASSES:
            oc_ref[rows] = outs[rows.start]

    @pl.when(jnp.logical_not(is_ctx))
    def _():
        for rows in _FFN_PASSES:
            ol_ref[rows] = outs[rows.start]


def _mod_row_of_tile(i, rows_per_tile):
    n_ctx = M_CTX // rows_per_tile
    per_b = T_LAT // rows_per_tile
    return jnp.where(i < n_ctx, 0, 1 + jnp.maximum(i - n_ctx, 0) // per_b)


def _ffn(xs, mods, norm_g, wi, wo, l, f, split_out=False):
    j0 = 6 * f
    n_ctx = M_CTX // FFN_TM
    row = functools.partial(_mod_row_of_tile, rows_per_tile=FFN_TM)

    def mod_spec(j):
        return pl.BlockSpec((None, None, None, 1, D), lambda i: (l, j, row(i), 0, 0))

    ctx_rows = pl.BlockSpec((FFN_TM, D), lambda i: (jnp.minimum(i, n_ctx - 1), 0))
    lat_rows = pl.BlockSpec((FFN_TM, D), lambda i: (jnp.maximum(i - n_ctx, 0), 0))
    all_rows = pl.BlockSpec((FFN_TM, D), lambda i: (i, 0))
    split_in = isinstance(xs, tuple)
    assert not (split_in and split_out)
    if split_in:
        body, x_specs, xs = _ffn_split_in_kernel, [ctx_rows, lat_rows], list(xs)
    else:
        body, x_specs, xs = (_ffn_split_out_kernel if split_out else _ffn_kernel), [all_rows], [xs]
    if split_out:
        out_specs = [ctx_rows, lat_rows]
        out_shape = [jax.ShapeDtypeStruct((M_CTX, D), f32), jax.ShapeDtypeStruct((M_LAT, D), f32)]
    else:
        out_specs = all_rows
        out_shape = jax.ShapeDtypeStruct((M_ALL, D), f32)

    const = dict(pipeline_mode=pl.Buffered(1))
    return pl.pallas_call(
        body,
        grid=(M_ALL // FFN_TM,),
        in_specs=x_specs + [
            mod_spec(j0), mod_spec(j0 + 1), mod_spec(j0 + 2),
            pl.BlockSpec((None, None, 1, D), lambda i: (l, 2 * f, 0, 0)),
            pl.BlockSpec((None, None, D, 2 * D_FF), lambda i: (l, f, 0, 0), **const),
            pl.BlockSpec((None, None, D_FF, D), lambda i: (l, f, 0, 0), **const),
        ],
        out_specs=out_specs,
        out_shape=out_shape,
        compiler_params=pltpu.CompilerParams(
            dimension_semantics=("arbitrary",), vmem_limit_bytes=VMEM_LIMIT),
        name=f"ffn{f}",
    )(*xs, mods, mods, mods, norm_g, wi, wo)


def _scan8(a, b, h_init, reverse, emit):
    T, C = a.shape
    nb = T // 8
    row8 = lax.broadcasted_iota(jnp.int32, (8, C), 0)
    for s in (1, 2, 4):
        if reverse:
            a_r = pltpu.roll(a, T - s, 0)
            b_r = pltpu.roll(b, T - s, 0)
            edge = row8 < 8 - s
            a_s = jnp.concatenate([a_r[:T - 8], jnp.where(edge, a_r[T - 8:], 1.0)], axis=0)
            b_s = jnp.concatenate([b_r[:T - 8], jnp.where(edge, b_r[T - 8:], 0.0)], axis=0)
        else:
            a_r = pltpu.roll(a, s, 0)
            b_r = pltpu.roll(b, s, 0)
            edge = row8 >= s
            a_s = jnp.concatenate([jnp.where(edge, a_r[:8], 1.0), a_r[8:]], axis=0)
            b_s = jnp.concatenate([jnp.where(edge, b_r[:8], 0.0), b_r[8:]], axis=0)
        b = a * b_s + b
        a = a * a_s
        yield
    h = jnp.broadcast_to(h_init, (8, C))
    for n, v in enumerate(range(nb - 1, -1, -1) if reverse else range(nb)):
        h = a[v * 8:(v + 1) * 8] * h + b[v * 8:(v + 1) * 8]
        emit(v, h)
        if n % 4 == 3:
            yield
    return h[0:1] if reverse else h[7:8]


def _lru_group(h_ref, win_ref, cw_ref, cb_ref, wg_ref, bg_ref, lam_ref, h0, ycat_ref, st_ref,
               gate_sc):
    T = h_ref.shape[0]
    p = _dot(h_ref[...], win_ref[:, C_XA:C_XA + 2 * GW])
    xa = p[:, :GW]
    yield
    gate_sc[...] = _gelu_tanh(p[:, GW:])
    yield
    tpos = lax.broadcasted_iota(jnp.int32, (T, GW), 0)

    def shifted(k):
        if k > 0:
            return jnp.where(tpos >= k, pltpu.roll(xa, k, 0), 0.0)
        return jnp.where(tpos < T + k, pltpu.roll(xa, T + k, 0), 0.0)

    cw = cw_ref[...]
    xc = shifted(2) * cw[0:1] + shifted(1) * cw[1:2] + xa * cw[2:3] + shifted(-1) * cw[3:4]
    xc = xc + cb_ref[...]
    xcb = xc.astype(bf16)
    lam = lam_ref[...]
    yield
    for d in range(2):
        ds_ = slice(d * 2 * GW, (d + 1) * 2 * GW)
        pre = _dot(xcb, wg_ref[:, ds_]) + bg_ref[:, ds_]
        yield
        r = _sigmoid(pre[:, 0:GW])
        gi = _sigmoid(pre[:, GW:2 * GW])
        yield
        log_a = (-LRU_C * r) * _softplus(-lam[d:d + 1])
        a = jnp.exp(log_a)
        u = jnp.sqrt(1.0 - a * a) * (gi * xc)
        yield
        h_init = jnp.zeros((1, GW), f32) if h0 is None else h0[d:d + 1]

        def emit_fwd(v, hv):
            ycat_ref[v * 8:(v + 1) * 8, 0:GW] = hv

        def emit_bwd(v, hv):
            rows = slice(v * 8, (v + 1) * 8)
            ycat_ref[rows, 0:GW] = (ycat_ref[rows, 0:GW] + hv) * gate_sc[rows, :]

        last = yield from _scan8(a, u, h_init, d == 1, emit_bwd if d == 1 else emit_fwd)
        if st_ref is not None:
            st_ref[d:d + 1, :] = last
        yield


def _build_decay_tiles(dec_ref, dm_sc, nt):
    r = lax.broadcasted_iota(jnp.int32, (Q_TILE, Q_TILE), 0)
    c = lax.broadcasted_iota(jnp.int32, (Q_TILE, Q_TILE), 1)
    base = (r - c).astype(f32)
    for h in range(4):
        lgf = -_softplus(-dec_ref[h])
        lgb = -_softplus(-dec_ref[4 + h])
        for m in range(-(nt - 1), nt):
            diff = base + float(m * Q_TILE)
            if m > 0:
                mat = jnp.exp(lgf * diff)
            elif m < 0:
                mat = jnp.exp(lgb * (-diff))
            else:
                mat = (jnp.where(diff >= 0, jnp.exp(lgf * jnp.maximum(diff, 0.0)), 0.0)
                       + jnp.where(diff <= 0, jnp.exp(lgb * jnp.maximum(-diff, 0.0)), 0.0))
            dm_sc[h * (2 * nt - 1) + m + nt - 1] = mat


def _retention_group(h_ref, win_ref, dec_ref, gn_ref, s0_ref, ycat_ref, sret_ref,
                     q_sc, k_sc, v_sc, g_sc, dm_sc, tiles_out=None):
    T = h_ref.shape[0]
    nt = T // Q_TILE
    q_sc[...] = _dot(h_ref[...], win_ref[:, C_QD:C_QD + GW]).astype(bf16)
    yield
    kf = _dot(h_ref[...], win_ref[:, C_KD:C_KD + GW]) * ATT_SCALE
    k_sc[...] = kf.astype(bf16)
    yield
    v_sc[...] = _dot(h_ref[...], win_ref[:, C_VD:C_VD + GW]).astype(bf16)
    yield
    g_sc[...] = _dot(h_ref[...], win_ref[:, C_GD:C_GD + GW])
    gn = gn_ref[...]
    tcol = lax.broadcasted_iota(jnp.int32, (T, 1), 0).astype(f32)
    yield

    def head(h):
        hs = slice(h * HD, (h + 1) * HD)
        lgf = -_softplus(-dec_ref[h])
        lgb = -_softplus(-dec_ref[4 + h])
        kh = k_sc[:, hs]
        vh = v_sc[:, hs]
        if sret_ref is not None:
            zf = jnp.exp(lgf * ((T - 1.0) - tcol))
            zb = jnp.exp(lgb * tcol)
            kff = kf[:, hs]
            sret_ref[h * HD:(h + 1) * HD, :] = _dot_tn((kff * zf).astype(bf16), vh)
            sret_ref[(4 + h) * HD:(5 + h) * HD, :] = _dot_tn((kff * zb).astype(bf16), vh)
        if s0_ref is not None:
            s0f = s0_ref[h * HD:(h + 1) * HD, :].astype(bf16)
            s0b = s0_ref[(4 + h) * HD:(5 + h) * HD, :].astype(bf16)
        gnh = gn[:, hs]

        def tile(i):
            t0 = pl.multiple_of(i * Q_TILE, Q_TILE)
            rows = pl.ds(t0, Q_TILE)
            qh = q_sc[rows, hs]
            o = None
            for j in range(nt):
                ks = slice(j * Q_TILE, (j + 1) * Q_TILE)
                dm = dm_sc[h * (2 * nt - 1) + (nt - 1 - j) + i]
                sd = (_dot_nt(qh, kh[ks]) * dm).astype(bf16)
                yield
                oj = _dot(sd, vh[ks])
                o = oj if o is None else o + oj
                yield
            if s0_ref is not None:
                tq = (lax.broadcasted_iota(jnp.int32, (Q_TILE, 1), 0) + t0).astype(f32)
                o = o + _dot(qh, s0f) * jnp.exp(lgf * (tq + 1.0))
                o = o + _dot(qh, s0b) * jnp.exp(lgb * (T - tq))
            mu = jnp.mean(o, axis=-1, keepdims=True)
            oc = o - mu
            var = jnp.mean(oc * oc, axis=-1, keepdims=True)
            yield
            on = oc * lax.rsqrt(var + EPS) * gnh
            ycat_ref[rows, 3 * GW + h * HD:3 * GW + (h + 1) * HD] = on * _silu(g_sc[rows, hs])
            yield

        return tile

    if nt == 1:
        for h in range(4):
            tile = head(h)
            yield
            yield from tile(0)
    else:
        tiles_out.extend(head(h) for h in range(4))


N_CTX_IN = 18


def _ctx_mixer_kernel(*refs):
    n_out = 7
    n_scratch = 8
    ins = refs[:N_CTX_IN]
    outs = refs[len(refs) - n_out - n_scratch:]
    _ctx_mixer_body(*ins, *outs)


def _ctx_mixer_body(x_ref, sh_ref, sc_ref, gt_ref, ng_ref, win_ref, wout_ref,
                    cw_ref, cb_ref, wg_ref, bg_ref, lam_ref,
                    qnb_ref, knb_ref, qnc_ref, knc_ref, dec_ref, gn_ref,
                    o_ref, kb_ref, vb_ref, kc_ref, vc_ref, st_ref, sret_ref,
                    h_ref, ycat_ref, q_sc, k_sc, v_sc, g_sc, gate_sc, dm_sc):
    @pl.when(pl.program_id(0) == 0)
    def _():
        _build_decay_tiles(dec_ref, dm_sc, T_CTX // Q_TILE)

    h_ref[...] = _rms_mod(x_ref[...], ng_ref[...], sc_ref[...], sh_ref[...]).astype(bf16)

    _run(_interleave(
        _lru_group(h_ref, win_ref, cw_ref, cb_ref, wg_ref, bg_ref, lam_ref, None, ycat_ref,
                   st_ref, gate_sc),
        _ctx_attention_group(h_ref, win_ref, C_QB, 2, qnb_ref, knb_ref, kb_ref, vb_ref,
                             ycat_ref, GW),
        _ctx_attention_group(h_ref, win_ref, C_QC, 4, qnc_ref, knc_ref, kc_ref, vc_ref,
                             ycat_ref, 2 * GW),
        _retention_group(h_ref, win_ref, dec_ref, gn_ref, None, ycat_ref, sret_ref,
                         q_sc, k_sc, v_sc, g_sc, dm_sc),
    ))
    y = _dot(ycat_ref[...].astype(bf16), wout_ref[...])
    o_ref[...] = x_ref[...] + gt_ref[...] * y


def _ctx_attention_group(h_ref, win_ref, col, n_kv, qn_ref, kn_ref, k_out_ref, v_out_ref,
                         ycat_ref, ycol):
    kw = n_kv * HD
    p = _dot(h_ref[...], win_ref[:, col:col + GW + 2 * kw])
    yield
    qf = _headnorm(p[:, 0:GW], qn_ref[...] * ATT_SCALE)
    yield
    kn = _headnorm(p[:, GW:GW + kw], kn_ref[...])
    vf = p[:, GW + kw:GW + 2 * kw]
    k_out_ref[...] = kn
    v_out_ref[...] = vf
    k = kn.astype(bf16)
    v = vf.astype(bf16)
    if n_kv == 2:
        k_sw = pltpu.roll(kn, HD, 1).astype(bf16)
        v_sw = pltpu.roll(vf, HD, 1).astype(bf16)
    yield
    lo = lax.broadcasted_iota(jnp.int32, (p.shape[0], 128), 1) < HD
    pending = {}

    def head(hh):
        j, e = divmod(hh, 2)
        pair = slice(j * 128, (j + 1) * 128)
        if n_kv == 4:
            kk, vv = k[:, pair], v[:, pair]
        else:
            kk, vv = (k, v) if j == e else (k_sw, v_sw)

        def scores():
            qm = jnp.where(lo if e == 0 else jnp.logical_not(lo), qf[:, pair], 0.0).astype(bf16)
            return [_dot_nt(qm, kk)]

        def store(o):
            if e == 0:
                pending[j] = o
            else:
                ycat_ref[:, ycol + j * 128:ycol + (j + 1) * 128] = jnp.where(lo, pending[j], o)

        return _attend_stages(scores, [vv], store)

    yield from _interleave(*[head(hh) for hh in range(4)])


def _lat_mixer_kernel(x_ref, sh_ref, sc_ref, gt_ref, ng_ref, win_ref, wout_ref,
                      cw_ref, cb_ref, wg_ref, bg_ref, lam_ref,
                      qnb_ref, knb_ref, qnc_ref, knc_ref, dec_ref, gn_ref,
                      cos_ref, sin_ref, bias_ref,
                      cbk_ref, cbv_ref, cck_ref, ccv_ref, lru0_ref, ret0_ref,
                      o_ref,
                      h_ref, q_sc, k_sc, v_sc, g_sc, gate_sc, dm_sc,
                      qb_sc, kb_sc, vb_sc, qc_sc, kc_sc, vc_sc):
    T = T_LAT
    ycat_ref = o_ref

    @pl.when(pl.program_id(0) == 0)
    def _():
        _build_decay_tiles(dec_ref, dm_sc, T // Q_TILE)

    h_ref[...] = _rms_mod(x_ref[...], ng_ref[...], sc_ref[...], sh_ref[...]).astype(bf16)

    def prep_b():
        cos = cos_ref[...]
        sin = sin_ref[...]
        p = _dot(h_ref[...], win_ref[:, C_QB:C_QB + GW])
        yield
        qr = _rope(_headnorm(p, qnb_ref[...] * ATT_SCALE), cos, sin)
        for e in range(2):
            qb_sc[e] = _keep_half(qr, e).astype(bf16)
        yield
        p = _dot(h_ref[...], win_ref[:, C_KB:C_KB + GW])
        yield
        for dst, lat, ctx in ((kb_sc, _rope(_headnorm(p[:, 0:128], knb_ref[...]), cos, sin),
                               cbk_ref[...]),
                              (vb_sc, p[:, 128:GW], cbv_ref[...])):
            dst[0, 0:T, :] = lat.astype(bf16)
            dst[0, T:T + PAST, :] = ctx.astype(bf16)
            dst[1, 0:T, :] = pltpu.roll(lat, HD, 1).astype(bf16)
            dst[1, T:T + PAST, :] = pltpu.roll(ctx, HD, 1).astype(bf16)
        yield

    def prep_c():
        p = _dot(h_ref[...], win_ref[:, C_QC:C_QC + GW])
        yield
        qn = _headnorm(p, qnc_ref[...] * ATT_SCALE)
        for e in range(2):
            qc_sc[e] = _keep_half(qn, e).astype(bf16)
        yield
        p = _dot(h_ref[...], win_ref[:, C_KC:C_KC + GW])
        yield
        kc_sc[0:T, :] = _headnorm(p, knc_ref[...]).astype(bf16)
        kc_sc[T:T + PAST, :] = cck_ref[...].astype(bf16)
        yield
        vc_sc[0:T, :] = _dot(h_ref[...], win_ref[:, C_VC:C_VC + GW]).astype(bf16)
        vc_sc[T:T + PAST, :] = ccv_ref[...].astype(bf16)
        yield

    ret_tiles = []
    retention = _retention_group(h_ref, win_ref, dec_ref, gn_ref, ret0_ref, ycat_ref, None,
                                 q_sc, k_sc, v_sc, g_sc, dm_sc, ret_tiles)

    def preps():
        yield from prep_b()
        yield from prep_c()
        for _ in range(RET_PREP_STAGES):
            next(retention)
            yield

    _run(_interleave(
        _every(_lru_group(h_ref, win_ref, cw_ref, cb_ref, wg_ref, bg_ref, lam_ref, lru0_ref[...],
                          ycat_ref, None, gate_sc), LRU_STAGES_PER_ROUND),
        preps()))
    _run(retention)

    lo_t = lax.broadcasted_iota(jnp.int32, (Q_TILE, 128), 1) < HD
    lo_r = lax.broadcasted_iota(jnp.int32, (GRID_W, 128), 1) < HD
    kvk = [kb_sc[plane] for plane in range(2)]
    kvv = [vb_sc[plane] for plane in range(2)]

    def tile_b(i):
        rows = pl.ds(pl.multiple_of(i * Q_TILE, Q_TILE), Q_TILE)
        pending = {}

        def head(hh):
            j, e = divmod(hh, 2)
            pair = slice(j * 128, (j + 1) * 128)
            plane = 0 if j == e else 1

            def store(o):
                if e == 0:
                    pending[j] = o
                else:
                    ycat_ref[rows, GW + j * 128:GW + (j + 1) * 128] = jnp.where(lo_t, pending[j], o)

            return _attend_stages(lambda: [_dot_nt(qb_sc[e, rows, pair], kvk[plane])],
                                  [kvv[plane]], store)

        return _interleave(*[head(hh) for hh in range(4)])

    n_loc = NAT_WR * GRID_W
    qcol = lax.broadcasted_iota(jnp.int32, (GRID_W, n_loc), 0)
    kcol = lax.broadcasted_iota(jnp.int32, (GRID_W, n_loc), 1) & (GRID_W - 1)
    cstart = jnp.clip(qcol - NAT_WC // 2, 0, GRID_W - NAT_WC)
    in_win = (kcol >= cstart) & (kcol < cstart + NAT_WC)

    def row_block(r):
        rs = jnp.clip(r - NAT_WR // 2, 0, GRID_R - NAT_WR)
        didx = rs - r + (NAT_WR - 1)
        qrows = pl.ds(pl.multiple_of(r * GRID_W, GRID_W), GRID_W)
        krows = pl.ds(pl.multiple_of(rs * GRID_W, GRID_W), n_loc)

        pending = {}

        def head(hh):
            j, e = divmod(hh, 2)
            pair = slice(j * 128, (j + 1) * 128)

            def scores():
                qh = qc_sc[e, qrows, pair]
                s_loc = _dot_nt(qh, kc_sc[krows, pair])
                s_loc = jnp.where(in_win, s_loc + bias_ref[hh, didx], NEG_INF)
                return [s_loc, _dot_nt(qh, kc_sc[T:T + PAST, pair])]

            def store(o):
                if e == 0:
                    pending[j] = o
                else:
                    ycat_ref[qrows, 2 * GW + j * 128:2 * GW + (j + 1) * 128] = jnp.where(
                        lo_r, pending[j], o)

            return _attend_stages(scores, [vc_sc[krows, pair], vc_sc[T:T + PAST, pair]], store)

        return _interleave(*[head(hh) for hh in range(4)])

    rows_per_tile = Q_TILE // GRID_W

    def fused_tile(i):
        def b_then_d():
            yield from tile_b(i)
            yield from _interleave(*[tile(i) for tile in ret_tiles])

        def c_rows():
            for rb in range(rows_per_tile):
                yield from row_block(i * rows_per_tile + rb)

        _run(_interleave(b_then_d(), c_rows()))

    _loop(T // Q_TILE, fused_tile)

    y = _dot(ycat_ref[...].astype(bf16), wout_ref[...])
    o_ref[...] = x_ref[...] + gt_ref[...] * y


def _layer_specs(l, mod_row):
    const = dict(pipeline_mode=pl.Buffered(1))

    def lspec(shape):
        nd = len(shape)
        return pl.BlockSpec((None,) + shape, lambda i: (l,) + (0,) * nd, **const)

    def mod_spec(j):
        return pl.BlockSpec((None, None, None, 1, D), lambda i: (l, j, mod_row(i), 0, 0))

    return [
        mod_spec(3), mod_spec(4), mod_spec(5),
        pl.BlockSpec((None, None, 1, D), lambda i: (l, 1, 0, 0)),
        lspec((D, IN_COLS)), lspec((D, D)),
        lspec((4, GW)), lspec((1, GW)), lspec((GW, 4 * GW)), lspec((1, 4 * GW)), lspec((2, GW)),
        lspec((1, GW)), lspec((1, 128)), lspec((1, GW)), lspec((1, GW)),
        lspec((8, 1, 1)), lspec((1, GW)),
    ]


def _ctx_mixer(x, mods, norm_g, lw, l, prev):
    T = T_CTX

    def layer_spec(*tail):
        return pl.BlockSpec((None, None) + tail, lambda i: (i, l) + (0,) * len(tail))

    tails = [(T, 128), (T, 128), (T, GW), (T, GW), (2, GW), (8 * HD, HD)]
    operands = [x, mods, mods, mods, norm_g, lw["w_in"], lw["w_out"],
                lw["conv_w"], lw["conv_b"], lw["wg"], lw["bg"], lw["lam"],
                lw["qnb"], lw["knb"], lw["qnc"], lw["knc"], lw["dec"], lw["gn"]]
    assert len(operands) == N_CTX_IN
    aliases = {0: 0}
    prev_specs = []
    if prev is not None:
        prev_specs = [pl.BlockSpec(memory_space=pl.ANY)] * len(prev)
        aliases.update({N_CTX_IN + k: 1 + k for k in range(len(prev))})
        operands = operands + list(prev)

    outs = pl.pallas_call(
        _ctx_mixer_kernel,
        grid=(N_CTX,),
        in_specs=[pl.BlockSpec((T, D), lambda i: (i, 0))] + _layer_specs(l, lambda i: 0)
        + prev_specs,
        out_specs=[pl.BlockSpec((T, D), lambda i: (i, 0))] + [layer_spec(*t) for t in tails],
        out_shape=[jax.ShapeDtypeStruct((M_ALL, D), f32)]
        + [jax.ShapeDtypeStruct((N_CTX, DEPTH) + t, f32) for t in tails],
        scratch_shapes=[
            pltpu.VMEM((T, D), bf16), pltpu.VMEM((T, D), f32),
            pltpu.VMEM((T, GW), bf16), pltpu.VMEM((T, GW), bf16), pltpu.VMEM((T, GW), bf16),
            pltpu.VMEM((T, GW), f32), pltpu.VMEM((T, GW), f32),
            pltpu.VMEM((4, Q_TILE, Q_TILE), f32),
        ],
        input_output_aliases=aliases,
        compiler_params=pltpu.CompilerParams(
            dimension_semantics=("arbitrary",), vmem_limit_bytes=VMEM_LIMIT),
        name="ctx_mixer",
    )(*operands)
    return outs[0], tuple(outs[1:])


def _lat_mixer(x, mods, norm_g, lw, l, cos, sin, caches):
    T = T_LAT
    blk0 = M_CTX // T
    const = dict(pipeline_mode=pl.Buffered(1))
    cbk, cbv, cck, ccv, lru0, ret0 = caches

    def cache_spec(w):
        return pl.BlockSpec((None, None, PAST, w), lambda i: (i, l, 0, 0))

    return pl.pallas_call(
        _lat_mixer_kernel,
        grid=(N_LAT,),
        in_specs=[pl.BlockSpec((T, D), lambda i: (blk0 + i, 0), **const)]
        + _layer_specs(l, lambda i: 1 + i)
        + [
            pl.BlockSpec((T, 128), lambda i: (0, 0), **const),
            pl.BlockSpec((T, 128), lambda i: (0, 0), **const),
            pl.BlockSpec((None, 4, NAT_WR, GRID_W, NAT_WR * GRID_W),
                         lambda i: (l, 0, 0, 0, 0), **const),
            cache_spec(128), cache_spec(128), cache_spec(GW), cache_spec(GW),
            pl.BlockSpec((None, None, 2, GW), lambda i: (i, l, 0, 0)),
            pl.BlockSpec((None, None, 8 * HD, HD), lambda i: (i, l, 0, 0)),
        ],
        out_specs=pl.BlockSpec((T, D), lambda i: (blk0 + i, 0), **const),
        out_shape=jax.ShapeDtypeStruct((M_ALL, D), f32),
        scratch_shapes=[
            pltpu.VMEM((T, D), bf16),
            pltpu.VMEM((T, GW), bf16), pltpu.VMEM((T, GW), bf16), pltpu.VMEM((T, GW), bf16),
            pltpu.VMEM((T, GW), f32), pltpu.VMEM((T, GW), f32),
            pltpu.VMEM((4 * (2 * (T // Q_TILE) - 1), Q_TILE, Q_TILE), f32),
            pltpu.VMEM((2, T, GW), bf16),
            pltpu.VMEM((2, T + PAST, 128), bf16), pltpu.VMEM((2, T + PAST, 128), bf16),
            pltpu.VMEM((2, T, GW), bf16),
            pltpu.VMEM((T + PAST, GW), bf16), pltpu.VMEM((T + PAST, GW), bf16),
        ],
        input_output_aliases={0: 0},
        compiler_params=pltpu.CompilerParams(
            dimension_semantics=("arbitrary",), vmem_limit_bytes=VMEM_LIMIT),
        name="lat_mixer",
    )(x, mods, mods, mods, norm_g, lw["w_in"], lw["w_out"],
      lw["conv_w"], lw["conv_b"], lw["wg"], lw["bg"], lw["lam"],
      lw["qnb"], lw["knb"], lw["qnc"], lw["knc"], lw["dec"], lw["gn"],
      cos, sin, lw["nat_bias"], cbk, cbv, cck, ccv, lru0, ret0)


def _block_diag(w):
    eye = jnp.eye(4, dtype=w.dtype)
    return jnp.einsum("lncd,nm->lncmd", w, eye).reshape(DEPTH, GW, GW)


def _rope_tables():
    t = jnp.arange(T_LAT)
    row = (t // GRID_W).astype(f32)
    col = (t % GRID_W).astype(f32)
    n_freq = HD // 4
    inv = ROPE_BASE ** (-jnp.arange(n_freq, dtype=f32) / n_freq)
    ang = jnp.concatenate([row[:, None] * inv, col[:, None] * inv], axis=-1)
    cos = jnp.repeat(jnp.cos(ang), 2, axis=-1)
    sin = jnp.repeat(jnp.sin(ang), 2, axis=-1)
    sign = jnp.where(jnp.arange(HD) % 2 == 0, -1.0, 1.0).astype(f32)
    return jnp.tile(cos, (1, 2)), jnp.tile(sin * sign, (1, 2))


def _nat_bias_windows(nat_bias):
    pad = GRID_W - NAT_WC
    ext = jnp.concatenate([jnp.repeat(nat_bias[..., :1], pad, -1), nat_bias,
                           jnp.repeat(nat_bias[..., -1:], pad + 1, -1)], axis=-1)
    t = jnp.tile(ext, (1, 1, 1, GRID_W))[..., :GRID_W * 127]
    t = t.reshape(DEPTH, 4, 2 * NAT_WR - 1, GRID_W, 127)[..., GRID_W - 1:]
    tab = jnp.stack([t[:, :, d:d + NAT_WR] for d in range(NAT_WR)], axis=2)
    tab = tab.transpose(0, 1, 2, 4, 3, 5)
    return tab.reshape(DEPTH, 4, NAT_WR, GRID_W, NAT_WR * GRID_W)


def kernel(x_prompt, x_sample, cache_b_k, cache_b_v, cache_c_k, cache_c_v, state_lru, state_ret, c, c_ctx, w_mod, b_mod, norm_g, ffn_w_in, ffn_w_out, w_in, w_out, conv_w, conv_b, lru_w_r, lru_b_r, lru_w_i, lru_b_i, lru_lambda, gqa_qn, gqa_kn, nat_qn, nat_kn, nat_bias, ret_decay, ret_gn):
    cvec = jnp.concatenate([c_ctx[None, :], c, jnp.zeros((8 - 1 - N_LAT, D), f32)], axis=0)
    mods = _modulation(cvec, w_mod, b_mod).reshape(DEPTH, N_MOD, 8, 1, D)
    ng = norm_g.reshape(DEPTH, 3, 1, D)

    wi = ffn_w_in.astype(bf16)
    wo = ffn_w_out.astype(bf16)
    lw = dict(
        w_in=w_in.astype(bf16),
        w_out=w_out.astype(bf16),
        conv_w=conv_w,
        conv_b=conv_b.reshape(DEPTH, 1, GW),
        wg=jnp.concatenate([_block_diag(lru_w_r[:, 0]), _block_diag(lru_w_i[:, 0]),
                            _block_diag(lru_w_r[:, 1]), _block_diag(lru_w_i[:, 1])],
                           axis=-1).astype(bf16),
        bg=jnp.concatenate([lru_b_r[:, 0], lru_b_i[:, 0], lru_b_r[:, 1], lru_b_i[:, 1]],
                           axis=-1).reshape(DEPTH, 1, 4 * GW),
        lam=lru_lambda,
        qnb=jnp.tile(gqa_qn, (1, 4)).reshape(DEPTH, 1, GW),
        knb=jnp.tile(gqa_kn, (1, 2)).reshape(DEPTH, 1, 128),
        qnc=jnp.tile(nat_qn, (1, 4)).reshape(DEPTH, 1, GW),
        knc=jnp.tile(nat_kn, (1, 4)).reshape(DEPTH, 1, GW),
        dec=ret_decay.reshape(DEPTH, 8, 1, 1),
        gn=ret_gn.reshape(DEPTH, 1, GW),
        nat_bias=_nat_bias_windows(nat_bias),
    )
    cos, sin = _rope_tables()
    caches = (cache_b_k.reshape(N_LAT, DEPTH, PAST, 128), cache_b_v.reshape(N_LAT, DEPTH, PAST, 128),
              cache_c_k.reshape(N_LAT, DEPTH, PAST, GW), cache_c_v.reshape(N_LAT, DEPTH, PAST, GW),
              state_lru, state_ret.reshape(N_LAT, DEPTH, 8 * HD, HD))

    x = (x_prompt.reshape(M_CTX, D), x_sample.reshape(M_LAT, D))
    new = None
    for l in range(DEPTH):
        x = _ffn(x, mods, ng, wi, wo, l, 0)
        x, new = _ctx_mixer(x, mods, ng, lw, l, new)
        x = _lat_mixer(x, mods, ng, lw, l, cos, sin, caches)
        x = _ffn(x, mods, ng, wi, wo, l, 1, split_out=(l == DEPTH - 1))

    new_bk, new_bv, new_ck, new_cv, new_lru, new_ret = new
    return (x[0].reshape(N_CTX, T_CTX, D), x[1].reshape(N_LAT, T_LAT, D),
            new_bk.reshape(N_CTX, DEPTH, T_CTX, 2, HD), new_bv.reshape(N_CTX, DEPTH, T_CTX, 2, HD),
            new_ck.reshape(N_CTX, DEPTH, T_CTX, 4, HD), new_cv.reshape(N_CTX, DEPTH, T_CTX, 4, HD),
            new_lru, new_ret.reshape(N_CTX, DEPTH, 2, 4, HD, HD))
```

```python
import functools
import math

import jax
import jax.numpy as jnp
from jax import lax
from jax.experimental import pallas as pl
from jax.experimental.pallas import tpu as pltpu

f32 = jnp.float32
bf16 = jnp.bfloat16

D = 1024
DEPTH = 4
N_CTX, T_CTX = 32, 256
N_LAT, T_LAT = 4, 1024
M_CTX = N_CTX * T_CTX
M_LAT = N_LAT * T_LAT
M_ALL = M_CTX + M_LAT
PAST = 256
GRID_W = 64
GRID_R = T_LAT // GRID_W
HD = 64
GW = 256
LRU_C = 8.0
NAT_WR, NAT_WC = 8, 16
D_FF = 2816
IN_COLS = 2816
EPS = 1e-6
NEG_INF = -1e30
N_MOD = 9
ATT_SCALE = HD ** -0.5
ROPE_BASE = 10000.0

C_XA, C_GA = 0, 256
C_QB, C_KB, C_VB = 512, 768, 896
C_QC, C_KC, C_VC = 1024, 1280, 1536
C_QD, C_KD, C_VD, C_GD = 1792, 2048, 2304, 2560

FFN_TM = 1024
FFN_SUB = 512
FFN_CHUNKS = ((0, 768), (768, 1536), (1536, 2304), (2304, 2816))
Q_TILE = 256
LRU_STAGES_PER_ROUND = 3
RET_PREP_STAGES = 4
VMEM_LIMIT = 58 * 1024 * 1024


def _dot(a, b):
    return jnp.dot(a, b, preferred_element_type=f32)


def _dot_nt(a, b):
    return lax.dot_general(a, b, (((1,), (1,)), ((), ())), preferred_element_type=f32)


def _dot_tn(a, b):
    return lax.dot_general(a, b, (((0,), (0,)), ((), ())), preferred_element_type=f32)


def _softplus(z):
    return jnp.maximum(z, 0.0) + jnp.log1p(jnp.exp(-jnp.abs(z)))


def _sigmoid(z):
    return 1.0 / (1.0 + jnp.exp(-z))


def _silu(z):
    return z * _sigmoid(z)


def _gelu_tanh(z):
    c = math.sqrt(2.0 / math.pi)
    return z * (0.5 * (1.0 + jnp.tanh(c * (z + 0.044715 * (z * z * z)))))


def _rms_mod(x, ng, sc, sh):
    ms = jnp.mean(x * x, axis=-1, keepdims=True)
    y = x * lax.rsqrt(ms + EPS) * ng
    return y * (1.0 + sc) + sh


def _headnorm(x, gain):
    T, W = x.shape
    lane = lax.broadcasted_iota(jnp.int32, (T, 128), 1)
    lo = lane < HD
    cols = []
    for c in range(W // 128):
        xc = x[:, c * 128:(c + 1) * 128]
        x2 = xc * xc
        s_lo = jnp.sum(jnp.where(lo, x2, 0.0), axis=-1, keepdims=True)
        s_hi = jnp.sum(jnp.where(lo, 0.0, x2), axis=-1, keepdims=True)
        r_lo = lax.rsqrt(s_lo * (1.0 / HD) + EPS)
        r_hi = lax.rsqrt(s_hi * (1.0 / HD) + EPS)
        cols.append(xc * jnp.where(lo, r_lo, r_hi))
    y = cols[0] if len(cols) == 1 else jnp.concatenate(cols, axis=-1)
    return y * gain


def _keep_half(x, e):
    T, W = x.shape
    lo = lax.broadcasted_iota(jnp.int32, (T, 128), 1) < HD
    keep = lo if e == 0 else jnp.logical_not(lo)
    cols = [jnp.where(keep, x[:, c * 128:(c + 1) * 128], 0.0) for c in range(W // 128)]
    return cols[0] if len(cols) == 1 else jnp.concatenate(cols, axis=-1)


def _rope(x, cos, sin_signed):
    T, W = x.shape
    lane = lax.broadcasted_iota(jnp.int32, (T, 128), 1)
    even = (lane & 1) == 0
    cols = []
    for c in range(W // 128):
        xc = x[:, c * 128:(c + 1) * 128]
        sw = jnp.where(even, pltpu.roll(xc, 127, 1), pltpu.roll(xc, 1, 1))
        cols.append(xc * cos + sw * sin_signed)
    return cols[0] if len(cols) == 1 else jnp.concatenate(cols, axis=-1)


def _attend_stages(scores, v_list, store):
    s_list = scores()
    yield
    m = s_list[0].max(axis=-1, keepdims=True)
    for s in s_list[1:]:
        m = jnp.maximum(m, s.max(axis=-1, keepdims=True))
    yield
    p_list = [jnp.exp(s - m) for s in s_list]
    l = None
    for p in p_list:
        ls = jnp.sum(p, axis=-1, keepdims=True)
        l = ls if l is None else l + ls
    yield
    o = None
    for p, v in zip(p_list, v_list):
        os_ = _dot(p.astype(bf16), v)
        o = os_ if o is None else o + os_
    store(o * (1.0 / l))
    yield


def _interleave(*stages):
    live = list(stages)
    while live:
        for g in list(live):
            try:
                next(g)
            except StopIteration:
                live.remove(g)
        yield


def _every(stages, n):
    while True:
        for _ in range(n):
            try:
                next(stages)
            except StopIteration:
                return
        yield


def _delayed(stages, n):
    for _ in range(n):
        yield
    yield from stages


def _run(stages):
    for _ in stages:
        pass


def _loop(n, body):
    def step(i, carry):
        body(i)
        return carry
    lax.fori_loop(0, n, step, 0)


def _mod_kernel(c_ref, w_ref, b_ref, o_ref):
    cs = _silu(c_ref[...]).astype(bf16)
    o_ref[...] = _dot(cs, w_ref[...].astype(bf16)) + b_ref[...]


def _modulation(cvec, w_mod, b_mod):
    return pl.pallas_call(
        _mod_kernel,
        grid=(DEPTH, N_MOD),
        in_specs=[
            pl.BlockSpec((8, D), lambda l, j: (0, 0)),
            pl.BlockSpec((None, D, D), lambda l, j: (l, 0, j)),
            pl.BlockSpec((None, None, 1, D), lambda l, j: (l, j, 0, 0)),
        ],
        out_specs=pl.BlockSpec((None, None, 8, D), lambda l, j: (l, j, 0, 0)),
        out_shape=jax.ShapeDtypeStruct((DEPTH, N_MOD, 8, D), f32),
        compiler_params=pltpu.CompilerParams(
            dimension_semantics=("parallel", "parallel"), vmem_limit_bytes=VMEM_LIMIT),
        name="adaln_mod",
    )(cvec, w_mod, b_mod.reshape(DEPTH, N_MOD, 1, D))


def _ffn_rows(load, store, sh_ref, sc_ref, gt_ref, ng_ref, wi_ref, wo_ref):
    x = load()
    h = _rms_mod(x, ng_ref[...], sc_ref[...], sh_ref[...]).astype(bf16)
    yield
    acc = None
    for c0, c1 in FFN_CHUNKS:
        a = _dot(h, wi_ref[:, c0:c1])
        b = _dot(h, wi_ref[:, D_FF + c0:D_FF + c1])
        g = (_silu(a) * b).astype(bf16)
        y = _dot(g, wo_ref[c0:c1, :])
        acc = y if acc is None else acc + y
        yield
    store(x + (0.5 * gt_ref[...]) * acc)


_FFN_PASSES = tuple(slice(r, r + FFN_SUB) for r in range(0, FFN_TM, FFN_SUB))


def _ffn_passes(load, store, *refs):
    _run(_interleave(*[
        _delayed(_ffn_rows(functools.partial(load, rows), functools.partial(store, rows), *refs), k)
        for k, rows in enumerate(_FFN_PASSES)]))


def _ffn_kernel(x_ref, sh_ref, sc_ref, gt_ref, ng_ref, wi_ref, wo_ref, o_ref):
    def store(rows, v):
        o_ref[rows] = v

    _ffn_passes(lambda rows: x_ref[rows], store, sh_ref, sc_ref, gt_ref, ng_ref, wi_ref, wo_ref)


def _ffn_split_in_kernel(xc_ref, xl_ref, sh_ref, sc_ref, gt_ref, ng_ref, wi_ref, wo_ref, o_ref):
    is_ctx = pl.program_id(0) < M_CTX // FFN_TM

    def store(rows, v):
        o_ref[rows] = v

    _ffn_passes(lambda rows: jnp.where(is_ctx, xc_ref[rows], xl_ref[rows]), store,
                sh_ref, sc_ref, gt_ref, ng_ref, wi_ref, wo_ref)


def _ffn_split_out_kernel(x_ref, sh_ref, sc_ref, gt_ref, ng_ref, wi_ref, wo_ref,
                          oc_ref, ol_ref):
    outs = {}

    def keep(rows, v):
        outs[rows.start] = v

    _ffn_passes(lambda rows: x_ref[rows], keep, sh_ref, sc_ref, gt_ref, ng_ref, wi_ref, wo_ref)
    is_ctx = pl.program_id(0) < M_CTX // FFN_TM

    @pl.when(is_ctx)
    def _():
        for rows in _FFN_PASSES:
            oc_ref[rows] = outs[rows.start]

    @pl.when(jnp.logical_not(is_ctx))
    def _():
        for rows in _FFN_PASSES:
            ol_ref[rows] = outs[rows.start]


def _mod_row_of_tile(i, rows_per_tile):
    n_ctx = M_CTX // rows_per_tile
    per_b = T_LAT // rows_per_tile
    return jnp.where(i < n_ctx, 0, 1 + jnp.maximum(i - n_ctx, 0) // per_b)


def _ffn(xs, mods, norm_g, wi, wo, l, f, split_out=False):
    j0 = 6 * f
    n_ctx = M_CTX // FFN_TM
    row = functools.partial(_mod_row_of_tile, rows_per_tile=FFN_TM)

    def mod_spec(j):
        return pl.BlockSpec((None, None, None, 1, D), lambda i: (l, j, row(i), 0, 0))

    ctx_rows = pl.BlockSpec((FFN_TM, D), lambda i: (jnp.minimum(i, n_ctx - 1), 0))
    lat_rows = pl.BlockSpec((FFN_TM, D), lambda i: (jnp.maximum(i - n_ctx, 0), 0))
    all_rows = pl.BlockSpec((FFN_TM, D), lambda i: (i, 0))
    split_in = isinstance(xs, tuple)
    assert not (split_in and split_out)
    if split_in:
        body, x_specs, xs = _ffn_split_in_kernel, [ctx_rows, lat_rows], list(xs)
    else:
        body, x_specs, xs = (_ffn_split_out_kernel if split_out else _ffn_kernel), [all_rows], [xs]
    if split_out:
        out_specs = [ctx_rows, lat_rows]
        out_shape = [jax.ShapeDtypeStruct((M_CTX, D), f32), jax.ShapeDtypeStruct((M_LAT, D), f32)]
    else:
        out_specs = all_rows
        out_shape = jax.ShapeDtypeStruct((M_ALL, D), f32)

    const = dict(pipeline_mode=pl.Buffered(1))
    return pl.pallas_call(
        body,
        grid=(M_ALL // FFN_TM,),
        in_specs=x_specs + [
            mod_spec(j0), mod_spec(j0 + 1), mod_spec(j0 + 2),
            pl.BlockSpec((None, None, 1, D), lambda i: (l, 2 * f, 0, 0)),
            pl.BlockSpec((None, None, D, 2 * D_FF), lambda i: (l, f, 0, 0), **const),
            pl.BlockSpec((None, None, D_FF, D), lambda i: (l, f, 0, 0), **const),
        ],
        out_specs=out_specs,
        out_shape=out_shape,
        compiler_params=pltpu.CompilerParams(
            dimension_semantics=("arbitrary",), vmem_limit_bytes=VMEM_LIMIT),
        name=f"ffn{f}",
    )(*xs, mods, mods, mods, norm_g, wi, wo)


def _scan8(a, b, h_init, reverse, emit):
    T, C = a.shape
    nb = T // 8
    row8 = lax.broadcasted_iota(jnp.int32, (8, C), 0)
    for s in (1, 2, 4):
        if reverse:
            a_r = pltpu.roll(a, T - s, 0)
            b_r = pltpu.roll(b, T - s, 0)
            edge = row8 < 8 - s
            a_s = jnp.concatenate([a_r[:T - 8], jnp.where(edge, a_r[T - 8:], 1.0)], axis=0)
            b_s = jnp.concatenate([b_r[:T - 8], jnp.where(edge, b_r[T - 8:], 0.0)], axis=0)
        else:
            a_r = pltpu.roll(a, s, 0)
            b_r = pltpu.roll(b, s, 0)
            edge = row8 >= s
            a_s = jnp.concatenate([jnp.where(edge, a_r[:8], 1.0), a_r[8:]], axis=0)
            b_s = jnp.concatenate([jnp.where(edge, b_r[:8], 0.0), b_r[8:]], axis=0)
        b = a * b_s + b
        a = a * a_s
        yield
    h = jnp.broadcast_to(h_init, (8, C))
    for n, v in enumerate(range(nb - 1, -1, -1) if reverse else range(nb)):
        h = a[v * 8:(v + 1) * 8] * h + b[v * 8:(v + 1) * 8]
        emit(v, h)
        if n % 4 == 3:
            yield
    return h[0:1] if reverse else h[7:8]


def _lru_group(h_ref, win_ref, cw_ref, cb_ref, wg_ref, bg_ref, lam_ref, h0, ycat_ref, st_ref,
               gate_sc):
    T = h_ref.shape[0]
    p = _dot(h_ref[...], win_ref[:, C_XA:C_XA + 2 * GW])
    xa = p[:, :GW]
    yield
    gate_sc[...] = _gelu_tanh(p[:, GW:])
    yield
    tpos = lax.broadcasted_iota(jnp.int32, (T, GW), 0)

    def shifted(k):
        if k > 0:
            return jnp.where(tpos >= k, pltpu.roll(xa, k, 0), 0.0)
        return jnp.where(tpos < T + k, pltpu.roll(xa, T + k, 0), 0.0)

    cw = cw_ref[...]
    xc = shifted(2) * cw[0:1] + shifted(1) * cw[1:2] + xa * cw[2:3] + shifted(-1) * cw[3:4]
    xc = xc + cb_ref[...]
    xcb = xc.astype(bf16)
    lam = lam_ref[...]
    yield
    for d in range(2):
        ds_ = slice(d * 2 * GW, (d + 1) * 2 * GW)
        pre = _dot(xcb, wg_ref[:, ds_]) + bg_ref[:, ds_]
        yield
        r = _sigmoid(pre[:, 0:GW])
        gi = _sigmoid(pre[:, GW:2 * GW])
        yield
        log_a = (-LRU_C * r) * _softplus(-lam[d:d + 1])
        a = jnp.exp(log_a)
        u = jnp.sqrt(1.0 - a * a) * (gi * xc)
        yield
        h_init = jnp.zeros((1, GW), f32) if h0 is None else h0[d:d + 1]

        def emit_fwd(v, hv):
            ycat_ref[v * 8:(v + 1) * 8, 0:GW] = hv

        def emit_bwd(v, hv):
            rows = slice(v * 8, (v + 1) * 8)
            ycat_ref[rows, 0:GW] = (ycat_ref[rows, 0:GW] + hv) * gate_sc[rows, :]

        last = yield from _scan8(a, u, h_init, d == 1, emit_bwd if d == 1 else emit_fwd)
        if st_ref is not None:
            st_ref[d:d + 1, :] = last
        yield


def _build_decay_tiles(dec_ref, dm_sc, nt):
    r = lax.broadcasted_iota(jnp.int32, (Q_TILE, Q_TILE), 0)
    c = lax.broadcasted_iota(jnp.int32, (Q_TILE, Q_TILE), 1)
    base = (r - c).astype(f32)
    for h in range(4):
        lgf = -_softplus(-dec_ref[h])
        lgb = -_softplus(-dec_ref[4 + h])
        for m in range(-(nt - 1), nt):
            diff = base + float(m * Q_TILE)
            if m > 0:
                mat = jnp.exp(lgf * diff)
            elif m < 0:
                mat = jnp.exp(lgb * (-diff))
            else:
                mat = (jnp.where(diff >= 0, jnp.exp(lgf * jnp.maximum(diff, 0.0)), 0.0)
                       + jnp.where(diff <= 0, jnp.exp(lgb * jnp.maximum(-diff, 0.0)), 0.0))
            dm_sc[h * (2 * nt - 1) + m + nt - 1] = mat


def _retention_group(h_ref, win_ref, dec_ref, gn_ref, s0_ref, ycat_ref, sret_ref,
                     q_sc, k_sc, v_sc, g_sc, dm_sc, tiles_out=None):
    T = h_ref.shape[0]
    nt = T // Q_TILE
    q_sc[...] = _dot(h_ref[...], win_ref[:, C_QD:C_QD + GW]).astype(bf16)
    yield
    kf = _dot(h_ref[...], win_ref[:, C_KD:C_KD + GW]) * ATT_SCALE
    k_sc[...] = kf.astype(bf16)
    yield
    v_sc[...] = _dot(h_ref[...], win_ref[:, C_VD:C_VD + GW]).astype(bf16)
    yield
    g_sc[...] = _dot(h_ref[...], win_ref[:, C_GD:C_GD + GW])
    gn = gn_ref[...]
    tcol = lax.broadcasted_iota(jnp.int32, (T, 1), 0).astype(f32)
    yield

    def head(h):
        hs = slice(h * HD, (h + 1) * HD)
        lgf = -_softplus(-dec_ref[h])
        lgb = -_softplus(-dec_ref[4 + h])
        kh = k_sc[:, hs]
        vh = v_sc[:, hs]
        if sret_ref is not None:
            zf = jnp.exp(lgf * ((T - 1.0) - tcol))
            zb = jnp.exp(lgb * tcol)
            kff = kf[:, hs]
            sret_ref[h * HD:(h + 1) * HD, :] = _dot_tn((kff * zf).astype(bf16), vh)
            sret_ref[(4 + h) * HD:(5 + h) * HD, :] = _dot_tn((kff * zb).astype(bf16), vh)
        if s0_ref is not None:
            s0f = s0_ref[h * HD:(h + 1) * HD, :].astype(bf16)
            s0b = s0_ref[(4 + h) * HD:(5 + h) * HD, :].astype(bf16)
        gnh = gn[:, hs]

        def tile(i):
            t0 = pl.multiple_of(i * Q_TILE, Q_TILE)
            rows = pl.ds(t0, Q_TILE)
            qh = q_sc[rows, hs]
            o = None
            for j in range(nt):
                ks = slice(j * Q_TILE, (j + 1) * Q_TILE)
                dm = dm_sc[h * (2 * nt - 1) + (nt - 1 - j) + i]
                sd = (_dot_nt(qh, kh[ks]) * dm).astype(bf16)
                yield
                oj = _dot(sd, vh[ks])
                o = oj if o is None else o + oj
                yield
            if s0_ref is not None:
                tq = (lax.broadcasted_iota(jnp.int32, (Q_TILE, 1), 0) + t0).astype(f32)
                o = o + _dot(qh, s0f) * jnp.exp(lgf * (tq + 1.0))
                o = o + _dot(qh, s0b) * jnp.exp(lgb * (T - tq))
            mu = jnp.mean(o, axis=-1, keepdims=True)
            oc = o - mu
            var = jnp.mean(oc * oc, axis=-1, keepdims=True)
            yield
            on = oc * lax.rsqrt(var + EPS) * gnh
            ycat_ref[rows, 3 * GW + h * HD:3 * GW + (h + 1) * HD] = on * _silu(g_sc[rows, hs])
            yield

        return tile

    if nt == 1:
        for h in range(4):
            tile = head(h)
            yield
            yield from tile(0)
    else:
        tiles_out.extend(head(h) for h in range(4))


N_CTX_IN = 18


def _ctx_mixer_kernel(*refs):
    n_out = 7
    n_scratch = 8
    ins = refs[:N_CTX_IN]
    outs = refs[len(refs) - n_out - n_scratch:]
    _ctx_mixer_body(*ins, *outs)


def _ctx_mixer_body(x_ref, sh_ref, sc_ref, gt_ref, ng_ref, win_ref, wout_ref,
                    cw_ref, cb_ref, wg_ref, bg_ref, lam_ref,
                    qnb_ref, knb_ref, qnc_ref, knc_ref, dec_ref, gn_ref,
                    o_ref, kb_ref, vb_ref, kc_ref, vc_ref, st_ref, sret_ref,
                    h_ref, ycat_ref, q_sc, k_sc, v_sc, g_sc, gate_sc, dm_sc):
    @pl.when(pl.program_id(0) == 0)
    def _():
        _build_decay_tiles(dec_ref, dm_sc, T_CTX // Q_TILE)

    h_ref[...] = _rms_mod(x_ref[...], ng_ref[...], sc_ref[...], sh_ref[...]).astype(bf16)

    _run(_interleave(
        _lru_group(h_ref, win_ref, cw_ref, cb_ref, wg_ref, bg_ref, lam_ref, None, ycat_ref,
                   st_ref, gate_sc),
        _ctx_attention_group(h_ref, win_ref, C_QB, 2, qnb_ref, knb_ref, kb_ref, vb_ref,
                             ycat_ref, GW),
        _ctx_attention_group(h_ref, win_ref, C_QC, 4, qnc_ref, knc_ref, kc_ref, vc_ref,
                             ycat_ref, 2 * GW),
        _retention_group(h_ref, win_ref, dec_ref, gn_ref, None, ycat_ref, sret_ref,
                         q_sc, k_sc, v_sc, g_sc, dm_sc),
    ))
    y = _dot(ycat_ref[...].astype(bf16), wout_ref[...])
    o_ref[...] = x_ref[...] + gt_ref[...] * y


def _ctx_attention_group(h_ref, win_ref, col, n_kv, qn_ref, kn_ref, k_out_ref, v_out_ref,
                         ycat_ref, ycol):
    kw = n_kv * HD
    p = _dot(h_ref[...], win_ref[:, col:col + GW + 2 * kw])
    yield
    qf = _headnorm(p[:, 0:GW], qn_ref[...] * ATT_SCALE)
    yield
    kn = _headnorm(p[:, GW:GW + kw], kn_ref[...])
    vf = p[:, GW + kw:GW + 2 * kw]
    k_out_ref[...] = kn
    v_out_ref[...] = vf
    k = kn.astype(bf16)
    v = vf.astype(bf16)
    yield
    lo = lax.broadcasted_iota(jnp.int32, (p.shape[0], 128), 1) < HD
    pending = {}

    def head(hh):
        j, e = divmod(hh, 2)
        pair = slice(j * 128, (j + 1) * 128)
        kk, vv = (k[:, pair], v[:, pair]) if n_kv == 4 else (k, v)

        def scores():
            qm = jnp.where(lo if e == 0 else jnp.logical_not(lo), qf[:, pair], 0.0).astype(bf16)
            return [_dot_nt(qm, kk)]

        def store(o):
            if e == 0:
                pending[j] = o
            else:
                ycat_ref[:, ycol + j * 128:ycol + (j + 1) * 128] = jnp.where(lo, pending[j], o)

        return _attend_stages(scores, [vv], store)

    yield from _interleave(*[head(hh) for hh in range(4)])


def _lat_mixer_kernel(x_ref, sh_ref, sc_ref, gt_ref, ng_ref, win_ref, wout_ref,
                      cw_ref, cb_ref, wg_ref, bg_ref, lam_ref,
                      qnb_ref, knb_ref, qnc_ref, knc_ref, dec_ref, gn_ref,
                      cos_ref, sin_ref, bias_ref,
                      cbk_ref, cbv_ref, cck_ref, ccv_ref, lru0_ref, ret0_ref,
                      o_ref,
                      h_ref, q_sc, k_sc, v_sc, g_sc, gate_sc, dm_sc,
                      qb_sc, kb_sc, vb_sc, qc_sc, kc_sc, vc_sc):
    T = T_LAT
    ycat_ref = o_ref

    @pl.when(pl.program_id(0) == 0)
    def _():
        _build_decay_tiles(dec_ref, dm_sc, T // Q_TILE)

    h_ref[...] = _rms_mod(x_ref[...], ng_ref[...], sc_ref[...], sh_ref[...]).astype(bf16)

    def prep_b():
        cos = cos_ref[...]
        sin = sin_ref[...]
        p = _dot(h_ref[...], win_ref[:, C_QB:C_QB + GW])
        yield
        qr = _rope(_headnorm(p, qnb_ref[...] * ATT_SCALE), cos, sin)
        for e in range(2):
            qb_sc[e] = _keep_half(qr, e).astype(bf16)
        yield
        p = _dot(h_ref[...], win_ref[:, C_KB:C_KB + GW])
        yield
        kb_sc[0:T, :] = _rope(_headnorm(p[:, 0:128], knb_ref[...]), cos, sin).astype(bf16)
        kb_sc[T:T + PAST, :] = cbk_ref[...].astype(bf16)
        vb_sc[0:T, :] = p[:, 128:GW].astype(bf16)
        vb_sc[T:T + PAST, :] = cbv_ref[...].astype(bf16)
        yield

    def prep_c():
        p = _dot(h_ref[...], win_ref[:, C_QC:C_QC + GW])
        yield
        qn = _headnorm(p, qnc_ref[...] * ATT_SCALE)
        for e in range(2):
            qc_sc[e] = _keep_half(qn, e).astype(bf16)
        yield
        p = _dot(h_ref[...], win_ref[:, C_KC:C_KC + GW])
        yield
        kc_sc[0:T, :] = _headnorm(p, knc_ref[...]).astype(bf16)
        kc_sc[T:T + PAST, :] = cck_ref[...].astype(bf16)
        yield
        vc_sc[0:T, :] = _dot(h_ref[...], win_ref[:, C_VC:C_VC + GW]).astype(bf16)
        vc_sc[T:T + PAST, :] = ccv_ref[...].astype(bf16)
        yield

    ret_tiles = []
    retention = _retention_group(h_ref, win_ref, dec_ref, gn_ref, ret0_ref, ycat_ref, None,
                                 q_sc, k_sc, v_sc, g_sc, dm_sc, ret_tiles)

    def preps():
        yield from prep_b()
        yield from prep_c()
        for _ in range(RET_PREP_STAGES):
            next(retention)
            yield

    _run(_interleave(
        _every(_lru_group(h_ref, win_ref, cw_ref, cb_ref, wg_ref, bg_ref, lam_ref, lru0_ref[...],
                          ycat_ref, None, gate_sc), LRU_STAGES_PER_ROUND),
        preps()))
    _run(retention)

    lo_t = lax.broadcasted_iota(jnp.int32, (Q_TILE, 128), 1) < HD
    lo_r = lax.broadcasted_iota(jnp.int32, (GRID_W, 128), 1) < HD
    kvk = kb_sc[...]
    kvv = vb_sc[...]

    def tile_b(i):
        rows = pl.ds(pl.multiple_of(i * Q_TILE, Q_TILE), Q_TILE)
        pending = {}

        def head(hh):
            j, e = divmod(hh, 2)
            pair = slice(j * 128, (j + 1) * 128)

            def store(o):
                if e == 0:
                    pending[j] = o
                else:
                    ycat_ref[rows, GW + j * 128:GW + (j + 1) * 128] = jnp.where(lo_t, pending[j], o)

            return _attend_stages(lambda: [_dot_nt(qb_sc[e, rows, pair], kvk)], [kvv], store)

        return _interleave(*[head(hh) for hh in range(4)])

    n_loc = NAT_WR * GRID_W
    qcol = lax.broadcasted_iota(jnp.int32, (GRID_W, n_loc), 0)
    kcol = lax.broadcasted_iota(jnp.int32, (GRID_W, n_loc), 1) & (GRID_W - 1)
    cstart = jnp.clip(qcol - NAT_WC // 2, 0, GRID_W - NAT_WC)
    in_win = (kcol >= cstart) & (kcol < cstart + NAT_WC)

    def row_block(r):
        rs = jnp.clip(r - NAT_WR // 2, 0, GRID_R - NAT_WR)
        didx = rs - r + (NAT_WR - 1)
        qrows = pl.ds(pl.multiple_of(r * GRID_W, GRID_W), GRID_W)
        krows = pl.ds(pl.multiple_of(rs * GRID_W, GRID_W), n_loc)

        pending = {}

        def head(hh):
            j, e = divmod(hh, 2)
            pair = slice(j * 128, (j + 1) * 128)

            def scores():
                qh = qc_sc[e, qrows, pair]
                s_loc = _dot_nt(qh, kc_sc[krows, pair])
                s_loc = jnp.where(in_win, s_loc + bias_ref[hh, didx], NEG_INF)
                return [s_loc, _dot_nt(qh, kc_sc[T:T + PAST, pair])]

            def store(o):
                if e == 0:
                    pending[j] = o
                else:
                    ycat_ref[qrows, 2 * GW + j * 128:2 * GW + (j + 1) * 128] = jnp.where(
                        lo_r, pending[j], o)

            return _attend_stages(scores, [vc_sc[krows, pair], vc_sc[T:T + PAST, pair]], store)

        return _interleave(*[head(hh) for hh in range(4)])

    rows_per_tile = Q_TILE // GRID_W

    def fused_tile(i):
        def b_then_d():
            yield from tile_b(i)
            yield from _interleave(*[tile(i) for tile in ret_tiles])

        def c_rows():
            for rb in range(rows_per_tile):
                yield from row_block(i * rows_per_tile + rb)

        _run(_interleave(b_then_d(), c_rows()))

    _loop(T // Q_TILE, fused_tile)

    y = _dot(ycat_ref[...].astype(bf16), wout_ref[...])
    o_ref[...] = x_ref[...] + gt_ref[...] * y


def _layer_specs(l, mod_row):
    const = dict(pipeline_mode=pl.Buffered(1))

    def lspec(shape):
        nd = len(shape)
        return pl.BlockSpec((None,) + shape, lambda i: (l,) + (0,) * nd, **const)

    def mod_spec(j):
        return pl.BlockSpec((None, None, None, 1, D), lambda i: (l, j, mod_row(i), 0, 0))

    return [
        mod_spec(3), mod_spec(4), mod_spec(5),
        pl.BlockSpec((None, None, 1, D), lambda i: (l, 1, 0, 0)),
        lspec((D, IN_COLS)), lspec((D, D)),
        lspec((4, GW)), lspec((1, GW)), lspec((GW, 4 * GW)), lspec((1, 4 * GW)), lspec((2, GW)),
        lspec((1, GW)), lspec((1, 128)), lspec((1, GW)), lspec((1, GW)),
        lspec((8, 1, 1)), lspec((1, GW)),
    ]


def _ctx_mixer(x, mods, norm_g, lw, l, prev):
    T = T_CTX

    def layer_spec(*tail):
        return pl.BlockSpec((None, None) + tail, lambda i: (i, l) + (0,) * len(tail))

    tails = [(T, 128), (T, 128), (T, GW), (T, GW), (2, GW), (8 * HD, HD)]
    operands = [x, mods, mods, mods, norm_g, lw["w_in"], lw["w_out"],
                lw["conv_w"], lw["conv_b"], lw["wg"], lw["bg"], lw["lam"],
                lw["qnb"], lw["knb"], lw["qnc"], lw["knc"], lw["dec"], lw["gn"]]
    assert len(operands) == N_CTX_IN
    aliases = {0: 0}
    prev_specs = []
    if prev is not None:
        prev_specs = [pl.BlockSpec(memory_space=pl.ANY)] * len(prev)
        aliases.update({N_CTX_IN + k: 1 + k for k in range(len(prev))})
        operands = operands + list(prev)

    outs = pl.pallas_call(
        _ctx_mixer_kernel,
        grid=(N_CTX,),
        in_specs=[pl.BlockSpec((T, D), lambda i: (i, 0))] + _layer_specs(l, lambda i: 0)
        + prev_specs,
        out_specs=[pl.BlockSpec((T, D), lambda i: (i, 0))] + [layer_spec(*t) for t in tails],
        out_shape=[jax.ShapeDtypeStruct((M_ALL, D), f32)]
        + [jax.ShapeDtypeStruct((N_CTX, DEPTH) + t, f32) for t in tails],
        scratch_shapes=[
            pltpu.VMEM((T, D), bf16), pltpu.VMEM((T, D), f32),
            pltpu.VMEM((T, GW), bf16), pltpu.VMEM((T, GW), bf16), pltpu.VMEM((T, GW), bf16),
            pltpu.VMEM((T, GW), f32), pltpu.VMEM((T, GW), f32),
            pltpu.VMEM((4, Q_TILE, Q_TILE), f32),
        ],
        input_output_aliases=aliases,
        compiler_params=pltpu.CompilerParams(
            dimension_semantics=("arbitrary",), vmem_limit_bytes=VMEM_LIMIT),
        name="ctx_mixer",
    )(*operands)
    return outs[0], tuple(outs[1:])


def _lat_mixer(x, mods, norm_g, lw, l, cos, sin, caches):
    T = T_LAT
    blk0 = M_CTX // T
    const = dict(pipeline_mode=pl.Buffered(1))
    cbk, cbv, cck, ccv, lru0, ret0 = caches

    def cache_spec(w):
        return pl.BlockSpec((None, None, PAST, w), lambda i: (i, l, 0, 0))

    return pl.pallas_call(
        _lat_mixer_kernel,
        grid=(N_LAT,),
        in_specs=[pl.BlockSpec((T, D), lambda i: (blk0 + i, 0), **const)]
        + _layer_specs(l, lambda i: 1 + i)
        + [
            pl.BlockSpec((T, 128), lambda i: (0, 0), **const),
            pl.BlockSpec((T, 128), lambda i: (0, 0), **const),
            pl.BlockSpec((None, 4, NAT_WR, GRID_W, NAT_WR * GRID_W),
                         lambda i: (l, 0, 0, 0, 0), **const),
            cache_spec(128), cache_spec(128), cache_spec(GW), cache_spec(GW),
            pl.BlockSpec((None, None, 2, GW), lambda i: (i, l, 0, 0)),
            pl.BlockSpec((None, None, 8 * HD, HD), lambda i: (i, l, 0, 0)),
        ],
        out_specs=pl.BlockSpec((T, D), lambda i: (blk0 + i, 0), **const),
        out_shape=jax.ShapeDtypeStruct((M_ALL, D), f32),
        scratch_shapes=[
            pltpu.VMEM((T, D), bf16),
            pltpu.VMEM((T, GW), bf16), pltpu.VMEM((T, GW), bf16), pltpu.VMEM((T, GW), bf16),
            pltpu.VMEM((T, GW), f32), pltpu.VMEM((T, GW), f32),
            pltpu.VMEM((4 * (2 * (T // Q_TILE) - 1), Q_TILE, Q_TILE), f32),
            pltpu.VMEM((2, T, GW), bf16),
            pltpu.VMEM((T + PAST, 128), bf16), pltpu.VMEM((T + PAST, 128), bf16),
            pltpu.VMEM((2, T, GW), bf16),
            pltpu.VMEM((T + PAST, GW), bf16), pltpu.VMEM((T + PAST, GW), bf16),
        ],
        input_output_aliases={0: 0},
        compiler_params=pltpu.CompilerParams(
            dimension_semantics=("arbitrary",), vmem_limit_bytes=VMEM_LIMIT),
        name="lat_mixer",
    )(x, mods, mods, mods, norm_g, lw["w_in"], lw["w_out"],
      lw["conv_w"], lw["conv_b"], lw["wg"], lw["bg"], lw["lam"],
      lw["qnb"], lw["knb"], lw["qnc"], lw["knc"], lw["dec"], lw["gn"],
      cos, sin, lw["nat_bias"], cbk, cbv, cck, ccv, lru0, ret0)


def _block_diag(w):
    eye = jnp.eye(4, dtype=w.dtype)
    return jnp.einsum("lncd,nm->lncmd", w, eye).reshape(DEPTH, GW, GW)


def _swap_middle_heads(w, axis, start):
    def cut(a, b):
        return lax.slice_in_dim(w, a, b, axis=axis)

    n = w.shape[axis]
    return jnp.concatenate([cut(0, start + HD), cut(start + 2 * HD, start + 3 * HD),
                            cut(start + HD, start + 2 * HD), cut(start + 3 * HD, n)], axis=axis)


def _rope_tables():
    t = jnp.arange(T_LAT)
    row = (t // GRID_W).astype(f32)
    col = (t % GRID_W).astype(f32)
    n_freq = HD // 4
    inv = ROPE_BASE ** (-jnp.arange(n_freq, dtype=f32) / n_freq)
    ang = jnp.concatenate([row[:, None] * inv, col[:, None] * inv], axis=-1)
    cos = jnp.repeat(jnp.cos(ang), 2, axis=-1)
    sin = jnp.repeat(jnp.sin(ang), 2, axis=-1)
    sign = jnp.where(jnp.arange(HD) % 2 == 0, -1.0, 1.0).astype(f32)
    return jnp.tile(cos, (1, 2)), jnp.tile(sin * sign, (1, 2))


def _nat_bias_windows(nat_bias):
    pad = GRID_W - NAT_WC
    ext = jnp.concatenate([jnp.repeat(nat_bias[..., :1], pad, -1), nat_bias,
                           jnp.repeat(nat_bias[..., -1:], pad + 1, -1)], axis=-1)
    t = jnp.tile(ext, (1, 1, 1, GRID_W))[..., :GRID_W * 127]
    t = t.reshape(DEPTH, 4, 2 * NAT_WR - 1, GRID_W, 127)[..., GRID_W - 1:]
    tab = jnp.stack([t[:, :, d:d + NAT_WR] for d in range(NAT_WR)], axis=2)
    tab = tab.transpose(0, 1, 2, 4, 3, 5)
    return tab.reshape(DEPTH, 4, NAT_WR, GRID_W, NAT_WR * GRID_W)


def kernel(x_prompt, x_sample, cache_b_k, cache_b_v, cache_c_k, cache_c_v, state_lru, state_ret, c, c_ctx, w_mod, b_mod, norm_g, ffn_w_in, ffn_w_out, w_in, w_out, conv_w, conv_b, lru_w_r, lru_b_r, lru_w_i, lru_b_i, lru_lambda, gqa_qn, gqa_kn, nat_qn, nat_kn, nat_bias, ret_decay, ret_gn):
    cvec = jnp.concatenate([c_ctx[None, :], c, jnp.zeros((8 - 1 - N_LAT, D), f32)], axis=0)
    mods = _modulation(cvec, w_mod, b_mod).reshape(DEPTH, N_MOD, 8, 1, D)
    ng = norm_g.reshape(DEPTH, 3, 1, D)

    wi = ffn_w_in.astype(bf16)
    wo = ffn_w_out.astype(bf16)
    lw = dict(
        w_in=_swap_middle_heads(w_in, 2, C_QB).astype(bf16),
        w_out=_swap_middle_heads(w_out, 1, GW).astype(bf16),
        conv_w=conv_w,
        conv_b=conv_b.reshape(DEPTH, 1, GW),
        wg=jnp.concatenate([_block_diag(lru_w_r[:, 0]), _block_diag(lru_w_i[:, 0]),
                            _block_diag(lru_w_r[:, 1]), _block_diag(lru_w_i[:, 1])],
                           axis=-1).astype(bf16),
        bg=jnp.concatenate([lru_b_r[:, 0], lru_b_i[:, 0], lru_b_r[:, 1], lru_b_i[:, 1]],
                           axis=-1).reshape(DEPTH, 1, 4 * GW),
        lam=lru_lambda,
        qnb=jnp.tile(gqa_qn, (1, 4)).reshape(DEPTH, 1, GW),
        knb=jnp.tile(gqa_kn, (1, 2)).reshape(DEPTH, 1, 128),
        qnc=jnp.tile(nat_qn, (1, 4)).reshape(DEPTH, 1, GW),
        knc=jnp.tile(nat_kn, (1, 4)).reshape(DEPTH, 1, GW),
        dec=ret_decay.reshape(DEPTH, 8, 1, 1),
        gn=ret_gn.reshape(DEPTH, 1, GW),
        nat_bias=_nat_bias_windows(nat_bias),
    )
    cos, sin = _rope_tables()
    caches = (cache_b_k.reshape(N_LAT, DEPTH, PAST, 128), cache_b_v.reshape(N_LAT, DEPTH, PAST, 128),
              cache_c_k.reshape(N_LAT, DEPTH, PAST, GW), cache_c_v.reshape(N_LAT, DEPTH, PAST, GW),
              state_lru, state_ret.reshape(N_LAT, DEPTH, 8 * HD, HD))

    x = (x_prompt.reshape(M_CTX, D), x_sample.reshape(M_LAT, D))
    new = None
    for l in range(DEPTH):
        x = _ffn(x, mods, ng, wi, wo, l, 0)
        x, new = _ctx_mixer(x, mods, ng, lw, l, new)
        x = _lat_mixer(x, mods, ng, lw, l, cos, sin, caches)
        x = _ffn(x, mods, ng, wi, wo, l, 1, split_out=(l == DEPTH - 1))

    new_bk, new_bv, new_ck, new_cv, new_lru, new_ret = new
    return (x[0].reshape(N_CTX, T_CTX, D), x[1].reshape(N_LAT, T_LAT, D),
            new_bk.reshape(N_CTX, DEPTH, T_CTX, 2, HD), new_bv.reshape(N_CTX, DEPTH, T_CTX, 2, HD),
            new_ck.reshape(N_CTX, DEPTH, T_CTX, 4, HD), new_cv.reshape(N_CTX, DEPTH, T_CTX, 4, HD),
            new_lru, new_ret.reshape(N_CTX, DEPTH, 2, 4, HD, HD))
```

```python
import functools
import math

import jax
import jax.numpy as jnp
from jax import lax
from jax.experimental import pallas as pl
from jax.experimental.pallas import tpu as pltpu

f32 = jnp.float32
bf16 = jnp.bfloat16

D = 1024
DEPTH = 4
N_CTX, T_CTX = 32, 256
N_LAT, T_LAT = 4, 1024
M_CTX = N_CTX * T_CTX
M_LAT = N_LAT * T_LAT
M_ALL = M_CTX + M_LAT
PAST = 256
GRID_W = 64
GRID_R = T_LAT // GRID_W
HD = 64
GW = 256
LRU_C = 8.0
NAT_WR, NAT_WC = 8, 16
D_FF = 2816
IN_COLS = 2816
EPS = 1e-6
NEG_INF = -1e30
N_MOD = 9
ATT_SCALE = HD ** -0.5
ROPE_BASE = 10000.0

C_XA, C_GA = 0, 256
C_QB, C_KB, C_VB = 512, 768, 896
C_QC, C_KC, C_VC = 1024, 1280, 1536
C_QD, C_KD, C_VD, C_GD = 1792, 2048, 2304, 2560

FFN_TM = 1024
FFN_SUB = 256
FFN_CHUNKS = ((0, 768), (768, 1536), (1536, 2304), (2304, 2816))
Q_TILE = 256
LRU_STAGES_PER_ROUND = 3
RET_PREP_STAGES = 4
VMEM_LIMIT = 58 * 1024 * 1024


def _dot(a, b):
    return jnp.dot(a, b, preferred_element_type=f32)


def _dot_nt(a, b):
    return lax.dot_general(a, b, (((1,), (1,)), ((), ())), preferred_element_type=f32)


def _dot_tn(a, b):
    return lax.dot_general(a, b, (((0,), (0,)), ((), ())), preferred_element_type=f32)


def _softplus(z):
    return jnp.maximum(z, 0.0) + jnp.log1p(jnp.exp(-jnp.abs(z)))


def _sigmoid(z):
    return 1.0 / (1.0 + jnp.exp(-z))


def _silu(z):
    return z * _sigmoid(z)


def _gelu_tanh(z):
    c = math.sqrt(2.0 / math.pi)
    return z * (0.5 * (1.0 + jnp.tanh(c * (z + 0.044715 * (z * z * z)))))


def _rms_mod(x, ng, sc, sh):
    ms = jnp.mean(x * x, axis=-1, keepdims=True)
    y = x * lax.rsqrt(ms + EPS) * ng
    return y * (1.0 + sc) + sh


def _headnorm(x, gain):
    T, W = x.shape
    lane = lax.broadcasted_iota(jnp.int32, (T, 128), 1)
    lo = lane < HD
    cols = []
    for c in range(W // 128):
        xc = x[:, c * 128:(c + 1) * 128]
        x2 = xc * xc
        s_lo = jnp.sum(jnp.where(lo, x2, 0.0), axis=-1, keepdims=True)
        s_hi = jnp.sum(jnp.where(lo, 0.0, x2), axis=-1, keepdims=True)
        r_lo = lax.rsqrt(s_lo * (1.0 / HD) + EPS)
        r_hi = lax.rsqrt(s_hi * (1.0 / HD) + EPS)
        cols.append(xc * jnp.where(lo, r_lo, r_hi))
    y = cols[0] if len(cols) == 1 else jnp.concatenate(cols, axis=-1)
    return y * gain


def _keep_half(x, e):
    T, W = x.shape
    lo = lax.broadcasted_iota(jnp.int32, (T, 128), 1) < HD
    keep = lo if e == 0 else jnp.logical_not(lo)
    cols = [jnp.where(keep, x[:, c * 128:(c + 1) * 128], 0.0) for c in range(W // 128)]
    return cols[0] if len(cols) == 1 else jnp.concatenate(cols, axis=-1)


def _rope(x, cos, sin_signed):
    T, W = x.shape
    lane = lax.broadcasted_iota(jnp.int32, (T, 128), 1)
    even = (lane & 1) == 0
    cols = []
    for c in range(W // 128):
        xc = x[:, c * 128:(c + 1) * 128]
        sw = jnp.where(even, pltpu.roll(xc, 127, 1), pltpu.roll(xc, 1, 1))
        cols.append(xc * cos + sw * sin_signed)
    return cols[0] if len(cols) == 1 else jnp.concatenate(cols, axis=-1)


def _attend_stages(scores, v_list, store):
    s_list = scores()
    yield
    m = s_list[0].max(axis=-1, keepdims=True)
    for s in s_list[1:]:
        m = jnp.maximum(m, s.max(axis=-1, keepdims=True))
    yield
    p_list = [jnp.exp(s - m) for s in s_list]
    l = None
    for p in p_list:
        ls = jnp.sum(p, axis=-1, keepdims=True)
        l = ls if l is None else l + ls
    yield
    o = None
    for p, v in zip(p_list, v_list):
        os_ = _dot(p.astype(bf16), v)
        o = os_ if o is None else o + os_
    store(o * (1.0 / l))
    yield


def _interleave(*stages):
    live = list(stages)
    while live:
        for g in list(live):
            try:
                next(g)
            except StopIteration:
                live.remove(g)
        yield


def _every(stages, n):
    while True:
        for _ in range(n):
            try:
                next(stages)
            except StopIteration:
                return
        yield


def _delayed(stages, n):
    for _ in range(n):
        yield
    yield from stages


def _run(stages):
    for _ in stages:
        pass


def _loop(n, body):
    def step(i, carry):
        body(i)
        return carry
    lax.fori_loop(0, n, step, 0)


def _mod_kernel(c_ref, w_ref, b_ref, o_ref):
    cs = _silu(c_ref[...]).astype(bf16)
    o_ref[...] = _dot(cs, w_ref[...].astype(bf16)) + b_ref[...]


def _modulation(cvec, w_mod, b_mod):
    return pl.pallas_call(
        _mod_kernel,
        grid=(DEPTH, N_MOD),
        in_specs=[
            pl.BlockSpec((8, D), lambda l, j: (0, 0)),
            pl.BlockSpec((None, D, D), lambda l, j: (l, 0, j)),
            pl.BlockSpec((None, None, 1, D), lambda l, j: (l, j, 0, 0)),
        ],
        out_specs=pl.BlockSpec((None, None, 8, D), lambda l, j: (l, j, 0, 0)),
        out_shape=jax.ShapeDtypeStruct((DEPTH, N_MOD, 8, D), f32),
        compiler_params=pltpu.CompilerParams(
            dimension_semantics=("parallel", "parallel"), vmem_limit_bytes=VMEM_LIMIT),
        name="adaln_mod",
    )(cvec, w_mod, b_mod.reshape(DEPTH, N_MOD, 1, D))


def _ffn_rows(load, store, sh_ref, sc_ref, gt_ref, ng_ref, wi_ref, wo_ref):
    x = load()
    h = _rms_mod(x, ng_ref[...], sc_ref[...], sh_ref[...]).astype(bf16)
    yield
    acc = None
    for c0, c1 in FFN_CHUNKS:
        a = _dot(h, wi_ref[:, c0:c1])
        b = _dot(h, wi_ref[:, D_FF + c0:D_FF + c1])
        g = (_silu(a) * b).astype(bf16)
        y = _dot(g, wo_ref[c0:c1, :])
        acc = y if acc is None else acc + y
        yield
    store(x + (0.5 * gt_ref[...]) * acc)


_FFN_PASSES = tuple(slice(r, r + FFN_SUB) for r in range(0, FFN_TM, FFN_SUB))


def _ffn_passes(load, store, *refs):
    _run(_interleave(*[
        _delayed(_ffn_rows(functools.partial(load, rows), functools.partial(store, rows), *refs), k)
        for k, rows in enumerate(_FFN_PASSES)]))


def _ffn_kernel(x_ref, sh_ref, sc_ref, gt_ref, ng_ref, wi_ref, wo_ref, o_ref):
    def store(rows, v):
        o_ref[rows] = v

    _ffn_passes(lambda rows: x_ref[rows], store, sh_ref, sc_ref, gt_ref, ng_ref, wi_ref, wo_ref)


def _ffn_split_in_kernel(xc_ref, xl_ref, sh_ref, sc_ref, gt_ref, ng_ref, wi_ref, wo_ref, o_ref):
    is_ctx = pl.program_id(0) < M_CTX // FFN_TM

    def store(rows, v):
        o_ref[rows] = v

    _ffn_passes(lambda rows: jnp.where(is_ctx, xc_ref[rows], xl_ref[rows]), store,
                sh_ref, sc_ref, gt_ref, ng_ref, wi_ref, wo_ref)


def _ffn_split_out_kernel(x_ref, sh_ref, sc_ref, gt_ref, ng_ref, wi_ref, wo_ref,
                          oc_ref, ol_ref):
    outs = {}

    def keep(rows, v):
        outs[rows.start] = v

    _ffn_passes(lambda rows: x_ref[rows], keep, sh_ref, sc_ref, gt_ref, ng_ref, wi_ref, wo_ref)
    is_ctx = pl.program_id(0) < M_CTX // FFN_TM

    @pl.when(is_ctx)
    def _():
        for rows in _FFN_PASSES:
            oc_ref[rows] = outs[rows.start]

    @pl.when(jnp.logical_not(is_ctx))
    def _():
        for rows in _FFN_PASSES:
            ol_ref[rows] = outs[rows.start]


def _mod_row_of_tile(i, rows_per_tile):
    n_ctx = M_CTX // rows_per_tile
    per_b = T_LAT // rows_per_tile
    return jnp.where(i < n_ctx, 0, 1 + jnp.maximum(i - n_ctx, 0) // per_b)


def _ffn(xs, mods, norm_g, wi, wo, l, f, split_out=False):
    j0 = 6 * f
    n_ctx = M_CTX // FFN_TM
    row = functools.partial(_mod_row_of_tile, rows_per_tile=FFN_TM)

    def mod_spec(j):
        return pl.BlockSpec((None, None, None, 1, D), lambda i: (l, j, row(i), 0, 0))

    ctx_rows = pl.BlockSpec((FFN_TM, D), lambda i: (jnp.minimum(i, n_ctx - 1), 0))
    lat_rows = pl.BlockSpec((FFN_TM, D), lambda i: (jnp.maximum(i - n_ctx, 0), 0))
    all_rows = pl.BlockSpec((FFN_TM, D), lambda i: (i, 0))
    split_in = isinstance(xs, tuple)
    assert not (split_in and split_out)
    if split_in:
        body, x_specs, xs = _ffn_split_in_kernel, [ctx_rows, lat_rows], list(xs)
    else:
        body, x_specs, xs = (_ffn_split_out_kernel if split_out else _ffn_kernel), [all_rows], [xs]
    if split_out:
        out_specs = [ctx_rows, lat_rows]
        out_shape = [jax.ShapeDtypeStruct((M_CTX, D), f32), jax.ShapeDtypeStruct((M_LAT, D), f32)]
    else:
        out_specs = all_rows
        out_shape = jax.ShapeDtypeStruct((M_ALL, D), f32)

    const = dict(pipeline_mode=pl.Buffered(1))
    return pl.pallas_call(
        body,
        grid=(M_ALL // FFN_TM,),
        in_specs=x_specs + [
            mod_spec(j0), mod_spec(j0 + 1), mod_spec(j0 + 2),
            pl.BlockSpec((None, None, 1, D), lambda i: (l, 2 * f, 0, 0)),
            pl.BlockSpec((None, None, D, 2 * D_FF), lambda i: (l, f, 0, 0), **const),
            pl.BlockSpec((None, None, D_FF, D), lambda i: (l, f, 0, 0), **const),
        ],
        out_specs=out_specs,
        out_shape=out_shape,
        compiler_params=pltpu.CompilerParams(
            dimension_semantics=("arbitrary",), vmem_limit_bytes=VMEM_LIMIT),
        name=f"ffn{f}",
    )(*xs, mods, mods, mods, norm_g, wi, wo)


def _scan8(a, b, h_init, reverse, emit):
    T, C = a.shape
    nb = T // 8
    row8 = lax.broadcasted_iota(jnp.int32, (8, C), 0)
    for s in (1, 2, 4):
        if reverse:
            a_r = pltpu.roll(a, T - s, 0)
            b_r = pltpu.roll(b, T - s, 0)
            edge = row8 < 8 - s
            a_s = jnp.concatenate([a_r[:T - 8], jnp.where(edge, a_r[T - 8:], 1.0)], axis=0)
            b_s = jnp.concatenate([b_r[:T - 8], jnp.where(edge, b_r[T - 8:], 0.0)], axis=0)
        else:
            a_r = pltpu.roll(a, s, 0)
            b_r = pltpu.roll(b, s, 0)
            edge = row8 >= s
            a_s = jnp.concatenate([jnp.where(edge, a_r[:8], 1.0), a_r[8:]], axis=0)
            b_s = jnp.concatenate([jnp.where(edge, b_r[:8], 0.0), b_r[8:]], axis=0)
        b = a * b_s + b
        a = a * a_s
        yield
    h = jnp.broadcast_to(h_init, (8, C))
    for n, v in enumerate(range(nb - 1, -1, -1) if reverse else range(nb)):
        h = a[v * 8:(v + 1) * 8] * h + b[v * 8:(v + 1) * 8]
        emit(v, h)
        if n % 4 == 3:
            yield
    return h[0:1] if reverse else h[7:8]


def _lru_group(h_ref, win_ref, cw_ref, cb_ref, wg_ref, bg_ref, lam_ref, h0, ycat_ref, st_ref,
               gate_sc):
    T = h_ref.shape[0]
    p = _dot(h_ref[...], win_ref[:, C_XA:C_XA + 2 * GW])
    xa = p[:, :GW]
    yield
    gate_sc[...] = _gelu_tanh(p[:, GW:])
    yield
    tpos = lax.broadcasted_iota(jnp.int32, (T, GW), 0)

    def shifted(k):
        if k > 0:
            return jnp.where(tpos >= k, pltpu.roll(xa, k, 0), 0.0)
        return jnp.where(tpos < T + k, pltpu.roll(xa, T + k, 0), 0.0)

    cw = cw_ref[...]
    xc = shifted(2) * cw[0:1] + shifted(1) * cw[1:2] + xa * cw[2:3] + shifted(-1) * cw[3:4]
    xc = xc + cb_ref[...]
    xcb = xc.astype(bf16)
    lam = lam_ref[...]
    yield
    for d in range(2):
        ds_ = slice(d * 2 * GW, (d + 1) * 2 * GW)
        pre = _dot(xcb, wg_ref[:, ds_]) + bg_ref[:, ds_]
        yield
        r = _sigmoid(pre[:, 0:GW])
        gi = _sigmoid(pre[:, GW:2 * GW])
        yield
        log_a = (-LRU_C * r) * _softplus(-lam[d:d + 1])
        a = jnp.exp(log_a)
        u = jnp.sqrt(1.0 - a * a) * (gi * xc)
        yield
        h_init = jnp.zeros((1, GW), f32) if h0 is None else h0[d:d + 1]

        def emit_fwd(v, hv):
            ycat_ref[v * 8:(v + 1) * 8, 0:GW] = hv

        def emit_bwd(v, hv):
            rows = slice(v * 8, (v + 1) * 8)
            ycat_ref[rows, 0:GW] = (ycat_ref[rows, 0:GW] + hv) * gate_sc[rows, :]

        last = yield from _scan8(a, u, h_init, d == 1, emit_bwd if d == 1 else emit_fwd)
        if st_ref is not None:
            st_ref[d:d + 1, :] = last
        yield


def _build_decay_tiles(dec_ref, dm_sc, nt):
    r = lax.broadcasted_iota(jnp.int32, (Q_TILE, Q_TILE), 0)
    c = lax.broadcasted_iota(jnp.int32, (Q_TILE, Q_TILE), 1)
    base = (r - c).astype(f32)
    for h in range(4):
        lgf = -_softplus(-dec_ref[h])
        lgb = -_softplus(-dec_ref[4 + h])
        for m in range(-(nt - 1), nt):
            diff = base + float(m * Q_TILE)
            if m > 0:
                mat = jnp.exp(lgf * diff)
            elif m < 0:
                mat = jnp.exp(lgb * (-diff))
            else:
                mat = (jnp.where(diff >= 0, jnp.exp(lgf * jnp.maximum(diff, 0.0)), 0.0)
                       + jnp.where(diff <= 0, jnp.exp(lgb * jnp.maximum(-diff, 0.0)), 0.0))
            dm_sc[h * (2 * nt - 1) + m + nt - 1] = mat


def _retention_group(h_ref, win_ref, dec_ref, gn_ref, s0_ref, ycat_ref, sret_ref,
                     q_sc, k_sc, v_sc, g_sc, dm_sc, tiles_out=None):
    T = h_ref.shape[0]
    nt = T // Q_TILE
    q_sc[...] = _dot(h_ref[...], win_ref[:, C_QD:C_QD + GW]).astype(bf16)
    yield
    kf = _dot(h_ref[...], win_ref[:, C_KD:C_KD + GW]) * ATT_SCALE
    k_sc[...] = kf.astype(bf16)
    yield
    v_sc[...] = _dot(h_ref[...], win_ref[:, C_VD:C_VD + GW]).astype(bf16)
    yield
    g_sc[...] = _dot(h_ref[...], win_ref[:, C_GD:C_GD + GW])
    gn = gn_ref[...]
    tcol = lax.broadcasted_iota(jnp.int32, (T, 1), 0).astype(f32)
    yield

    def head(h):
        hs = slice(h * HD, (h + 1) * HD)
        lgf = -_softplus(-dec_ref[h])
        lgb = -_softplus(-dec_ref[4 + h])
        kh = k_sc[:, hs]
        vh = v_sc[:, hs]
        if sret_ref is not None:
            zf = jnp.exp(lgf * ((T - 1.0) - tcol))
            zb = jnp.exp(lgb * tcol)
            kff = kf[:, hs]
            sret_ref[h * HD:(h + 1) * HD, :] = _dot_tn((kff * zf).astype(bf16), vh)
            sret_ref[(4 + h) * HD:(5 + h) * HD, :] = _dot_tn((kff * zb).astype(bf16), vh)
        if s0_ref is not None:
            s0f = s0_ref[h * HD:(h + 1) * HD, :].astype(bf16)
            s0b = s0_ref[(4 + h) * HD:(5 + h) * HD, :].astype(bf16)
        gnh = gn[:, hs]

        def tile(i):
            t0 = pl.multiple_of(i * Q_TILE, Q_TILE)
            rows = pl.ds(t0, Q_TILE)
            qh = q_sc[rows, hs]
            o = None
            for j in range(nt):
                ks = slice(j * Q_TILE, (j + 1) * Q_TILE)
                dm = dm_sc[h * (2 * nt - 1) + (nt - 1 - j) + i]
                sd = (_dot_nt(qh, kh[ks]) * dm).astype(bf16)
                yield
                oj = _dot(sd, vh[ks])
                o = oj if o is None else o + oj
                yield
            if s0_ref is not None:
                tq = (lax.broadcasted_iota(jnp.int32, (Q_TILE, 1), 0) + t0).astype(f32)
                o = o + _dot(qh, s0f) * jnp.exp(lgf * (tq + 1.0))
                o = o + _dot(qh, s0b) * jnp.exp(lgb * (T - tq))
            mu = jnp.mean(o, axis=-1, keepdims=True)
            oc = o - mu
            var = jnp.mean(oc * oc, axis=-1, keepdims=True)
            yield
            on = oc * lax.rsqrt(var + EPS) * gnh
            ycat_ref[rows, 3 * GW + h * HD:3 * GW + (h + 1) * HD] = on * _silu(g_sc[rows, hs])
            yield

        return tile

    if nt == 1:
        for h in range(4):
            tile = head(h)
            yield
            yield from tile(0)
    else:
        tiles_out.extend(head(h) for h in range(4))


N_CTX_IN = 18


def _ctx_mixer_kernel(*refs):
    n_out = 7
    n_scratch = 8
    ins = refs[:N_CTX_IN]
    outs = refs[len(refs) - n_out - n_scratch:]
    _ctx_mixer_body(*ins, *outs)


def _ctx_mixer_body(x_ref, sh_ref, sc_ref, gt_ref, ng_ref, win_ref, wout_ref,
                    cw_ref, cb_ref, wg_ref, bg_ref, lam_ref,
                    qnb_ref, knb_ref, qnc_ref, knc_ref, dec_ref, gn_ref,
                    o_ref, kb_ref, vb_ref, kc_ref, vc_ref, st_ref, sret_ref,
                    h_ref, ycat_ref, q_sc, k_sc, v_sc, g_sc, gate_sc, dm_sc):
    @pl.when(pl.program_id(0) == 0)
    def _():
        _build_decay_tiles(dec_ref, dm_sc, T_CTX // Q_TILE)

    h_ref[...] = _rms_mod(x_ref[...], ng_ref[...], sc_ref[...], sh_ref[...]).astype(bf16)

    _run(_interleave(
        _lru_group(h_ref, win_ref, cw_ref, cb_ref, wg_ref, bg_ref, lam_ref, None, ycat_ref,
                   st_ref, gate_sc),
        _ctx_attention_group(h_ref, win_ref, C_QB, 2, qnb_ref, knb_ref, kb_ref, vb_ref,
                             ycat_ref, GW),
        _ctx_attention_group(h_ref, win_ref, C_QC, 4, qnc_ref, knc_ref, kc_ref, vc_ref,
                             ycat_ref, 2 * GW),
        _retention_group(h_ref, win_ref, dec_ref, gn_ref, None, ycat_ref, sret_ref,
                         q_sc, k_sc, v_sc, g_sc, dm_sc),
    ))
    y = _dot(ycat_ref[...].astype(bf16), wout_ref[...])
    o_ref[...] = x_ref[...] + gt_ref[...] * y


def _ctx_attention_group(h_ref, win_ref, col, n_kv, qn_ref, kn_ref, k_out_ref, v_out_ref,
                         ycat_ref, ycol):
    kw = n_kv * HD
    p = _dot(h_ref[...], win_ref[:, col:col + GW + 2 * kw])
    yield
    qf = _headnorm(p[:, 0:GW], qn_ref[...] * ATT_SCALE)
    yield
    kn = _headnorm(p[:, GW:GW + kw], kn_ref[...])
    vf = p[:, GW + kw:GW + 2 * kw]
    k_out_ref[...] = kn
    v_out_ref[...] = vf
    k = kn.astype(bf16)
    v = vf.astype(bf16)
    if n_kv == 2:
        k_sw = pltpu.roll(kn, HD, 1).astype(bf16)
        v_sw = pltpu.roll(vf, HD, 1).astype(bf16)
    yield
    lo = lax.broadcasted_iota(jnp.int32, (p.shape[0], 128), 1) < HD
    pending = {}

    def head(hh):
        j, e = divmod(hh, 2)
        pair = slice(j * 128, (j + 1) * 128)
        if n_kv == 4:
            kk, vv = k[:, pair], v[:, pair]
        else:
            kk, vv = (k, v) if j == e else (k_sw, v_sw)

        def scores():
            qm = jnp.where(lo if e == 0 else jnp.logical_not(lo), qf[:, pair], 0.0).astype(bf16)
            return [_dot_nt(qm, kk)]

        def store(o):
            if e == 0:
                pending[j] = o
            else:
                ycat_ref[:, ycol + j * 128:ycol + (j + 1) * 128] = jnp.where(lo, pending[j], o)

        return _attend_stages(scores, [vv], store)

    yield from _interleave(*[head(hh) for hh in range(4)])


def _lat_mixer_kernel(x_ref, sh_ref, sc_ref, gt_ref, ng_ref, win_ref, wout_ref,
                      cw_ref, cb_ref, wg_ref, bg_ref, lam_ref,
                      qnb_ref, knb_ref, qnc_ref, knc_ref, dec_ref, gn_ref,
                      cos_ref, sin_ref, bias_ref,
                      cbk_ref, cbv_ref, cck_ref, ccv_ref, lru0_ref, ret0_ref,
                      o_ref,
                      h_ref, q_sc, k_sc, v_sc, g_sc, gate_sc, dm_sc,
                      qb_sc, kb_sc, vb_sc, qc_sc, kc_sc, vc_sc):
    T = T_LAT
    ycat_ref = o_ref

    @pl.when(pl.program_id(0) == 0)
    def _():
        _build_decay_tiles(dec_ref, dm_sc, T // Q_TILE)

    h_ref[...] = _rms_mod(x_ref[...], ng_ref[...], sc_ref[...], sh_ref[...]).astype(bf16)

    def prep_b():
        cos = cos_ref[...]
        sin = sin_ref[...]
        p = _dot(h_ref[...], win_ref[:, C_QB:C_QB + GW])
        yield
        qr = _rope(_headnorm(p, qnb_ref[...] * ATT_SCALE), cos, sin)
        for e in range(2):
            qb_sc[e] = _keep_half(qr, e).astype(bf16)
        yield
        p = _dot(h_ref[...], win_ref[:, C_KB:C_KB + GW])
        yield
        for dst, lat, ctx in ((kb_sc, _rope(_headnorm(p[:, 0:128], knb_ref[...]), cos, sin),
                               cbk_ref[...]),
                              (vb_sc, p[:, 128:GW], cbv_ref[...])):
            dst[0, 0:T, :] = lat.astype(bf16)
            dst[0, T:T + PAST, :] = ctx.astype(bf16)
            dst[1, 0:T, :] = pltpu.roll(lat, HD, 1).astype(bf16)
            dst[1, T:T + PAST, :] = pltpu.roll(ctx, HD, 1).astype(bf16)
        yield

    def prep_c():
        p = _dot(h_ref[...], win_ref[:, C_QC:C_QC + GW])
        yield
        qn = _headnorm(p, qnc_ref[...] * ATT_SCALE)
        for e in range(2):
            qc_sc[e] = _keep_half(qn, e).astype(bf16)
        yield
        p = _dot(h_ref[...], win_ref[:, C_KC:C_KC + GW])
        yield
        kc_sc[0:T, :] = _headnorm(p, knc_ref[...]).astype(bf16)
        kc_sc[T:T + PAST, :] = cck_ref[...].astype(bf16)
        yield
        vc_sc[0:T, :] = _dot(h_ref[...], win_ref[:, C_VC:C_VC + GW]).astype(bf16)
        vc_sc[T:T + PAST, :] = ccv_ref[...].astype(bf16)
        yield

    ret_tiles = []
    retention = _retention_group(h_ref, win_ref, dec_ref, gn_ref, ret0_ref, ycat_ref, None,
                                 q_sc, k_sc, v_sc, g_sc, dm_sc, ret_tiles)

    def preps():
        yield from prep_b()
        yield from prep_c()
        for _ in range(RET_PREP_STAGES):
            next(retention)
            yield

    _run(_interleave(
        _every(_lru_group(h_ref, win_ref, cw_ref, cb_ref, wg_ref, bg_ref, lam_ref, lru0_ref[...],
                          ycat_ref, None, gate_sc), LRU_STAGES_PER_ROUND),
        preps()))
    _run(retention)

    lo_t = lax.broadcasted_iota(jnp.int32, (Q_TILE, 128), 1) < HD
    lo_r = lax.broadcasted_iota(jnp.int32, (GRID_W, 128), 1) < HD
    kvk = [kb_sc[plane] for plane in range(2)]
    kvv = [vb_sc[plane] for plane in range(2)]

    def tile_b(i):
        rows = pl.ds(pl.multiple_of(i * Q_TILE, Q_TILE), Q_TILE)
        pending = {}

        def head(hh):
            j, e = divmod(hh, 2)
            pair = slice(j * 128, (j + 1) * 128)
            plane = 0 if j == e else 1

            def store(o):
                if e == 0:
                    pending[j] = o
                else:
                    ycat_ref[rows, GW + j * 128:GW + (j + 1) * 128] = jnp.where(lo_t, pending[j], o)

            return _attend_stages(lambda: [_dot_nt(qb_sc[e, rows, pair], kvk[plane])],
                                  [kvv[plane]], store)

        return _interleave(*[head(hh) for hh in range(4)])

    n_loc = NAT_WR * GRID_W
    qcol = lax.broadcasted_iota(jnp.int32, (GRID_W, n_loc), 0)
    kcol = lax.broadcasted_iota(jnp.int32, (GRID_W, n_loc), 1) & (GRID_W - 1)
    cstart = jnp.clip(qcol - NAT_WC // 2, 0, GRID_W - NAT_WC)
    in_win = (kcol >= cstart) & (kcol < cstart + NAT_WC)

    def row_block(r):
        rs = jnp.clip(r - NAT_WR // 2, 0, GRID_R - NAT_WR)
        didx = rs - r + (NAT_WR - 1)
        qrows = pl.ds(pl.multiple_of(r * GRID_W, GRID_W), GRID_W)
        krows = pl.ds(pl.multiple_of(rs * GRID_W, GRID_W), n_loc)

        pending = {}

        def head(hh):
            j, e = divmod(hh, 2)
            pair = slice(j * 128, (j + 1) * 128)

            def scores():
                qh = qc_sc[e, qrows, pair]
                s_loc = _dot_nt(qh, kc_sc[krows, pair])
                s_loc = jnp.where(in_win, s_loc + bias_ref[hh, didx], NEG_INF)
                return [s_loc, _dot_nt(qh, kc_sc[T:T + PAST, pair])]

            def store(o):
                if e == 0:
                    pending[j] = o
                else:
                    ycat_ref[qrows, 2 * GW + j * 128:2 * GW + (j + 1) * 128] = jnp.where(
                        lo_r, pending[j], o)

            return _attend_stages(scores, [vc_sc[krows, pair], vc_sc[T:T + PAST, pair]], store)

        return _interleave(*[head(hh) for hh in range(4)])

    rows_per_tile = Q_TILE // GRID_W

    def fused_tile(i):
        def b_then_d():
            yield from tile_b(i)
            yield from _interleave(*[tile(i) for tile in ret_tiles])

        def c_rows():
            for rb in range(rows_per_tile):
                yield from row_block(i * rows_per_tile + rb)

        _run(_interleave(b_then_d(), c_rows()))

    _loop(T // Q_TILE, fused_tile)

    y = _dot(ycat_ref[...].astype(bf16), wout_ref[...])
    o_ref[...] = x_ref[...] + gt_ref[...] * y


def _layer_specs(l, mod_row):
    const = dict(pipeline_mode=pl.Buffered(1))

    def lspec(shape):
        nd = len(shape)
        return pl.BlockSpec((None,) + shape, lambda i: (l,) + (0,) * nd, **const)

    def mod_spec(j):
        return pl.BlockSpec((None, None, None, 1, D), lambda i: (l, j, mod_row(i), 0, 0))

    return [
        mod_spec(3), mod_spec(4), mod_spec(5),
        pl.BlockSpec((None, None, 1, D), lambda i: (l, 1, 0, 0)),
        lspec((D, IN_COLS)), lspec((D, D)),
        lspec((4, GW)), lspec((1, GW)), lspec((GW, 4 * GW)), lspec((1, 4 * GW)), lspec((2, GW)),
        lspec((1, GW)), lspec((1, 128)), lspec((1, GW)), lspec((1, GW)),
        lspec((8, 1, 1)), lspec((1, GW)),
    ]


def _ctx_mixer(x, mods, norm_g, lw, l, prev):
    T = T_CTX

    def layer_spec(*tail):
        return pl.BlockSpec((None, None) + tail, lambda i: (i, l) + (0,) * len(tail))

    tails = [(T, 128), (T, 128), (T, GW), (T, GW), (2, GW), (8 * HD, HD)]
    operands = [x, mods, mods, mods, norm_g, lw["w_in"], lw["w_out"],
                lw["conv_w"], lw["conv_b"], lw["wg"], lw["bg"], lw["lam"],
                lw["qnb"], lw["knb"], lw["qnc"], lw["knc"], lw["dec"], lw["gn"]]
    assert len(operands) == N_CTX_IN
    aliases = {0: 0}
    prev_specs = []
    if prev is not None:
        prev_specs = [pl.BlockSpec(memory_space=pl.ANY)] * len(prev)
        aliases.update({N_CTX_IN + k: 1 + k for k in range(len(prev))})
        operands = operands + list(prev)

    outs = pl.pallas_call(
        _ctx_mixer_kernel,
        grid=(N_CTX,),
        in_specs=[pl.BlockSpec((T, D), lambda i: (i, 0))] + _layer_specs(l, lambda i: 0)
        + prev_specs,
        out_specs=[pl.BlockSpec((T, D), lambda i: (i, 0))] + [layer_spec(*t) for t in tails],
        out_shape=[jax.ShapeDtypeStruct((M_ALL, D), f32)]
        + [jax.ShapeDtypeStruct((N_CTX, DEPTH) + t, f32) for t in tails],
        scratch_shapes=[
            pltpu.VMEM((T, D), bf16), pltpu.VMEM((T, D), f32),
            pltpu.VMEM((T, GW), bf16), pltpu.VMEM((T, GW), bf16), pltpu.VMEM((T, GW), bf16),
            pltpu.VMEM((T, GW), f32), pltpu.VMEM((T, GW), f32),
            pltpu.VMEM((4, Q_TILE, Q_TILE), f32),
        ],
        input_output_aliases=aliases,
        compiler_params=pltpu.CompilerParams(
            dimension_semantics=("arbitrary",), vmem_limit_bytes=VMEM_LIMIT),
        name="ctx_mixer",
    )(*operands)
    return outs[0], tuple(outs[1:])


def _lat_mixer(x, mods, norm_g, lw, l, cos, sin, caches):
    T = T_LAT
    blk0 = M_CTX // T
    const = dict(pipeline_mode=pl.Buffered(1))
    cbk, cbv, cck, ccv, lru0, ret0 = caches

    def cache_spec(w):
        return pl.BlockSpec((None, None, PAST, w), lambda i: (i, l, 0, 0))

    return pl.pallas_call(
        _lat_mixer_kernel,
        grid=(N_LAT,),
        in_specs=[pl.BlockSpec((T, D), lambda i: (blk0 + i, 0), **const)]
        + _layer_specs(l, lambda i: 1 + i)
        + [
            pl.BlockSpec((T, 128), lambda i: (0, 0), **const),
            pl.BlockSpec((T, 128), lambda i: (0, 0), **const),
            pl.BlockSpec((None, 4, NAT_WR, GRID_W, NAT_WR * GRID_W),
                         lambda i: (l, 0, 0, 0, 0), **const),
            cache_spec(128), cache_spec(128), cache_spec(GW), cache_spec(GW),
            pl.BlockSpec((None, None, 2, GW), lambda i: (i, l, 0, 0)),
            pl.BlockSpec((None, None, 8 * HD, HD), lambda i: (i, l, 0, 0)),
        ],
        out_specs=pl.BlockSpec((T, D), lambda i: (blk0 + i, 0), **const),
        out_shape=jax.ShapeDtypeStruct((M_ALL, D), f32),
        scratch_shapes=[
            pltpu.VMEM((T, D), bf16),
            pltpu.VMEM((T, GW), bf16), pltpu.VMEM((T, GW), bf16), pltpu.VMEM((T, GW), bf16),
            pltpu.VMEM((T, GW), f32), pltpu.VMEM((T, GW), f32),
            pltpu.VMEM((4 * (2 * (T // Q_TILE) - 1), Q_TILE, Q_TILE), f32),
            pltpu.VMEM((2, T, GW), bf16),
            pltpu.VMEM((2, T + PAST, 128), bf16), pltpu.VMEM((2, T + PAST, 128), bf16),
            pltpu.VMEM((2, T, GW), bf16),
            pltpu.VMEM((T + PAST, GW), bf16), pltpu.VMEM((T + PAST, GW), bf16),
        ],
        input_output_aliases={0: 0},
        compiler_params=pltpu.CompilerParams(
            dimension_semantics=("arbitrary",), vmem_limit_bytes=VMEM_LIMIT),
        name="lat_mixer",
    )(x, mods, mods, mods, norm_g, lw["w_in"], lw["w_out"],
      lw["conv_w"], lw["conv_b"], lw["wg"], lw["bg"], lw["lam"],
      lw["qnb"], lw["knb"], lw["qnc"], lw["knc"], lw["dec"], lw["gn"],
      cos, sin, lw["nat_bias"], cbk, cbv, cck, ccv, lru0, ret0)


def _block_diag(w):
    eye = jnp.eye(4, dtype=w.dtype)
    return jnp.einsum("lncd,nm->lncmd", w, eye).reshape(DEPTH, GW, GW)


def _rope_tables():
    t = jnp.arange(T_LAT)
    row = (t // GRID_W).astype(f32)
    col = (t % GRID_W).astype(f32)
    n_freq = HD // 4
    inv = ROPE_BASE ** (-jnp.arange(n_freq, dtype=f32) / n_freq)
    ang = jnp.concatenate([row[:, None] * inv, col[:, None] * inv], axis=-1)
    cos = jnp.repeat(jnp.cos(ang), 2, axis=-1)
    sin = jnp.repeat(jnp.sin(ang), 2, axis=-1)
    sign = jnp.where(jnp.arange(HD) % 2 == 0, -1.0, 1.0).astype(f32)
    return jnp.tile(cos, (1, 2)), jnp.tile(sin * sign, (1, 2))


def _nat_bias_windows(nat_bias):
    pad = GRID_W - NAT_WC
    ext = jnp.concatenate([jnp.repeat(nat_bias[..., :1], pad, -1), nat_bias,
                           jnp.repeat(nat_bias[..., -1:], pad + 1, -1)], axis=-1)
    t = jnp.tile(ext, (1, 1, 1, GRID_W))[..., :GRID_W * 127]
    t = t.reshape(DEPTH, 4, 2 * NAT_WR - 1, GRID_W, 127)[..., GRID_W - 1:]
    tab = jnp.stack([t[:, :, d:d + NAT_WR] for d in range(NAT_WR)], axis=2)
    tab = tab.transpose(0, 1, 2, 4, 3, 5)
    return tab.reshape(DEPTH, 4, NAT_WR, GRID_W, NAT_WR * GRID_W)


def kernel(x_prompt, x_sample, cache_b_k, cache_b_v, cache_c_k, cache_c_v, state_lru, state_ret, c, c_ctx, w_mod, b_mod, norm_g, ffn_w_in, ffn_w_out, w_in, w_out, conv_w, conv_b, lru_w_r, lru_b_r, lru_w_i, lru_b_i, lru_lambda, gqa_qn, gqa_kn, nat_qn, nat_kn, nat_bias, ret_decay, ret_gn):
    cvec = jnp.concatenate([c_ctx[None, :], c, jnp.zeros((8 - 1 - N_LAT, D), f32)], axis=0)
    mods = _modulation(cvec, w_mod, b_mod).reshape(DEPTH, N_MOD, 8, 1, D)
    ng = norm_g.reshape(DEPTH, 3, 1, D)

    wi = ffn_w_in.astype(bf16)
    wo = ffn_w_out.astype(bf16)
    lw = dict(
        w_in=w_in.astype(bf16),
        w_out=w_out.astype(bf16),
        conv_w=conv_w,
        conv_b=conv_b.reshape(DEPTH, 1, GW),
        wg=jnp.concatenate([_block_diag(lru_w_r[:, 0]), _block_diag(lru_w_i[:, 0]),
                            _block_diag(lru_w_r[:, 1]), _block_diag(lru_w_i[:, 1])],
                           axis=-1).astype(bf16),
        bg=jnp.concatenate([lru_b_r[:, 0], lru_b_i[:, 0], lru_b_r[:, 1], lru_b_i[:, 1]],
                           axis=-1).reshape(DEPTH, 1, 4 * GW),
        lam=lru_lambda,
        qnb=jnp.tile(gqa_qn, (1, 4)).reshape(DEPTH, 1, GW),
        knb=jnp.tile(gqa_kn, (1, 2)).reshape(DEPTH, 1, 128),
        qnc=jnp.tile(nat_qn, (1, 4)).reshape(DEPTH, 1, GW),
        knc=jnp.tile(nat_kn, (1, 4)).reshape(DEPTH, 1, GW),
        dec=ret_decay.reshape(DEPTH, 8, 1, 1),
        gn=ret_gn.reshape(DEPTH, 1, GW),
        nat_bias=_nat_bias_windows(nat_bias),
    )
    cos, sin = _rope_tables()
    caches = (cache_b_k.reshape(N_LAT, DEPTH, PAST, 128), cache_b_v.reshape(N_LAT, DEPTH, PAST, 128),
              cache_c_k.reshape(N_LAT, DEPTH, PAST, GW), cache_c_v.reshape(N_LAT, DEPTH, PAST, GW),
              state_lru, state_ret.reshape(N_LAT, DEPTH, 8 * HD, HD))

    x = (x_prompt.reshape(M_CTX, D), x_sample.reshape(M_LAT, D))
    new = None
    for l in range(DEPTH):
        x = _ffn(x, mods, ng, wi, wo, l, 0)
        x, new = _ctx_mixer(x, mods, ng, lw, l, new)
        x = _lat_mixer(x, mods, ng, lw, l, cos, sin, caches)
        x = _ffn(x, mods, ng, wi, wo, l, 1, split_out=(l == DEPTH - 1))

    new_bk, new_bv, new_ck, new_cv, new_lru, new_ret = new
    return (x[0].reshape(N_CTX, T_CTX, D), x[1].reshape(N_LAT, T_LAT, D),
            new_bk.reshape(N_CTX, DEPTH, T_CTX, 2, HD), new_bv.reshape(N_CTX, DEPTH, T_CTX, 2, HD),
            new_ck.reshape(N_CTX, DEPTH, T_CTX, 4, HD), new_cv.reshape(N_CTX, DEPTH, T_CTX, 4, HD),
            new_lru, new_ret.reshape(N_CTX, DEPTH, 2, 4, HD, HD))
```

```python
import functools
import math

import jax
import jax.numpy as jnp
from jax import lax
from jax.experimental import pallas as pl
from jax.experimental.pallas import tpu as pltpu

f32 = jnp.float32
bf16 = jnp.bfloat16

D = 1024
DEPTH = 4
N_CTX, T_CTX = 32, 256
N_LAT, T_LAT = 4, 1024
M_CTX = N_CTX * T_CTX
M_LAT = N_LAT * T_LAT
M_ALL = M_CTX + M_LAT
PAST = 256
GRID_W = 64
GRID_R = T_LAT // GRID_W
HD = 64
GW = 256
LRU_C = 8.0
NAT_WR, NAT_WC = 8, 16
D_FF = 2816
IN_COLS = 2816
EPS = 1e-6
NEG_INF = -1e30
N_MOD = 9
ATT_SCALE = HD ** -0.5
ROPE_BASE = 10000.0

C_XA, C_GA = 0, 256
C_QB, C_KB, C_VB = 512, 768, 896
C_QC, C_KC, C_VC = 1024, 1280, 1536
C_QD, C_KD, C_VD, C_GD = 1792, 2048, 2304, 2560

FFN_TM = 1024
FFN_SUB = 512
FFN_CHUNKS = ((0, 768), (768, 1536), (1536, 2304), (2304, 2816))
Q_TILE = 256
LRU_STAGES_PER_ROUND = 3
RET_PREP_STAGES = 4
VMEM_LIMIT = 60 * 1024 * 1024


def _dot(a, b):
    return jnp.dot(a, b, preferred_element_type=f32)


def _dot_nt(a, b):
    return lax.dot_general(a, b, (((1,), (1,)), ((), ())), preferred_element_type=f32)


def _dot_tn(a, b):
    return lax.dot_general(a, b, (((0,), (0,)), ((), ())), preferred_element_type=f32)


def _softplus(z):
    return jnp.maximum(z, 0.0) + jnp.log1p(jnp.exp(-jnp.abs(z)))


def _sigmoid(z):
    return 1.0 / (1.0 + jnp.exp(-z))


def _silu(z):
    return z * _sigmoid(z)


def _gelu_tanh(z):
    c = math.sqrt(2.0 / math.pi)
    return z * (0.5 * (1.0 + jnp.tanh(c * (z + 0.044715 * (z * z * z)))))


def _rms_mod(x, ng, sc, sh):
    ms = jnp.mean(x * x, axis=-1, keepdims=True)
    y = x * lax.rsqrt(ms + EPS) * ng
    return y * (1.0 + sc) + sh


def _headnorm(x, gain):
    T, W = x.shape
    lane = lax.broadcasted_iota(jnp.int32, (T, 128), 1)
    lo = lane < HD
    cols = []
    for c in range(W // 128):
        xc = x[:, c * 128:(c + 1) * 128]
        x2 = xc * xc
        s_lo = jnp.sum(jnp.where(lo, x2, 0.0), axis=-1, keepdims=True)
        s_hi = jnp.sum(jnp.where(lo, 0.0, x2), axis=-1, keepdims=True)
        r_lo = lax.rsqrt(s_lo * (1.0 / HD) + EPS)
        r_hi = lax.rsqrt(s_hi * (1.0 / HD) + EPS)
        cols.append(xc * jnp.where(lo, r_lo, r_hi))
    y = cols[0] if len(cols) == 1 else jnp.concatenate(cols, axis=-1)
    return y * gain


def _keep_half(x, e):
    T, W = x.shape
    lo = lax.broadcasted_iota(jnp.int32, (T, 128), 1) < HD
    keep = lo if e == 0 else jnp.logical_not(lo)
    cols = [jnp.where(keep, x[:, c * 128:(c + 1) * 128], 0.0) for c in range(W // 128)]
    return cols[0] if len(cols) == 1 else jnp.concatenate(cols, axis=-1)


def _rope(x, cos, sin_signed):
    T, W = x.shape
    lane = lax.broadcasted_iota(jnp.int32, (T, 128), 1)
    even = (lane & 1) == 0
    cols = []
    for c in range(W // 128):
        xc = x[:, c * 128:(c + 1) * 128]
        sw = jnp.where(even, pltpu.roll(xc, 127, 1), pltpu.roll(xc, 1, 1))
        cols.append(xc * cos + sw * sin_signed)
    return cols[0] if len(cols) == 1 else jnp.concatenate(cols, axis=-1)


def _attend_stages(scores, v_list, store):
    s_list = scores()
    yield
    m = s_list[0].max(axis=-1, keepdims=True)
    for s in s_list[1:]:
        m = jnp.maximum(m, s.max(axis=-1, keepdims=True))
    yield
    p_list = [jnp.exp(s - m) for s in s_list]
    l = None
    for p in p_list:
        ls = jnp.sum(p, axis=-1, keepdims=True)
        l = ls if l is None else l + ls
    yield
    o = None
    for p, v in zip(p_list, v_list):
        os_ = _dot(p.astype(bf16), v)
        o = os_ if o is None else o + os_
    store(o * (1.0 / l))
    yield


def _interleave(*stages):
    live = list(stages)
    while live:
        for g in list(live):
            try:
                next(g)
            except StopIteration:
                live.remove(g)
        yield


def _every(stages, n):
    while True:
        for _ in range(n):
            try:
                next(stages)
            except StopIteration:
                return
        yield


def _delayed(stages, n):
    for _ in range(n):
        yield
    yield from stages


def _run(stages):
    for _ in stages:
        pass


def _loop(n, body):
    def step(i, carry):
        body(i)
        return carry
    lax.fori_loop(0, n, step, 0)


def _mod_kernel(c_ref, w_ref, b_ref, o_ref):
    cs = _silu(c_ref[...]).astype(bf16)
    o_ref[...] = _dot(cs, w_ref[...].astype(bf16)) + b_ref[...]


def _modulation(cvec, w_mod, b_mod):
    return pl.pallas_call(
        _mod_kernel,
        grid=(DEPTH, N_MOD),
        in_specs=[
            pl.BlockSpec((8, D), lambda l, j: (0, 0)),
            pl.BlockSpec((None, D, D), lambda l, j: (l, 0, j)),
            pl.BlockSpec((None, None, 1, D), lambda l, j: (l, j, 0, 0)),
        ],
        out_specs=pl.BlockSpec((None, None, 8, D), lambda l, j: (l, j, 0, 0)),
        out_shape=jax.ShapeDtypeStruct((DEPTH, N_MOD, 8, D), f32),
        compiler_params=pltpu.CompilerParams(
            dimension_semantics=("parallel", "parallel"), vmem_limit_bytes=VMEM_LIMIT),
        name="adaln_mod",
    )(cvec, w_mod, b_mod.reshape(DEPTH, N_MOD, 1, D))


def _ffn_rows(load, store, sh_ref, sc_ref, gt_ref, ng_ref, wi_ref, wo_ref):
    x = load()
    h = _rms_mod(x, ng_ref[...], sc_ref[...], sh_ref[...]).astype(bf16)
    yield
    acc = None
    for c0, c1 in FFN_CHUNKS:
        a = _dot(h, wi_ref[:, c0:c1])
        b = _dot(h, wi_ref[:, D_FF + c0:D_FF + c1])
        g = (_silu(a) * b).astype(bf16)
        y = _dot(g, wo_ref[c0:c1, :])
        acc = y if acc is None else acc + y
        yield
    store(x + (0.5 * gt_ref[...]) * acc)


_FFN_PASSES = tuple(slice(r, r + FFN_SUB) for r in range(0, FFN_TM, FFN_SUB))


def _ffn_passes(load, store, *refs):
    _run(_interleave(*[
        _delayed(_ffn_rows(functools.partial(load, rows), functools.partial(store, rows), *refs), k)
        for k, rows in enumerate(_FFN_PASSES)]))


def _ffn_kernel(x_ref, sh_ref, sc_ref, gt_ref, ng_ref, wi_ref, wo_ref, o_ref):
    def store(rows, v):
        o_ref[rows] = v

    _ffn_passes(lambda rows: x_ref[rows], store, sh_ref, sc_ref, gt_ref, ng_ref, wi_ref, wo_ref)


def _ffn_split_in_kernel(xc_ref, xl_ref, sh_ref, sc_ref, gt_ref, ng_ref, wi_ref, wo_ref, o_ref):
    is_ctx = pl.program_id(0) < M_CTX // FFN_TM

    def store(rows, v):
        o_ref[rows] = v

    _ffn_passes(lambda rows: jnp.where(is_ctx, xc_ref[rows], xl_ref[rows]), store,
                sh_ref, sc_ref, gt_ref, ng_ref, wi_ref, wo_ref)


def _ffn_split_out_kernel(x_ref, sh_ref, sc_ref, gt_ref, ng_ref, wi_ref, wo_ref,
                          oc_ref, ol_ref):
    outs = {}

    def keep(rows, v):
        outs[rows.start] = v

    _ffn_passes(lambda rows: x_ref[rows], keep, sh_ref, sc_ref, gt_ref, ng_ref, wi_ref, wo_ref)
    is_ctx = pl.program_id(0) < M_CTX // FFN_TM

    @pl.when(is_ctx)
    def _():
        for rows in _FFN_PASSES:
            oc_ref[rows] = outs[rows.start]

    @pl.when(jnp.logical_not(is_ctx))
    def _():
        for rows in _FFN_PASSES:
            ol_ref[rows] = outs[rows.start]


def _mod_row_of_tile(i, rows_per_tile):
    n_ctx = M_CTX // rows_per_tile
    per_b = T_LAT // rows_per_tile
    return jnp.where(i < n_ctx, 0, 1 + jnp.maximum(i - n_ctx, 0) // per_b)


def _ffn(xs, mods, norm_g, wi, wo, l, f, split_out=False):
    j0 = 6 * f
    n_ctx = M_CTX // FFN_TM
    row = functools.partial(_mod_row_of_tile, rows_per_tile=FFN_TM)

    def mod_spec(j):
        return pl.BlockSpec((None, None, None, 1, D), lambda i: (l, j, row(i), 0, 0))

    ctx_rows = pl.BlockSpec((FFN_TM, D), lambda i: (jnp.minimum(i, n_ctx - 1), 0))
    lat_rows = pl.BlockSpec((FFN_TM, D), lambda i: (jnp.maximum(i - n_ctx, 0), 0))
    all_rows = pl.BlockSpec((FFN_TM, D), lambda i: (i, 0))
    split_in = isinstance(xs, tuple)
    assert not (split_in and split_out)
    if split_in:
        body, x_specs, xs = _ffn_split_in_kernel, [ctx_rows, lat_rows], list(xs)
    else:
        body, x_specs, xs = (_ffn_split_out_kernel if split_out else _ffn_kernel), [all_rows], [xs]
    if split_out:
        out_specs = [ctx_rows, lat_rows]
        out_shape = [jax.ShapeDtypeStruct((M_CTX, D), f32), jax.ShapeDtypeStruct((M_LAT, D), f32)]
    else:
        out_specs = all_rows
        out_shape = jax.ShapeDtypeStruct((M_ALL, D), f32)

    const = dict(pipeline_mode=pl.Buffered(1))
    return pl.pallas_call(
        body,
        grid=(M_ALL // FFN_TM,),
        in_specs=x_specs + [
            mod_spec(j0), mod_spec(j0 + 1), mod_spec(j0 + 2),
            pl.BlockSpec((None, None, 1, D), lambda i: (l, 2 * f, 0, 0)),
            pl.BlockSpec((None, None, D, 2 * D_FF), lambda i: (l, f, 0, 0), **const),
            pl.BlockSpec((None, None, D_FF, D), lambda i: (l, f, 0, 0), **const),
        ],
        out_specs=out_specs,
        out_shape=out_shape,
        compiler_params=pltpu.CompilerParams(
            dimension_semantics=("arbitrary",), vmem_limit_bytes=VMEM_LIMIT),
        name=f"ffn{f}",
    )(*xs, mods, mods, mods, norm_g, wi, wo)


def _scan8(a, b, h_init, reverse, emit):
    T, C = a.shape
    nb = T // 8
    row8 = lax.broadcasted_iota(jnp.int32, (8, C), 0)
    for s in (1, 2, 4):
        if reverse:
            a_r = pltpu.roll(a, T - s, 0)
            b_r = pltpu.roll(b, T - s, 0)
            edge = row8 < 8 - s
            a_s = jnp.concatenate([a_r[:T - 8], jnp.where(edge, a_r[T - 8:], 1.0)], axis=0)
            b_s = jnp.concatenate([b_r[:T - 8], jnp.where(edge, b_r[T - 8:], 0.0)], axis=0)
        else:
            a_r = pltpu.roll(a, s, 0)
            b_r = pltpu.roll(b, s, 0)
            edge = row8 >= s
            a_s = jnp.concatenate([jnp.where(edge, a_r[:8], 1.0), a_r[8:]], axis=0)
            b_s = jnp.concatenate([jnp.where(edge, b_r[:8], 0.0), b_r[8:]], axis=0)
        b = a * b_s + b
        a = a * a_s
        yield
    h = jnp.broadcast_to(h_init, (8, C))
    for n, v in enumerate(range(nb - 1, -1, -1) if reverse else range(nb)):
        h = a[v * 8:(v + 1) * 8] * h + b[v * 8:(v + 1) * 8]
        emit(v, h)
        if n % 4 == 3:
            yield
    return h[0:1] if reverse else h[7:8]


def _lru_group(h_ref, win_ref, cw_ref, cb_ref, wg_ref, bg_ref, lam_ref, h0, ycat_ref, st_ref,
               gate_sc):
    T = h_ref.shape[0]
    p = _dot(h_ref[...], win_ref[:, C_XA:C_XA + 2 * GW])
    xa = p[:, :GW]
    yield
    gate_sc[...] = _gelu_tanh(p[:, GW:])
    yield
    tpos = lax.broadcasted_iota(jnp.int32, (T, GW), 0)

    def shifted(k):
        if k > 0:
            return jnp.where(tpos >= k, pltpu.roll(xa, k, 0), 0.0)
        return jnp.where(tpos < T + k, pltpu.roll(xa, T + k, 0), 0.0)

    cw = cw_ref[...]
    xc = shifted(2) * cw[0:1] + shifted(1) * cw[1:2] + xa * cw[2:3] + shifted(-1) * cw[3:4]
    xc = xc + cb_ref[...]
    xcb = xc.astype(bf16)
    lam = lam_ref[...]
    yield
    for d in range(2):
        ds_ = slice(d * 2 * GW, (d + 1) * 2 * GW)
        pre = _dot(xcb, wg_ref[:, ds_]) + bg_ref[:, ds_]
        yield
        r = _sigmoid(pre[:, 0:GW])
        gi = _sigmoid(pre[:, GW:2 * GW])
        yield
        log_a = (-LRU_C * r) * _softplus(-lam[d:d + 1])
        a = jnp.exp(log_a)
        u = jnp.sqrt(1.0 - a * a) * (gi * xc)
        yield
        h_init = jnp.zeros((1, GW), f32) if h0 is None else h0[d:d + 1]

        def emit_fwd(v, hv):
            ycat_ref[v * 8:(v + 1) * 8, 0:GW] = hv

        def emit_bwd(v, hv):
            rows = slice(v * 8, (v + 1) * 8)
            ycat_ref[rows, 0:GW] = (ycat_ref[rows, 0:GW] + hv) * gate_sc[rows, :]

        last = yield from _scan8(a, u, h_init, d == 1, emit_bwd if d == 1 else emit_fwd)
        if st_ref is not None:
            st_ref[d:d + 1, :] = last
        yield


def _build_decay_tiles(dec_ref, dm_sc, nt):
    r = lax.broadcasted_iota(jnp.int32, (Q_TILE, Q_TILE), 0)
    c = lax.broadcasted_iota(jnp.int32, (Q_TILE, Q_TILE), 1)
    base = (r - c).astype(f32)
    for h in range(4):
        lgf = -_softplus(-dec_ref[h])
        lgb = -_softplus(-dec_ref[4 + h])
        for m in range(-(nt - 1), nt):
            diff = base + float(m * Q_TILE)
            if m > 0:
                mat = jnp.exp(lgf * diff)
            elif m < 0:
                mat = jnp.exp(lgb * (-diff))
            else:
                mat = (jnp.where(diff >= 0, jnp.exp(lgf * jnp.maximum(diff, 0.0)), 0.0)
                       + jnp.where(diff <= 0, jnp.exp(lgb * jnp.maximum(-diff, 0.0)), 0.0))
            dm_sc[h * (2 * nt - 1) + m + nt - 1] = mat


def _retention_group(h_ref, win_ref, dec_ref, gn_ref, s0_ref, ycat_ref, sret_ref,
                     q_sc, k_sc, v_sc, g_sc, dm_sc, tiles_out=None):
    T = h_ref.shape[0]
    nt = T // Q_TILE
    q_sc[...] = _dot(h_ref[...], win_ref[:, C_QD:C_QD + GW]).astype(bf16)
    yield
    kf = _dot(h_ref[...], win_ref[:, C_KD:C_KD + GW]) * ATT_SCALE
    k_sc[...] = kf.astype(bf16)
    yield
    v_sc[...] = _dot(h_ref[...], win_ref[:, C_VD:C_VD + GW]).astype(bf16)
    yield
    g_sc[...] = _dot(h_ref[...], win_ref[:, C_GD:C_GD + GW])
    gn = gn_ref[...]
    tcol = lax.broadcasted_iota(jnp.int32, (T, 1), 0).astype(f32)
    yield

    def head(h):
        hs = slice(h * HD, (h + 1) * HD)
        lgf = -_softplus(-dec_ref[h])
        lgb = -_softplus(-dec_ref[4 + h])
        kh = k_sc[:, hs]
        vh = v_sc[:, hs]
        if sret_ref is not None:
            zf = jnp.exp(lgf * ((T - 1.0) - tcol))
            zb = jnp.exp(lgb * tcol)
            kff = kf[:, hs]
            sret_ref[h * HD:(h + 1) * HD, :] = _dot_tn((kff * zf).astype(bf16), vh)
            sret_ref[(4 + h) * HD:(5 + h) * HD, :] = _dot_tn((kff * zb).astype(bf16), vh)
        if s0_ref is not None:
            s0f = s0_ref[h * HD:(h + 1) * HD, :].astype(bf16)
            s0b = s0_ref[(4 + h) * HD:(5 + h) * HD, :].astype(bf16)
        gnh = gn[:, hs]

        def tile(i):
            t0 = pl.multiple_of(i * Q_TILE, Q_TILE)
            rows = pl.ds(t0, Q_TILE)
            qh = q_sc[rows, hs]
            o = None
            for j in range(nt):
                ks = slice(j * Q_TILE, (j + 1) * Q_TILE)
                dm = dm_sc[h * (2 * nt - 1) + (nt - 1 - j) + i]
                sd = (_dot_nt(qh, kh[ks]) * dm).astype(bf16)
                yield
                oj = _dot(sd, vh[ks])
                o = oj if o is None else o + oj
                yield
            if s0_ref is not None:
                tq = (lax.broadcasted_iota(jnp.int32, (Q_TILE, 1), 0) + t0).astype(f32)
                o = o + _dot(qh, s0f) * jnp.exp(lgf * (tq + 1.0))
                o = o + _dot(qh, s0b) * jnp.exp(lgb * (T - tq))
            mu = jnp.mean(o, axis=-1, keepdims=True)
            oc = o - mu
            var = jnp.mean(oc * oc, axis=-1, keepdims=True)
            yield
            on = oc * lax.rsqrt(var + EPS) * gnh
            ycat_ref[rows, 3 * GW + h * HD:3 * GW + (h + 1) * HD] = on * _silu(g_sc[rows, hs])
            yield

        return tile

    if nt == 1:
        for h in range(4):
            tile = head(h)
            yield
            yield from tile(0)
    else:
        tiles_out.extend(head(h) for h in range(4))


N_CTX_IN = 18


def _ctx_mixer_kernel(*refs):
    n_out = 7
    n_scratch = 8
    ins = refs[:N_CTX_IN]
    outs = refs[len(refs) - n_out - n_scratch:]
    _ctx_mixer_body(*ins, *outs)


def _ctx_mixer_body(x_ref, sh_ref, sc_ref, gt_ref, ng_ref, win_ref, wout_ref,
                    cw_ref, cb_ref, wg_ref, bg_ref, lam_ref,
                    qnb_ref, knb_ref, qnc_ref, knc_ref, dec_ref, gn_ref,
                    o_ref, kb_ref, vb_ref, kc_ref, vc_ref, st_ref, sret_ref,
                    h_ref, ycat_ref, q_sc, k_sc, v_sc, g_sc, gate_sc, dm_sc):
    @pl.when(pl.program_id(0) == 0)
    def _():
        _build_decay_tiles(dec_ref, dm_sc, T_CTX // Q_TILE)

    h_ref[...] = _rms_mod(x_ref[...], ng_ref[...], sc_ref[...], sh_ref[...]).astype(bf16)

    _run(_interleave(
        _lru_group(h_ref, win_ref, cw_ref, cb_ref, wg_ref, bg_ref, lam_ref, None, ycat_ref,
                   st_ref, gate_sc),
        _ctx_attention_group(h_ref, win_ref, C_QB, 2, qnb_ref, knb_ref, kb_ref, vb_ref,
                             ycat_ref, GW),
        _ctx_attention_group(h_ref, win_ref, C_QC, 4, qnc_ref, knc_ref, kc_ref, vc_ref,
                             ycat_ref, 2 * GW),
        _retention_group(h_ref, win_ref, dec_ref, gn_ref, None, ycat_ref, sret_ref,
                         q_sc, k_sc, v_sc, g_sc, dm_sc),
    ))
    y = _dot(ycat_ref[...].astype(bf16), wout_ref[...])
    o_ref[...] = x_ref[...] + gt_ref[...] * y


def _ctx_attention_group(h_ref, win_ref, col, n_kv, qn_ref, kn_ref, k_out_ref, v_out_ref,
                         ycat_ref, ycol):
    kw = n_kv * HD
    p = _dot(h_ref[...], win_ref[:, col:col + GW + 2 * kw])
    yield
    qf = _headnorm(p[:, 0:GW], qn_ref[...] * ATT_SCALE)
    yield
    kn = _headnorm(p[:, GW:GW + kw], kn_ref[...])
    vf = p[:, GW + kw:GW + 2 * kw]
    k_out_ref[...] = kn
    v_out_ref[...] = vf
    k = kn.astype(bf16)
    v = vf.astype(bf16)
    if n_kv == 2:
        k_sw = pltpu.roll(kn, HD, 1).astype(bf16)
        v_sw = pltpu.roll(vf, HD, 1).astype(bf16)
    yield
    lo = lax.broadcasted_iota(jnp.int32, (p.shape[0], 128), 1) < HD
    pending = {}

    def head(hh):
        j, e = divmod(hh, 2)
        pair = slice(j * 128, (j + 1) * 128)
        if n_kv == 4:
            kk, vv = k[:, pair], v[:, pair]
        else:
            kk, vv = (k, v) if j == e else (k_sw, v_sw)

        def scores():
            qm = jnp.where(lo if e == 0 else jnp.logical_not(lo), qf[:, pair], 0.0).astype(bf16)
            return [_dot_nt(qm, kk)]

        def store(o):
            if e == 0:
                pending[j] = o
            else:
                ycat_ref[:, ycol + j * 128:ycol + (j + 1) * 128] = jnp.where(lo, pending[j], o)

        return _attend_stages(scores, [vv], store)

    yield from _interleave(*[head(hh) for hh in range(4)])


def _lat_mixer_kernel(x_ref, sh_ref, sc_ref, gt_ref, ng_ref, win_ref, wout_ref,
                      cw_ref, cb_ref, wg_ref, bg_ref, lam_ref,
                      qnb_ref, knb_ref, qnc_ref, knc_ref, dec_ref, gn_ref,
                      cos_ref, sin_ref, bias_ref,
                      cbk_ref, cbv_ref, cck_ref, ccv_ref, lru0_ref, ret0_ref,
                      o_ref,
                      h_ref, q_sc, k_sc, v_sc, g_sc, gate_sc, dm_sc,
                      qb_sc, kb_sc, vb_sc, qc_sc, kc_sc, vc_sc):
    T = T_LAT
    ycat_ref = o_ref

    @pl.when(pl.program_id(0) == 0)
    def _():
        _build_decay_tiles(dec_ref, dm_sc, T // Q_TILE)

    h_ref[...] = _rms_mod(x_ref[...], ng_ref[...], sc_ref[...], sh_ref[...]).astype(bf16)

    def prep_b():
        cos = cos_ref[...]
        sin = sin_ref[...]
        p = _dot(h_ref[...], win_ref[:, C_QB:C_QB + GW])
        yield
        qr = _rope(_headnorm(p, qnb_ref[...] * ATT_SCALE), cos, sin)
        for e in range(2):
            qb_sc[e] = _keep_half(qr, e).astype(bf16)
        yield
        p = _dot(h_ref[...], win_ref[:, C_KB:C_KB + GW])
        yield
        for dst, lat, ctx in ((kb_sc, _rope(_headnorm(p[:, 0:128], knb_ref[...]), cos, sin),
                               cbk_ref[...]),
                              (vb_sc, p[:, 128:GW], cbv_ref[...])):
            dst[0, 0:T, :] = lat.astype(bf16)
            dst[0, T:T + PAST, :] = ctx.astype(bf16)
            dst[1, 0:T, :] = pltpu.roll(lat, HD, 1).astype(bf16)
            dst[1, T:T + PAST, :] = pltpu.roll(ctx, HD, 1).astype(bf16)
        yield

    def prep_c():
        p = _dot(h_ref[...], win_ref[:, C_QC:C_QC + GW])
        yield
        qn = _headnorm(p, qnc_ref[...] * ATT_SCALE)
        for e in range(2):
            qc_sc[e] = _keep_half(qn, e).astype(bf16)
        yield
        p = _dot(h_ref[...], win_ref[:, C_KC:C_KC + GW])
        yield
        kc_sc[0:T, :] = _headnorm(p, knc_ref[...]).astype(bf16)
        kc_sc[T:T + PAST, :] = cck_ref[...].astype(bf16)
        yield
        vc_sc[0:T, :] = _dot(h_ref[...], win_ref[:, C_VC:C_VC + GW]).astype(bf16)
        vc_sc[T:T + PAST, :] = ccv_ref[...].astype(bf16)
        yield

    ret_tiles = []
    retention = _retention_group(h_ref, win_ref, dec_ref, gn_ref, ret0_ref, ycat_ref, None,
                                 q_sc, k_sc, v_sc, g_sc, dm_sc, ret_tiles)

    def preps():
        yield from prep_b()
        yield from prep_c()
        for _ in range(RET_PREP_STAGES):
            next(retention)
            yield

    _run(_interleave(
        _every(_lru_group(h_ref, win_ref, cw_ref, cb_ref, wg_ref, bg_ref, lam_ref, lru0_ref[...],
                          ycat_ref, None, gate_sc), LRU_STAGES_PER_ROUND),
        preps()))
    _run(retention)

    lo_t = lax.broadcasted_iota(jnp.int32, (Q_TILE, 128), 1) < HD
    lo_r = lax.broadcasted_iota(jnp.int32, (GRID_W, 128), 1) < HD
    kvk = [kb_sc[plane] for plane in range(2)]
    kvv = [vb_sc[plane] for plane in range(2)]

    def tile_b(i):
        rows = pl.ds(pl.multiple_of(i * Q_TILE, Q_TILE), Q_TILE)
        pending = {}

        def head(hh):
            j, e = divmod(hh, 2)
            pair = slice(j * 128, (j + 1) * 128)
            plane = 0 if j == e else 1

            def store(o):
                if e == 0:
                    pending[j] = o
                else:
                    ycat_ref[rows, GW + j * 128:GW + (j + 1) * 128] = jnp.where(lo_t, pending[j], o)

            return _attend_stages(lambda: [_dot_nt(qb_sc[e, rows, pair], kvk[plane])],
                                  [kvv[plane]], store)

        return _interleave(*[head(hh) for hh in range(4)])

    n_loc = NAT_WR * GRID_W
    qcol = lax.broadcasted_iota(jnp.int32, (GRID_W, n_loc), 0)
    kcol = lax.broadcasted_iota(jnp.int32, (GRID_W, n_loc), 1) & (GRID_W - 1)
    cstart = jnp.clip(qcol - NAT_WC // 2, 0, GRID_W - NAT_WC)
    in_win = (kcol >= cstart) & (kcol < cstart + NAT_WC)

    def row_block(r):
        rs = jnp.clip(r - NAT_WR // 2, 0, GRID_R - NAT_WR)
        didx = rs - r + (NAT_WR - 1)
        qrows = pl.ds(pl.multiple_of(r * GRID_W, GRID_W), GRID_W)
        krows = pl.ds(pl.multiple_of(rs * GRID_W, GRID_W), n_loc)

        pending = {}

        def head(hh):
            j, e = divmod(hh, 2)
            pair = slice(j * 128, (j + 1) * 128)

            def scores():
                qh = qc_sc[e, qrows, pair]
                s_loc = _dot_nt(qh, kc_sc[krows, pair])
                s_loc = jnp.where(in_win, s_loc + bias_ref[hh, didx], NEG_INF)
                return [s_loc, _dot_nt(qh, kc_sc[T:T + PAST, pair])]

            def store(o):
                if e == 0:
                    pending[j] = o
                else:
                    ycat_ref[qrows, 2 * GW + j * 128:2 * GW + (j + 1) * 128] = jnp.where(
                        lo_r, pending[j], o)

            return _attend_stages(scores, [vc_sc[krows, pair], vc_sc[T:T + PAST, pair]], store)

        return _interleave(*[head(hh) for hh in range(4)])

    rows_per_tile = Q_TILE // GRID_W

    def fused_tile(i):
        def b_then_d():
            yield from tile_b(i)
            yield from _interleave(*[tile(i) for tile in ret_tiles])

        def c_rows():
            for rb in range(rows_per_tile):
                yield from row_block(i * rows_per_tile + rb)

        _run(_interleave(b_then_d(), c_rows()))

    _loop(T // Q_TILE, fused_tile)

    y = _dot(ycat_ref[...].astype(bf16), wout_ref[...])
    o_ref[...] = x_ref[...] + gt_ref[...] * y


def _layer_specs(l, mod_row):
    const = dict(pipeline_mode=pl.Buffered(1))

    def lspec(shape):
        nd = len(shape)
        return pl.BlockSpec((None,) + shape, lambda i: (l,) + (0,) * nd, **const)

    def mod_spec(j):
        return pl.BlockSpec((None, None, None, 1, D), lambda i: (l, j, mod_row(i), 0, 0))

    return [
        mod_spec(3), mod_spec(4), mod_spec(5),
        pl.BlockSpec((None, None, 1, D), lambda i: (l, 1, 0, 0)),
        lspec((D, IN_COLS)), lspec((D, D)),
        lspec((4, GW)), lspec((1, GW)), lspec((GW, 4 * GW)), lspec((1, 4 * GW)), lspec((2, GW)),
        lspec((1, GW)), lspec((1, 128)), lspec((1, GW)), lspec((1, GW)),
        lspec((8, 1, 1)), lspec((1, GW)),
    ]


def _ctx_mixer(x, mods, norm_g, lw, l, prev):
    T = T_CTX

    def layer_spec(*tail):
        return pl.BlockSpec((None, None) + tail, lambda i: (i, l) + (0,) * len(tail))

    tails = [(T, 128), (T, 128), (T, GW), (T, GW), (2, GW), (8 * HD, HD)]
    operands = [x, mods, mods, mods, norm_g, lw["w_in"], lw["w_out"],
                lw["conv_w"], lw["conv_b"], lw["wg"], lw["bg"], lw["lam"],
                lw["qnb"], lw["knb"], lw["qnc"], lw["knc"], lw["dec"], lw["gn"]]
    assert len(operands) == N_CTX_IN
    aliases = {0: 0}
    prev_specs = []
    if prev is not None:
        prev_specs = [pl.BlockSpec(memory_space=pl.ANY)] * len(prev)
        aliases.update({N_CTX_IN + k: 1 + k for k in range(len(prev))})
        operands = operands + list(prev)

    outs = pl.pallas_call(
        _ctx_mixer_kernel,
        grid=(N_CTX,),
        in_specs=[pl.BlockSpec((T, D), lambda i: (i, 0))] + _layer_specs(l, lambda i: 0)
        + prev_specs,
        out_specs=[pl.BlockSpec((T, D), lambda i: (i, 0))] + [layer_spec(*t) for t in tails],
        out_shape=[jax.ShapeDtypeStruct((M_ALL, D), f32)]
        + [jax.ShapeDtypeStruct((N_CTX, DEPTH) + t, f32) for t in tails],
        scratch_shapes=[
            pltpu.VMEM((T, D), bf16), pltpu.VMEM((T, D), f32),
            pltpu.VMEM((T, GW), bf16), pltpu.VMEM((T, GW), bf16), pltpu.VMEM((T, GW), bf16),
            pltpu.VMEM((T, GW), f32), pltpu.VMEM((T, GW), f32),
            pltpu.VMEM((4, Q_TILE, Q_TILE), f32),
        ],
        input_output_aliases=aliases,
        compiler_params=pltpu.CompilerParams(
            dimension_semantics=("arbitrary",), vmem_limit_bytes=VMEM_LIMIT),
        name="ctx_mixer",
    )(*operands)
    return outs[0], tuple(outs[1:])


def _lat_mixer(x, mods, norm_g, lw, l, cos, sin, caches):
    T = T_LAT
    blk0 = M_CTX // T
    const = dict(pipeline_mode=pl.Buffered(1))
    cbk, cbv, cck, ccv, lru0, ret0 = caches

    def cache_spec(w):
        return pl.BlockSpec((None, None, PAST, w), lambda i: (i, l, 0, 0))

    return pl.pallas_call(
        _lat_mixer_kernel,
        grid=(N_LAT,),
        in_specs=[pl.BlockSpec((T, D), lambda i: (blk0 + i, 0))]
        + _layer_specs(l, lambda i: 1 + i)
        + [
            pl.BlockSpec((T, 128), lambda i: (0, 0), **const),
            pl.BlockSpec((T, 128), lambda i: (0, 0), **const),
            pl.BlockSpec((None, 4, NAT_WR, GRID_W, NAT_WR * GRID_W),
                         lambda i: (l, 0, 0, 0, 0), **const),
            cache_spec(128), cache_spec(128), cache_spec(GW), cache_spec(GW),
            pl.BlockSpec((None, None, 2, GW), lambda i: (i, l, 0, 0)),
            pl.BlockSpec((None, None, 8 * HD, HD), lambda i: (i, l, 0, 0)),
        ],
        out_specs=pl.BlockSpec((T, D), lambda i: (blk0 + i, 0), **const),
        out_shape=jax.ShapeDtypeStruct((M_ALL, D), f32),
        scratch_shapes=[
            pltpu.VMEM((T, D), bf16),
            pltpu.VMEM((T, GW), bf16), pltpu.VMEM((T, GW), bf16), pltpu.VMEM((T, GW), bf16),
            pltpu.VMEM((T, GW), f32), pltpu.VMEM((T, GW), f32),
            pltpu.VMEM((4 * (2 * (T // Q_TILE) - 1), Q_TILE, Q_TILE), f32),
            pltpu.VMEM((2, T, GW), bf16),
            pltpu.VMEM((2, T + PAST, 128), bf16), pltpu.VMEM((2, T + PAST, 128), bf16),
            pltpu.VMEM((2, T, GW), bf16),
            pltpu.VMEM((T + PAST, GW), bf16), pltpu.VMEM((T + PAST, GW), bf16),
        ],
        input_output_aliases={0: 0},
        compiler_params=pltpu.CompilerParams(
            dimension_semantics=("arbitrary",), vmem_limit_bytes=VMEM_LIMIT),
        name="lat_mixer",
    )(x, mods, mods, mods, norm_g, lw["w_in"], lw["w_out"],
      lw["conv_w"], lw["conv_b"], lw["wg"], lw["bg"], lw["lam"],
      lw["qnb"], lw["knb"], lw["qnc"], lw["knc"], lw["dec"], lw["gn"],
      cos, sin, lw["nat_bias"], cbk, cbv, cck, ccv, lru0, ret0)


def _block_diag(w):
    eye = jnp.eye(4, dtype=w.dtype)
    return jnp.einsum("lncd,nm->lncmd", w, eye).reshape(DEPTH, GW, GW)


def _rope_tables():
    t = jnp.arange(T_LAT)
    row = (t // GRID_W).astype(f32)
    col = (t % GRID_W).astype(f32)
    n_freq = HD // 4
    inv = ROPE_BASE ** (-jnp.arange(n_freq, dtype=f32) / n_freq)
    ang = jnp.concatenate([row[:, None] * inv, col[:, None] * inv], axis=-1)
    cos = jnp.repeat(jnp.cos(ang), 2, axis=-1)
    sin = jnp.repeat(jnp.sin(ang), 2, axis=-1)
    sign = jnp.where(jnp.arange(HD) % 2 == 0, -1.0, 1.0).astype(f32)
    return jnp.tile(cos, (1, 2)), jnp.tile(sin * sign, (1, 2))


def _nat_bias_windows(nat_bias):
    pad = GRID_W - NAT_WC
    ext = jnp.concatenate([jnp.repeat(nat_bias[..., :1], pad, -1), nat_bias,
                           jnp.repeat(nat_bias[..., -1:], pad + 1, -1)], axis=-1)
    t = jnp.tile(ext, (1, 1, 1, GRID_W))[..., :GRID_W * 127]
    t = t.reshape(DEPTH, 4, 2 * NAT_WR - 1, GRID_W, 127)[..., GRID_W - 1:]
    tab = jnp.stack([t[:, :, d:d + NAT_WR] for d in range(NAT_WR)], axis=2)
    tab = tab.transpose(0, 1, 2, 4, 3, 5)
    return tab.reshape(DEPTH, 4, NAT_WR, GRID_W, NAT_WR * GRID_W)


def kernel(x_prompt, x_sample, cache_b_k, cache_b_v, cache_c_k, cache_c_v, state_lru, state_ret, c, c_ctx, w_mod, b_mod, norm_g, ffn_w_in, ffn_w_out, w_in, w_out, conv_w, conv_b, lru_w_r, lru_b_r, lru_w_i, lru_b_i, lru_lambda, gqa_qn, gqa_kn, nat_qn, nat_kn, nat_bias, ret_decay, ret_gn):
    cvec = jnp.concatenate([c_ctx[None, :], c, jnp.zeros((8 - 1 - N_LAT, D), f32)], axis=0)
    mods = _modulation(cvec, w_mod, b_mod).reshape(DEPTH, N_MOD, 8, 1, D)
    ng = norm_g.reshape(DEPTH, 3, 1, D)

    wi = ffn_w_in.astype(bf16)
    wo = ffn_w_out.astype(bf16)
    lw = dict(
        w_in=w_in.astype(bf16),
        w_out=w_out.astype(bf16),
        conv_w=conv_w,
        conv_b=conv_b.reshape(DEPTH, 1, GW),
        wg=jnp.concatenate([_block_diag(lru_w_r[:, 0]), _block_diag(lru_w_i[:, 0]),
                            _block_diag(lru_w_r[:, 1]), _block_diag(lru_w_i[:, 1])],
                           axis=-1).astype(bf16),
        bg=jnp.concatenate([lru_b_r[:, 0], lru_b_i[:, 0], lru_b_r[:, 1], lru_b_i[:, 1]],
                           axis=-1).reshape(DEPTH, 1, 4 * GW),
        lam=lru_lambda,
        qnb=jnp.tile(gqa_qn, (1, 4)).reshape(DEPTH, 1, GW),
        knb=jnp.tile(gqa_kn, (1, 2)).reshape(DEPTH, 1, 128),
        qnc=jnp.tile(nat_qn, (1, 4)).reshape(DEPTH, 1, GW),
        knc=jnp.tile(nat_kn, (1, 4)).reshape(DEPTH, 1, GW),
        dec=ret_decay.reshape(DEPTH, 8, 1, 1),
        gn=ret_gn.reshape(DEPTH, 1, GW),
        nat_bias=_nat_bias_windows(nat_bias),
    )
    cos, sin = _rope_tables()
    caches = (cache_b_k.reshape(N_LAT, DEPTH, PAST, 128), cache_b_v.reshape(N_LAT, DEPTH, PAST, 128),
              cache_c_k.reshape(N_LAT, DEPTH, PAST, GW), cache_c_v.reshape(N_LAT, DEPTH, PAST, GW),
              state_lru, state_ret.reshape(N_LAT, DEPTH, 8 * HD, HD))

    x = (x_prompt.reshape(M_CTX, D), x_sample.reshape(M_LAT, D))
    new = None
    for l in range(DEPTH):
        x = _ffn(x, mods, ng, wi, wo, l, 0)
        x, new = _ctx_mixer(x, mods, ng, lw, l, new)
        x = _lat_mixer(x, mods, ng, lw, l, cos, sin, caches)
        x = _ffn(x, mods, ng, wi, wo, l, 1, split_out=(l == DEPTH - 1))

    new_bk, new_bv, new_ck, new_cv, new_lru, new_ret = new
    return (x[0].reshape(N_CTX, T_CTX, D), x[1].reshape(N_LAT, T_LAT, D),
            new_bk.reshape(N_CTX, DEPTH, T_CTX, 2, HD), new_bv.reshape(N_CTX, DEPTH, T_CTX, 2, HD),
            new_ck.reshape(N_CTX, DEPTH, T_CTX, 4, HD), new_cv.reshape(N_CTX, DEPTH, T_CTX, 4, HD),
            new_lru, new_ret.reshape(N_CTX, DEPTH, 2, 4, HD, HD))
```
